```python
import jax
import jax.numpy as jnp
from jax import lax
import numpy as np

D_MODEL = 1024
BATCH = 32
SEQ = 256
DEPTH = 2
DEC_BATCH = 2
DEC_SEQ = 4096
PAST_LEN = 512

GRID_W = 64
POS_BASE = 10000.0
N_MIXERS = 4
D_MIX = D_MODEL
GROUP_W = D_MIX // N_MIXERS
N_DIRS = 2
CHUNK = 16
EPS = 1e-6
N_MOD = 6
GLA_HEADS = 4
GLA_DK = GROUP_W // GLA_HEADS
GLA_DV = GROUP_W // GLA_HEADS
GLA_LOWRANK = 16
GLA_GATE_TEMP = 16.0
RG_BLOCKS = 4
RG_BLOCK_W = GROUP_W // RG_BLOCKS
RG_CONV_W = 4
RG_C = 8.0
HY_ORDER = 2
HY_SHORT_W = 3
HY_POS_BANDS = 16
HY_POS_DIM = 2 * HY_POS_BANDS + 1
HY_FFN_W = 64
HY_DECAY_MIN = 3.07
HY_DECAY_MAX = 15.35
HG_HEADS = 4
HG_DK = GROUP_W // HG_HEADS
HG_DV = GROUP_W // HG_HEADS
D_FF = -(-(8 * D_MODEL) // (3 * 256)) * 256

IN_SPLITS = (GROUP_W, GROUP_W, GROUP_W, GROUP_W, GLA_LOWRANK,
             GROUP_W, GROUP_W,
             (HY_ORDER + 1) * GROUP_W,
             GROUP_W, GROUP_W, GROUP_W, GROUP_W, GROUP_W)
IN_OFFSETS = tuple(int(o) for o in np.cumsum(IN_SPLITS)[:-1])
D_IN = int(sum(IN_SPLITS))

kernel_name = 'hybrid_diffusion_gla_rglru_hyena_hgrn2_step'

F32 = jnp.float32


def rmsnorm(x, g):
    xf = x.astype(F32)
    y = xf * lax.rsqrt(jnp.mean(xf * xf, axis=-1, keepdims=True) + EPS)
    return (y * g.astype(F32)).astype(x.dtype)


def split_heads(t, n_heads):
    return t.reshape(t.shape[:-1] + (n_heads, t.shape[-1] // n_heads))


def head_rmsnorm_gate(o, gain, gate):
    o = o * lax.rsqrt(jnp.mean(o * o, axis=-1, keepdims=True) + EPS)
    o = o.reshape(o.shape[:2] + (-1,)) * gain.astype(F32)
    return (o * jax.nn.silu(gate.astype(F32))).astype(gate.dtype)


def depthwise_conv(x, w, b, pad_left):
    width, ch = w.shape
    y = lax.conv_general_dilated(x, w[:, None, :], (1,), [(pad_left, width - 1 - pad_left)],
                                 dimension_numbers=('NWC', 'WIO', 'NWC'), feature_group_count=ch)
    return y + b


def grid_position_embedding(n_tokens, dim):
    rows = n_tokens // GRID_W
    row = jnp.broadcast_to(jnp.arange(rows, dtype=F32)[:, None], (rows, GRID_W)).reshape(-1)
    col = jnp.broadcast_to(jnp.arange(GRID_W, dtype=F32)[None, :], (rows, GRID_W)).reshape(-1)
    quarter = dim // 4
    omega = 1.0 / (POS_BASE ** (jnp.arange(quarter, dtype=F32) / quarter))

    def enc(pos):
        ang = pos[:, None] * omega[None, :]
        return jnp.concatenate([jnp.sin(ang), jnp.cos(ang)], axis=-1)

    return jnp.concatenate([enc(row), enc(col)], axis=-1)


def chunked_gated_state(q, k, v, log_a, s0):
    B, L, H, _ = q.shape
    V = v.shape[-1]
    n = L // CHUNK

    def chunks(t):
        return t.astype(F32).reshape(B, n, CHUNK, H, t.shape[-1]).transpose(1, 0, 3, 2, 4)

    qc, kc, vc, ac = chunks(q), chunks(k), chunks(v), chunks(log_a)
    b = jnp.cumsum(ac, axis=3)
    causal = jnp.tril(jnp.ones((CHUNK, CHUNK), dtype=bool))
    rel = jnp.where(causal[:, :, None], b[:, :, :, :, None, :] - b[:, :, :, None, :, :], -jnp.inf)
    scores = jnp.einsum('nbhtk,nbhsk,nbhtsk->nbhts', qc, kc, jnp.exp(rel))
    o_intra = jnp.einsum('nbhts,nbhsv->nbhtv', scores, vc)
    b_end = b[:, :, :, -1:, :]
    q_dec = qc * jnp.exp(b)
    kv = jnp.einsum('nbhck,nbhcv->nbhkv', kc * jnp.exp(b_end - b), vc)
    decay_end = jnp.exp(b_end[:, :, :, 0, :])

    def step(S, inp):
        qd, dec, kv_c = inp
        o = jnp.einsum('bhck,bhkv->bhcv', qd, S)
        return dec[..., None] * S + kv_c, o

    s_final, o_inter = lax.scan(step, s0.astype(F32), (q_dec, decay_end, kv))
    o = (o_intra + o_inter).transpose(1, 0, 3, 2, 4).reshape(B, L, H, V)
    return o, s_final


def diag_linear_scan(a, u, h0):
    def combine(e1, e2):
        a1, b1 = e1
        a2, b2 = e2
        return a1 * a2, a2 * b1 + b2

    a_cum, u_cum = lax.associative_scan(combine, (a, u), axis=1)
    h = a_cum * h0[:, None, :] + u_cum
    return h, h[:, -1]


def gla_mixer(q, k, v, g, lr, w_gate, b_gate, norm_g, s0):
    qh = split_heads(q, GLA_HEADS) * GLA_DK ** -0.5
    kh = split_heads(k, GLA_HEADS)
    vh = split_heads(v, GLA_HEADS)
    outs, finals = [], []
    for d in range(N_DIRS):
        log_a = jax.nn.log_sigmoid((lr @ w_gate[d] + b_gate[d]).astype(F32)) / GLA_GATE_TEMP
        seq = (qh, kh, vh, split_heads(log_a, GLA_HEADS))
        if d == 1:
            seq = tuple(jnp.flip(t, axis=1) for t in seq)
        o, s_f = chunked_gated_state(*seq, s0[:, d])
        outs.append(o if d == 0 else jnp.flip(o, axis=1))
        finals.append(s_f)
    return head_rmsnorm_gate(outs[0] + outs[1], norm_g, g), jnp.stack(finals, axis=1)


def rglru_mixer(xb, gb, conv_w, conv_b, w_a, b_a, w_x, b_x, lam, h0):
    B, L, C = xb.shape
    xc = depthwise_conv(xb, conv_w, conv_b, RG_CONV_W // 2)
    xblk = split_heads(xc, RG_BLOCKS)
    xf = xc.astype(F32)
    outs, finals = [], []
    for d in range(N_DIRS):
        r = jax.nn.sigmoid((jnp.einsum('blhi,hij->blhj', xblk, w_a[d]).reshape(B, L, C) + b_a[d]).astype(F32))
        i = jax.nn.sigmoid((jnp.einsum('blhi,hij->blhj', xblk, w_x[d]).reshape(B, L, C) + b_x[d]).astype(F32))
        log_a = -RG_C * r * jax.nn.softplus(-lam[d].astype(F32))
        a = jnp.exp(log_a)
        u = jnp.sqrt(-jnp.expm1(2.0 * log_a)) * (i * xf)
        if d == 1:
            a, u = jnp.flip(a, axis=1), jnp.flip(u, axis=1)
        h, h_last = diag_linear_scan(a, u, h0[:, d].astype(F32))
        outs.append(h if d == 0 else jnp.flip(h, axis=1))
        finals.append(h_last)
    y = (outs[0] + outs[1]) * jax.nn.gelu(gb.astype(F32))
    return y.astype(xb.dtype), jnp.stack(finals, axis=1)


def hyena_filters(L, w1, b1, w2, b2, w3, decay_rate):
    pos = jnp.arange(L, dtype=F32)
    t = pos / (L - 1)
    ang = (2.0 * jnp.pi * pos / L)[:, None] * jnp.linspace(1e-4, HY_POS_BANDS - 1, HY_POS_BANDS, dtype=F32)[None, :]
    pe = jnp.concatenate([t[:, None], jnp.cos(ang), -jnp.sin(ang)], axis=-1)
    h = jnp.sin(pe @ w1.astype(F32) + b1.astype(F32))
    h = jnp.sin(h @ w2.astype(F32) + b2.astype(F32))
    h = h @ w3.astype(F32)
    half = L // 2
    dist = jnp.abs(pos - half) / half
    h = h * jnp.exp(-dist[:, None] * decay_rate.astype(F32)[None, :])
    return h / jnp.sum(jnp.abs(h), axis=0, keepdims=True)


def fft_long_conv(z, h):
    L = z.shape[1]
    n_fft = 2 * L
    zf = jnp.fft.rfft(z, n=n_fft, axis=1)
    hf = jnp.fft.rfft(h, n=n_fft, axis=0)
    full = jnp.fft.irfft(zf * hf[None], n=n_fft, axis=1)
    return full[:, L // 2: L // 2 + L]


def hyena_mixer(proj, conv_w, conv_b, w1, b1, w2, b2, w3, decay_rate, skip):
    L = proj.shape[1]
    uc = depthwise_conv(proj, conv_w, conv_b, HY_SHORT_W // 2).astype(F32)
    v, x1, x2 = jnp.split(uc, HY_ORDER + 1, axis=-1)
    filt = hyena_filters(L, w1, b1, w2, b2, w3, decay_rate)
    skip = skip.astype(F32)
    z = v
    for n, gate in enumerate((x1, x2)):
        sl = slice(n * GROUP_W, (n + 1) * GROUP_W)
        z = gate * (fft_long_conv(z, filt[:, sl]) + skip[sl] * z)
    return z.astype(proj.dtype)


def hgrn2_mixer(q, f_fwd, f_bwd, i, g, lb, norm_g, s0):
    qh = split_heads(jax.nn.silu(q.astype(F32)), HG_HEADS)
    vh = split_heads(i, HG_HEADS)
    log_lb, log_1m_lb = jnp.log(lb), jnp.log1p(-lb)
    outs, finals = [], []
    for d, fl in enumerate((f_fwd, f_bwd)):
        fl = fl.astype(F32)
        log_f = jnp.logaddexp(log_lb, log_1m_lb + jax.nn.log_sigmoid(fl))
        k = (1.0 - lb) * jax.nn.sigmoid(-fl)
        seq = (qh, split_heads(k, HG_HEADS), vh, split_heads(log_f, HG_HEADS))
        if d == 1:
            seq = tuple(jnp.flip(t, axis=1) for t in seq)
        o, s_f = chunked_gated_state(*seq, s0[:, d])
        outs.append(o if d == 0 else jnp.flip(o, axis=1))
        finals.append(s_f)
    return head_rmsnorm_gate(outs[0] + outs[1], norm_g, g), jnp.stack(finals, axis=1)


def token_mixers(u, p, s_gla, s_rg, s_hg):
    proj = u @ p['w_in']
    (a_q, a_k, a_v, a_g, a_lr, b_x, b_g, c_in, d_q, d_ff, d_fb, d_i, d_g) = jnp.split(proj, IN_OFFSETS, axis=-1)
    y_a, st_a = gla_mixer(a_q, a_k, a_v, a_g, a_lr, p['gla_w_gate'], p['gla_b_gate'], p['gla_norm_g'], s_gla)
    y_b, st_b = rglru_mixer(b_x, b_g, p['rg_conv_w'], p['rg_conv_b'], p['rg_w_a'], p['rg_b_a'],
                            p['rg_w_x'], p['rg_b_x'], p['rg_lambda'], s_rg)
    y_c = hyena_mixer(c_in, p['hy_conv_w'], p['hy_conv_b'], p['hy_w1'], p['hy_b1'], p['hy_w2'],
                      p['hy_b2'], p['hy_w3'], p['hy_decay'], p['hy_skip'])
    y_d, st_d = hgrn2_mixer(d_q, d_ff, d_fb, d_i, d_g, p['hg_lb'], p['hg_norm_g'], s_hg)
    mixed = jnp.concatenate([y_a, y_b, y_c, y_d], axis=-1) @ p['w_out']
    return mixed, (st_a.astype(u.dtype), st_b.astype(u.dtype), st_d.astype(u.dtype))


def trunk_layer(x, mod, p, s_gla, s_rg, s_hg):
    sh1, sc1, g1, sh2, sc2, g2 = jnp.split(mod, N_MOD, axis=-1)
    u = rmsnorm(x, p['norm1_g']) * (1 + sc1) + sh1
    mixed, finals = token_mixers(u, p, s_gla, s_rg, s_hg)
    x = x + g1 * mixed
    u = rmsnorm(x, p['norm2_g']) * (1 + sc2) + sh2
    hidden = jax.nn.silu(u @ p['ffn_w1']) * (u @ p['ffn_w3'])
    x = x + g2 * (hidden @ p['ffn_w2'])
    return x, finals


def setup_inputs(seed: int = 0) -> dict:
    key = jax.random.key(seed)
    keys = iter(jax.random.split(key, 64))

    def nrm(shape, scale=1.0):
        return scale * jax.random.normal(next(keys), shape, F32)

    def gain(shape):
        return 1.0 + nrm(shape, 0.02)

    D, W, NL = D_MODEL, GROUP_W, DEPTH
    u = jax.random.uniform(next(keys), (NL, N_DIRS, W), F32, 0.9, 0.999)
    a_base = u ** (1.0 / RG_C)
    rg_lambda = jnp.log(a_base) - jnp.log1p(-a_base)
    hy_decay = (jnp.tile(jnp.linspace(HY_DECAY_MIN, HY_DECAY_MAX, W, dtype=F32), (NL, HY_ORDER))
                + nrm((NL, HY_ORDER * W), 0.1))
    return {
        'x_prompt': nrm((BATCH, SEQ, D)),
        'x_sample': nrm((DEC_BATCH, DEC_SEQ, D)),
        'state_gla': nrm((DEC_BATCH, NL, N_DIRS, GLA_HEADS, GLA_DK, GLA_DV), 0.5),
        'state_rglru': nrm((DEC_BATCH, NL, N_DIRS, W), 0.5),
        'state_hgrn': nrm((DEC_BATCH, NL, N_DIRS, HG_HEADS, HG_DK, HG_DV), 0.5),
        'c': nrm((DEC_BATCH, D)),
        'c_ctx': nrm((D,)),
        'norm1_g': gain((NL, D)),
        'norm2_g': gain((NL, D)),
        'final_norm_g': gain((D,)),
        'w_mod': nrm((NL, D, N_MOD * D), 0.5 * D ** -0.5),
        'b_mod': nrm((NL, N_MOD * D), 0.02),
        'w_in': nrm((NL, D, D_IN), D ** -0.5),
        'w_out': nrm((NL, D_MIX, D), D_MIX ** -0.5),
        'gla_w_gate': nrm((NL, N_DIRS, GLA_LOWRANK, W), GLA_LOWRANK ** -0.5),
        'gla_b_gate': nrm((NL, N_DIRS, W), 0.1),
        'gla_norm_g': gain((NL, W)),
        'rg_conv_w': nrm((NL, RG_CONV_W, W), RG_CONV_W ** -0.5),
        'rg_conv_b': nrm((NL, W), 0.02),
        'rg_w_a': nrm((NL, N_DIRS, RG_BLOCKS, RG_BLOCK_W, RG_BLOCK_W), RG_BLOCK_W ** -0.5),
        'rg_b_a': nrm((NL, N_DIRS, W), 0.1),
        'rg_w_x': nrm((NL, N_DIRS, RG_BLOCKS, RG_BLOCK_W, RG_BLOCK_W), RG_BLOCK_W ** -0.5),
        'rg_b_x': nrm((NL, N_DIRS, W), 0.1),
        'rg_lambda': rg_lambda,
        'hy_conv_w': nrm((NL, HY_SHORT_W, (HY_ORDER + 1) * W), HY_SHORT_W ** -0.5),
        'hy_conv_b': nrm((NL, (HY_ORDER + 1) * W), 0.02),
        'hy_w1': nrm((NL, HY_POS_DIM, HY_FFN_W), HY_POS_DIM ** -0.5),
        'hy_b1': nrm((NL, HY_FFN_W), 0.1),
        'hy_w2': nrm((NL, HY_FFN_W, HY_FFN_W), HY_FFN_W ** -0.5),
        'hy_b2': nrm((NL, HY_FFN_W), 0.1),
        'hy_w3': nrm((NL, HY_FFN_W, HY_ORDER * W), HY_FFN_W ** -0.5),
        'hy_decay': hy_decay,
        'hy_skip': nrm((NL, HY_ORDER * W)),
        'hg_lower': nrm((NL, W), 0.1),
        'hg_norm_g': gain((NL, W)),
        'ffn_w1': nrm((NL, D, D_FF), D ** -0.5),
        'ffn_w3': nrm((NL, D, D_FF), D ** -0.5),
        'ffn_w2': nrm((NL, D_FF, D), D_FF ** -0.5),
    }


def reference(x_prompt, x_sample, state_gla, state_rglru, state_hgrn, c, c_ctx,
              norm1_g, norm2_g, final_norm_g, w_mod, b_mod, w_in, w_out,
              gla_w_gate, gla_b_gate, gla_norm_g,
              rg_conv_w, rg_conv_b, rg_w_a, rg_b_a, rg_w_x, rg_b_x, rg_lambda,
              hy_conv_w, hy_conv_b, hy_w1, hy_b1, hy_w2, hy_b2, hy_w3, hy_decay, hy_skip,
              hg_lower, hg_norm_g, ffn_w1, ffn_w3, ffn_w2):
    n_ctx_req = x_prompt.shape[0]
    hg_lb = jnp.cumsum(jax.nn.softmax(hg_lower.astype(F32), axis=0), axis=0)
    hg_lb = hg_lb - hg_lb[0:1]
    xp = x_prompt
    xs = x_sample + grid_position_embedding(x_sample.shape[1], x_sample.shape[2]).astype(x_sample.dtype)[None]
    zero_gla = jnp.zeros((n_ctx_req, N_DIRS, GLA_HEADS, GLA_DK, GLA_DV), x_prompt.dtype)
    zero_rg = jnp.zeros((n_ctx_req, N_DIRS, GROUP_W), x_prompt.dtype)
    zero_hg = jnp.zeros((n_ctx_req, N_DIRS, HG_HEADS, HG_DK, HG_DV), x_prompt.dtype)
    gla_states, rg_states, hg_states = [], [], []
    for l in range(DEPTH):
        p = dict(norm1_g=norm1_g[l], norm2_g=norm2_g[l], w_in=w_in[l], w_out=w_out[l],
                 gla_w_gate=gla_w_gate[l], gla_b_gate=gla_b_gate[l], gla_norm_g=gla_norm_g[l],
                 rg_conv_w=rg_conv_w[l], rg_conv_b=rg_conv_b[l], rg_w_a=rg_w_a[l], rg_b_a=rg_b_a[l],
                 rg_w_x=rg_w_x[l], rg_b_x=rg_b_x[l], rg_lambda=rg_lambda[l],
                 hy_conv_w=hy_conv_w[l], hy_conv_b=hy_conv_b[l], hy_w1=hy_w1[l], hy_b1=hy_b1[l],
                 hy_w2=hy_w2[l], hy_b2=hy_b2[l], hy_w3=hy_w3[l], hy_decay=hy_decay[l], hy_skip=hy_skip[l],
                 hg_lb=hg_lb[l], hg_norm_g=hg_norm_g[l],
                 ffn_w1=ffn_w1[l], ffn_w3=ffn_w3[l], ffn_w2=ffn_w2[l])
        mod_ctx = (jax.nn.silu(c_ctx)[None, :] @ w_mod[l] + b_mod[l])[:, None, :]
        mod_lat = (jax.nn.silu(c) @ w_mod[l] + b_mod[l])[:, None, :]
        xp, (sg, sr, sh) = trunk_layer(xp, mod_ctx, p, zero_gla, zero_rg, zero_hg)
        xs, _ = trunk_layer(xs, mod_lat, p, state_gla[:, l], state_rglru[:, l], state_hgrn[:, l])
        gla_states.append(sg)
        rg_states.append(sr)
        hg_states.append(sh)
    y_prompt = rmsnorm(xp, final_norm_g)
    y_sample = rmsnorm(xs, final_norm_g)
    new_state_gla = jnp.stack(gla_states, axis=1)
    new_state_rglru = jnp.stack(rg_states, axis=1)
    new_state_hgrn = jnp.stack(hg_states, axis=1)
    return (y_prompt, y_sample, new_state_gla, new_state_rglru, new_state_hgrn)
```

```python
import functools
import math

import jax
import jax.numpy as jnp
from jax import lax
from jax.experimental import pallas as pl
from jax.experimental.pallas import tpu as pltpu

F32 = jnp.float32
BF16 = jnp.bfloat16

D_MODEL = 1024
N_MOD = 6
GROUP_W = 256
N_HEADS = 4
HEAD_D = GROUP_W // N_HEADS
GLA_LOWRANK = 16
GLA_GATE_TEMP = 16.0
RG_C = 8.0
RG_CONV_W = 4
HY_POS_BANDS = 16
HY_FFN_W = 64
D_FF = 2816
EPS = 1e-6
GRID_W = 64
POS_BASE = 10000.0

LANE = 128
SUBLANE = 8
VMEM_LIMIT = 56 * 1024 * 1024

SUB = 16
LC = 256
N_SUB = LC // SUB
LR_PAD = LANE
W_GLA = 4 * GROUP_W + LR_PAD
W_RG = 2 * GROUP_W
W_HY = 3 * GROUP_W
W_HG = 5 * GROUP_W
W_PROJ = W_GLA + W_RG + W_HY + W_HG
TM = 512
TF = D_FF // 2
TN_MOD = 512
RG_HALF = GROUP_W // 2
RG_ROWS = 256
FREQ_TILE = 256


def _dot(a, b):
    return jnp.dot(a, b, preferred_element_type=F32)


def _split(x):
    hi = x.astype(BF16)
    lo = (x - hi.astype(F32)).astype(BF16)
    return hi, lo


def _dot_x2(x, w):
    hi, lo = _split(x)
    return _dot(hi, w) + _dot(lo, w)


def _dot_w2(w, x):
    hi, lo = _split(x)
    return _dot(w, hi) + _dot(w, lo)


def _dot3(a, b):
    ah, al = _split(a)
    bh, bl = _split(b)
    return _dot(ah, bh) + _dot(al, bh) + _dot(ah, bl)


def _sigmoid(x):
    return 1.0 / (1.0 + jnp.exp(-x))


def _silu(x):
    return x * _sigmoid(x)


def _log_sigmoid(x):
    return jnp.minimum(x, 0.0) - jnp.log1p(jnp.exp(-jnp.abs(x)))


def _softplus(x):
    return jnp.maximum(x, 0.0) + jnp.log1p(jnp.exp(-jnp.abs(x)))


def _rms(x):
    return x * lax.rsqrt(jnp.mean(x * x, axis=-1, keepdims=True) + EPS)


def _params(sem, vmem=VMEM_LIMIT):
    return pltpu.CompilerParams(dimension_semantics=sem, vmem_limit_bytes=vmem)


def _mod_kernel(c_ref, w_ref, b_ref, o_ref):
    c = c_ref[...]
    o_ref[0] = _dot3(_silu(c), w_ref[0]) + b_ref[0]


def _modulation(cvec, w_mod, b_mod):
    depth = w_mod.shape[0]
    n = N_MOD * D_MODEL
    return pl.pallas_call(
        _mod_kernel,
        out_shape=jax.ShapeDtypeStruct((depth, SUBLANE, n), F32),
        grid=(depth, n // TN_MOD),
        in_specs=[
            pl.BlockSpec((SUBLANE, D_MODEL), lambda l, j: (0, 0)),
            pl.BlockSpec((1, D_MODEL, TN_MOD), lambda l, j: (l, 0, j)),
            pl.BlockSpec((1, 1, TN_MOD), lambda l, j: (l, 0, j)),
        ],
        out_specs=pl.BlockSpec((1, SUBLANE, TN_MOD), lambda l, j: (l, 0, j)),
        compiler_params=_params(("parallel", "parallel")),
        name="modulation",
    )(cvec, w_mod, b_mod.reshape(depth, 1, n))


def _add_kernel(x_ref, p_ref, o_ref):
    o_ref[0] = x_ref[0] + p_ref[...]


def _add_pos(x, pos):
    b, l, d = x.shape
    return pl.pallas_call(
        _add_kernel,
        out_shape=jax.ShapeDtypeStruct(x.shape, F32),
        grid=(b, l // TM),
        in_specs=[pl.BlockSpec((1, TM, d), lambda i, j: (i, j, 0)),
                  pl.BlockSpec((TM, d), lambda i, j: (j, 0))],
        out_specs=pl.BlockSpec((1, TM, d), lambda i, j: (i, j, 0)),
        compiler_params=_params(("parallel", "parallel")),
        name="add_pos",
    )(x, pos)


def _proj_kernel(x_ref, mod_ref, g_ref, w_ref, oa_ref, ob_ref, oc_ref, od_ref):
    x = x_ref[0]
    m = mod_ref[0, 0]
    sh = m[:, 0:D_MODEL]
    sc = m[:, D_MODEL:2 * D_MODEL]
    u = _rms(x) * g_ref[...] * (1.0 + sc) + sh
    p = _dot(u.astype(BF16), w_ref[...])
    oa_ref[0] = p[:, 0:W_GLA]
    ob_ref[0] = p[:, W_GLA:W_GLA + W_RG]
    oc_ref[0] = p[:, W_GLA + W_RG:W_GLA + W_RG + W_HY]
    od_ref[0] = p[:, W_GLA + W_RG + W_HY:W_PROJ]


def _norm_proj(x, mod4, layer, row0, gain, w):
    bm, lm, d = x.shape
    widths = (W_GLA, W_RG, W_HY, W_HG)
    return pl.pallas_call(
        _proj_kernel,
        out_shape=[jax.ShapeDtypeStruct((bm, lm, wd), F32) for wd in widths],
        grid=(bm, lm // TM),
        in_specs=[
            pl.BlockSpec((1, TM, d), lambda i, j: (i, j, 0)),
            pl.BlockSpec((1, 1, 1, N_MOD * d), lambda i, j: (layer, row0 + i, 0, 0)),
            pl.BlockSpec((1, d), lambda i, j: (0, 0)),
            pl.BlockSpec((d, W_PROJ), lambda i, j: (0, 0)),
        ],
        out_specs=[pl.BlockSpec((1, TM, wd), lambda i, j: (i, j, 0)) for wd in widths],
        compiler_params=_params(("parallel", "parallel")),
        name="norm_proj",
    )(x, mod4, gain, w)


def _ffn_kernel(x_ref, ya_ref, yb_ref, yc_ref, yd_ref, mod_ref, g2_ref, gf_ref,
                wo_ref, w1_ref, w3_ref, w2_ref, o_ref, x1_s, u_s, acc_s, *, nf, final):
    f = pl.program_id(2)
    d = D_MODEL

    @pl.when(f == 0)
    def _():
        m = mod_ref[0, 0]
        g1 = m[:, 2 * d:3 * d]
        sh2 = m[:, 3 * d:4 * d]
        sc2 = m[:, 4 * d:5 * d]
        y = jnp.concatenate([ya_ref[0], yb_ref[0], yc_ref[0], yd_ref[0]], axis=-1)
        x1 = x_ref[0] + g1 * _dot(y.astype(BF16), wo_ref[...])
        x1_s[...] = x1
        u_s[...] = (_rms(x1) * g2_ref[...] * (1.0 + sc2) + sh2).astype(BF16)
        acc_s[...] = jnp.zeros_like(acc_s)

    u = u_s[...]
    h = _silu(_dot(u, w1_ref[...])) * _dot(u, w3_ref[...])
    acc_s[...] += _dot(h.astype(BF16), w2_ref[...])

    @pl.when(f == nf - 1)
    def _():
        g2 = mod_ref[0, 0][:, 5 * d:6 * d]
        xo = x1_s[...] + g2 * acc_s[...]
        if final:
            xo = _rms(xo) * gf_ref[...]
        o_ref[0] = xo


def _out_ffn(x, ys, mod4, layer, row0, g2, gf, wo, w1, w3, w2, final):
    bm, lm, d = x.shape
    nf = D_FF // TF
    tok = lambda wd: pl.BlockSpec((1, TM, wd), lambda i, j, f: (i, j, 0))
    return pl.pallas_call(
        functools.partial(_ffn_kernel, nf=nf, final=final),
        out_shape=jax.ShapeDtypeStruct(x.shape, F32),
        grid=(bm, lm // TM, nf),
        in_specs=[
            tok(d), tok(GROUP_W), tok(GROUP_W), tok(GROUP_W), tok(GROUP_W),
            pl.BlockSpec((1, 1, 1, N_MOD * d), lambda i, j, f: (layer, row0 + i, 0, 0)),
            pl.BlockSpec((1, d), lambda i, j, f: (0, 0)),
            pl.BlockSpec((1, d), lambda i, j, f: (0, 0)),
            pl.BlockSpec((d, d), lambda i, j, f: (0, 0)),
            pl.BlockSpec((d, TF), lambda i, j, f: (0, f)),
            pl.BlockSpec((d, TF), lambda i, j, f: (0, f)),
            pl.BlockSpec((TF, d), lambda i, j, f: (f, 0)),
        ],
        out_specs=tok(d),
        scratch_shapes=[pltpu.VMEM((TM, d), F32), pltpu.VMEM((TM, d), BF16), pltpu.VMEM((TM, d), F32)],
        compiler_params=_params(("parallel", "parallel", "arbitrary")),
        name="out_ffn",
    )(x, *ys, mod4, g2, gf, wo, w1, w3, w2)


def _gated_direction(blk, wg_ref, par_ref, tri_ref, ones_ref, mbd_ref,
                     q_s, k_s, v_s, b_s, p_s, o_s, st_s, *, mode, rev):
    d = 1 if rev else 0
    w = GROUP_W
    if mode == "gla":
        q = blk[:, 0:w] * (HEAD_D ** -0.5)
        k = blk[:, w:2 * w]
        v = blk[:, 2 * w:3 * w]
        x = _dot(blk[:, 4 * w:4 * w + LR_PAD].astype(BF16), wg_ref[d]) + par_ref[d:d + 1, :]
        la = _log_sigmoid(x) * (1.0 / GLA_GATE_TEMP)
    else:
        q = _silu(blk[:, 0:w])
        f = blk[:, (1 + d) * w:(2 + d) * w]
        v = blk[:, 3 * w:4 * w]
        k = par_ref[0:1, :] * _sigmoid(-f)
        y = par_ref[2:3, :] + _log_sigmoid(f)
        lb = par_ref[1:2, :]
        la = jnp.maximum(lb, y) + jnp.log1p(jnp.exp(-jnp.abs(lb - y)))
    q_s[...] = q
    k_s[...] = k
    v_s[...] = v
    b_s[...] = _dot_w2(tri_ref[d], la)

    tio = lax.broadcasted_iota(jnp.int32, (SUB, w), 0)
    edge = 0 if rev else SUB - 1

    def body(it, carry):
        i = (N_SUB - 1 - it) if rev else it
        r0 = pl.multiple_of(i * SUB, SUB)
        rows = pl.ds(r0, SUB)
        bb = b_s[rows, :]
        qb = q_s[rows, :]
        kb = k_s[rows, :]
        vb = v_s[rows, :]
        for s in range(SUB):
            bs = b_s[pl.ds(r0 + s, 1), :]
            ks = k_s[pl.ds(r0 + s, 1), :]
            valid = (tio <= s) if rev else (tio >= s)
            e = jnp.exp(jnp.where(valid, bb - bs, -jnp.inf))
            p_s[s * SUB:(s + 1) * SUB, :] = (e * qb * ks).astype(BF16)
        r = _dot(p_s[...], ones_ref[...])
        od = jnp.zeros((SUB, w), F32)
        for s in range(SUB):
            vs = v_s[pl.ds(r0 + s, 1), :]
            od = od + r[s * SUB:(s + 1) * SUB, :] * vs
        bend = b_s[pl.ds(r0 + edge, 1), :]
        st = st_s[...]
        qt = (qb * jnp.exp(bb)).astype(BF16)
        oi = lax.dot_general(qt, st.astype(BF16), (((1,), (1,)), ((), ())), preferred_element_type=F32)
        kt = (kb * jnp.exp(bend - bb)).astype(BF16)
        kv = lax.dot_general(vb.astype(BF16), kt, (((0,), (0,)), ((), ())), preferred_element_type=F32)
        st_s[...] = st * jnp.exp(bend) + kv * mbd_ref[...]
        o_s[rows, :] = od + oi
        return carry

    lax.fori_loop(0, N_SUB, body, 0)


def _gated_kernel(p_ref, wg_ref, par_ref, s0_ref, tri_ref, ones_ref, mbd_ref, y_ref, st_ref,
                  q_s, k_s, v_s, b_s, p_s, o_s, st_s, of_s, *, mode, nc):
    j = pl.program_id(1)
    w = GROUP_W
    scratch = (q_s, k_s, v_s, b_s, p_s, o_s, st_s)

    @pl.when(j < nc)
    def _():
        @pl.when(j == 0)
        def _():
            st_s[...] = s0_ref[0, 0]
        _gated_direction(p_ref[0], wg_ref, par_ref, tri_ref, ones_ref, mbd_ref, *scratch, mode=mode, rev=False)
        of_s[pl.ds(pl.multiple_of(j * LC, LC), LC), :] = o_s[...]

        @pl.when(j == nc - 1)
        def _():
            st_ref[0, 0] = st_s[...]

    @pl.when(j >= nc)
    def _():
        @pl.when(j == nc)
        def _():
            st_s[...] = s0_ref[0, 1]
        blk = p_ref[0]
        _gated_direction(blk, wg_ref, par_ref, tri_ref, ones_ref, mbd_ref, *scratch, mode=mode, rev=True)
        ci = 2 * nc - 1 - j
        o = o_s[...] + of_s[pl.ds(pl.multiple_of(ci * LC, LC), LC), :]
        ms = _dot_x2(o * o, ones_ref[...]) * (1.0 / HEAD_D)
        gate = blk[:, 3 * w:4 * w] if mode == "gla" else blk[:, 4 * w:5 * w]
        y_ref[0] = o * lax.rsqrt(ms + EPS) * par_ref[3:4, :] * _silu(gate)

        @pl.when(j == 2 * nc - 1)
        def _():
            st_ref[0, 1] = st_s[...]


def _gated_consts():
    r = jnp.arange(LC)
    same = (r[:, None] // SUB) == (r[None, :] // SUB)
    lower = (same & (r[None, :] <= r[:, None])).astype(BF16)
    upper = (same & (r[None, :] >= r[:, None])).astype(BF16)
    head = (r[:, None] // HEAD_D) == (r[None, :] // HEAD_D)
    return jnp.stack([lower, upper]), head.astype(BF16), head.astype(F32)


def _gated_mixer(p, wg, par, s0t, consts, mode):
    b, l, width = p.shape
    nc = l // LC
    tri, ones, mbd = consts
    w = GROUP_W

    def chunk(i, j):
        return (i, jnp.where(j < nc, j, 2 * nc - 1 - j), 0)

    def out_chunk(i, j):
        return (i, jnp.where(j < nc, nc - 1, 2 * nc - 1 - j), 0)

    const2 = lambda shape: pl.BlockSpec(shape, lambda i, j: (0,) * len(shape))
    return pl.pallas_call(
        functools.partial(_gated_kernel, mode=mode, nc=nc),
        out_shape=[jax.ShapeDtypeStruct((b, l, w), F32), jax.ShapeDtypeStruct((b, 2, w, w), F32)],
        grid=(b, 2 * nc),
        in_specs=[
            pl.BlockSpec((1, LC, width), chunk),
            const2(wg.shape), const2(par.shape),
            pl.BlockSpec((1, 2, w, w), lambda i, j: (i, 0, 0, 0)),
            const2(tri.shape), const2(ones.shape), const2(mbd.shape),
        ],
        out_specs=[pl.BlockSpec((1, LC, w), out_chunk),
                   pl.BlockSpec((1, 2, w, w), lambda i, j: (i, 0, 0, 0))],
        scratch_shapes=[pltpu.VMEM((LC, w), F32), pltpu.VMEM((LC, w), F32), pltpu.VMEM((LC, w), F32),
                        pltpu.VMEM((LC, w), F32), pltpu.VMEM((SUB * SUB, w), BF16), pltpu.VMEM((LC, w), F32),
                        pltpu.VMEM((w, w), F32), pltpu.VMEM((l, w), F32)],
        compiler_params=_params(("parallel", "arbitrary")),
        name="gated_" + mode,
    )(p, wg, par, s0t, tri, ones, mbd)


def _state_to_blockdiag_t(s):
    b = s.shape[0]
    st = jnp.swapaxes(s, -1, -2)
    eye = jnp.eye(N_HEADS, dtype=s.dtype)
    full = st[:, :, :, :, None, :] * eye[None, None, :, None, :, None]
    return full.reshape(b, 2, GROUP_W, GROUP_W)


def _blockdiag_t_to_state(st):
    b = st.shape[0]
    x = st.reshape(b, 2, N_HEADS, HEAD_D, N_HEADS, HEAD_D)
    diag = jnp.stack([x[:, :, h, :, h, :] for h in range(N_HEADS)], axis=2)
    return jnp.swapaxes(diag, -1, -2)


def _rglru_kernel(x_ref, g_ref, cw_ref, cb_ref, w_ref, bias_ref, nsp_ref, h0_ref, y_ref, hT_ref,
                  xc_s, a_s, u_s, h_s, *, l):
    c = RG_HALF
    x = x_ref[0]
    row = lax.broadcasted_iota(jnp.int32, (l, c), 0)
    xc = x * cw_ref[2:3, :] + cb_ref[...]
    xc = xc + jnp.where(row >= 2, pltpu.roll(x, 2, 0), 0.0) * cw_ref[0:1, :]
    xc = xc + jnp.where(row >= 1, pltpu.roll(x, 1, 0), 0.0) * cw_ref[1:2, :]
    xc = xc + jnp.where(row <= l - 2, pltpu.roll(x, l - 1, 0), 0.0) * cw_ref[3:4, :]
    xc_s[...] = xc

    nslab = l // RG_ROWS
    sub = lax.broadcasted_iota(jnp.int32, (RG_ROWS, c), 0) & (SUBLANE - 1)

    def slab(n, carry):
        rows = pl.ds(pl.multiple_of(n * RG_ROWS, RG_ROWS), RG_ROWS)
        xs = xc_s[rows, :]
        gates = _sigmoid(_dot(xs.astype(BF16), w_ref[...]) + bias_ref[...])
        for d in range(2):
            r = gates[:, (2 * d) * c:(2 * d + 1) * c]
            i = gates[:, (2 * d + 1) * c:(2 * d + 2) * c]
            log_a = r * nsp_ref[d:d + 1, :]
            a = jnp.exp(log_a)
            u = jnp.sqrt(1.0 - jnp.exp(2.0 * log_a)) * (i * xs)
            for sft in (1, 2, 4):
                if d == 0:
                    ok = sub >= sft
                    a_n, u_n = pltpu.roll(a, sft, 0), pltpu.roll(u, sft, 0)
                else:
                    ok = sub <= SUBLANE - 1 - sft
                    a_n, u_n = pltpu.roll(a, RG_ROWS - sft, 0), pltpu.roll(u, RG_ROWS - sft, 0)
                u = jnp.where(ok, a * u_n + u, u)
                a = jnp.where(ok, a * a_n, a)
            a_s[d, rows, :] = a
            u_s[d, rows, :] = u
        return carry

    lax.fori_loop(0, nslab, slab, 0)

    ngrp = l // SUBLANE

    def fwd(n, h):
        rows = pl.ds(pl.multiple_of(n * SUBLANE, SUBLANE), SUBLANE)
        hh = a_s[0, rows, :] * h + u_s[0, rows, :]
        h_s[rows, :] = hh
        return jnp.broadcast_to(hh[SUBLANE - 1:SUBLANE, :], (SUBLANE, c))

    hf = lax.fori_loop(0, ngrp, fwd, jnp.broadcast_to(h0_ref[0, 0:1, :], (SUBLANE, c)))

    def bwd(n, h):
        rows = pl.ds(pl.multiple_of((ngrp - 1 - n) * SUBLANE, SUBLANE), SUBLANE)
        hh = a_s[1, rows, :] * h + u_s[1, rows, :]
        h_s[rows, :] = h_s[rows, :] + hh
        return jnp.broadcast_to(hh[0:1, :], (SUBLANE, c))

    hb = lax.fori_loop(0, ngrp, bwd, jnp.broadcast_to(h0_ref[0, 1:2, :], (SUBLANE, c)))
    hT_ref[0, 0:1, :] = hf[0:1, :]
    hT_ref[0, 1:2, :] = hb[0:1, :]

    g = g_ref[0]
    gelu = 0.5 * g * (1.0 + jnp.tanh(math.sqrt(2.0 / math.pi) * (g + 0.044715 * (g * g * g))))
    y_ref[0] = h_s[...] * gelu


def _rglru(p, cw, cb, wbd, bias, nsp, h0):
    b, l, _ = p.shape
    c = RG_HALF
    half = lambda shape: pl.BlockSpec(shape, lambda i, j: (0,) * (len(shape) - 1) + (j,))
    return pl.pallas_call(
        functools.partial(_rglru_kernel, l=l),
        out_shape=[jax.ShapeDtypeStruct((b, l, GROUP_W), F32), jax.ShapeDtypeStruct((b, 2, GROUP_W), F32)],
        grid=(b, 2),
        in_specs=[
            pl.BlockSpec((1, l, c), lambda i, j: (i, 0, j)),
            pl.BlockSpec((1, l, c), lambda i, j: (i, 0, 2 + j)),
            half((RG_CONV_W, c)), half((1, c)),
            pl.BlockSpec((None, c, 4 * c), lambda i, j: (j, 0, 0)),
            pl.BlockSpec((None, 1, 4 * c), lambda i, j: (j, 0, 0)),
            half((2, c)),
            pl.BlockSpec((1, 2, c), lambda i, j: (i, 0, j)),
        ],
        out_specs=[pl.BlockSpec((1, l, c), lambda i, j: (i, 0, j)),
                   pl.BlockSpec((1, 2, c), lambda i, j: (i, 0, j))],
        scratch_shapes=[pltpu.VMEM((l, c), F32), pltpu.VMEM((2, l, c), F32), pltpu.VMEM((2, l, c), F32),
                        pltpu.VMEM((l, c), F32)],
        compiler_params=_params(("parallel", "parallel")),
        name="rglru",
    )(p, p, cw, cb, wbd, bias, nsp, h0)


def _hy_pre_kernel(c_ref, w_ref, b_ref, o_ref, *, l):
    x = c_ref[0]
    row = lax.broadcasted_iota(jnp.int32, x.shape, 0)
    o = x * w_ref[1:2, :] + b_ref[...]
    o = o + jnp.where(row >= 1, pltpu.roll(x, 1, 0), 0.0) * w_ref[0:1, :]
    o = o + jnp.where(row <= l - 2, pltpu.roll(x, l - 1, 0), 0.0) * w_ref[2:3, :]
    o_ref[0] = o


def _hy_short_conv(p, w, bias):
    b, l, wd = p.shape
    c = GROUP_W
    return pl.pallas_call(
        functools.partial(_hy_pre_kernel, l=l),
        out_shape=jax.ShapeDtypeStruct(p.shape, F32),
        grid=(b, wd // c),
        in_specs=[pl.BlockSpec((1, l, c), lambda i, j: (i, 0, j)),
                  pl.BlockSpec((3, c), lambda i, j: (0, j)),
                  pl.BlockSpec((1, c), lambda i, j: (0, j))],
        out_specs=pl.BlockSpec((1, l, c), lambda i, j: (i, 0, j)),
        compiler_params=_params(("parallel", "parallel")),
        name="hy_short_conv",
    )(p, w, bias)


def _hy_filter_kernel(pe_ref, w1_ref, b1_ref, w2_ref, b2_ref, w3_ref, dec_ref, o_ref):
    pe = pe_ref[...]
    h = jnp.sin(_dot3(pe, w1_ref[...]) + b1_ref[...])
    h = jnp.sin(_dot3(h, w2_ref[...]) + b2_ref[...])
    h = _dot3(h, w3_ref[...])
    dist = pe[:, LANE - 1:LANE]
    h = h * jnp.exp(-dist * dec_ref[...])
    o_ref[...] = h / jnp.sum(jnp.abs(h), axis=0, keepdims=True)


def _hy_filters(pe, w1, b1, w2, b2, w3, decay):
    l = pe.shape[0]
    n = w3.shape[1]
    args = (pe, w1, b1, w2, b2, w3, decay)
    return pl.pallas_call(
        _hy_filter_kernel,
        out_shape=jax.ShapeDtypeStruct((l, n), F32),
        grid=(1,),
        in_specs=[pl.BlockSpec(a.shape, lambda i: (0, 0)) for a in args],
        out_specs=pl.BlockSpec((l, n), lambda i: (0, 0)),
        compiler_params=_params(("arbitrary",)),
        name="hy_filters",
    )(*args)


def _hy_conv_kernel(z_ref, x_ref, h_ref, skip_ref, cf_ref, sf_ref, ci_ref, si_ref, o_ref, *, bb, nk):
    kt = pl.program_id(1)
    c = GROUP_W

    @pl.when(kt == 0)
    def _():
        o_ref[...] = jnp.zeros_like(o_ref)

    zs = [z_ref[b].astype(BF16) for b in range(bb)] + [h_ref[...].astype(BF16)]
    zcat = jnp.concatenate(zs, axis=-1)
    xc = _dot(cf_ref[...], zcat)
    xs = _dot(sf_ref[...], zcat)
    hc = xc[:, bb * c:]
    hs = xs[:, bb * c:]
    for b in range(bb):
        zc = xc[:, b * c:(b + 1) * c]
        zsn = xs[:, b * c:(b + 1) * c]
        yc = (zc * hc - zsn * hs).astype(BF16)
        ys = (zc * hs + zsn * hc).astype(BF16)
        o_ref[b] += _dot(ci_ref[...], yc) + _dot(si_ref[...], ys)

    @pl.when(kt == nk - 1)
    def _():
        for b in range(bb):
            z = z_ref[b]
            o_ref[b] = x_ref[b] * (o_ref[b] + skip_ref[...] * z)


def _hy_long_conv(uc, zprev, z_col, gate_col, h, h_col, skip, skip_col, tables, bb):
    b, l, _ = uc.shape
    c = GROUP_W
    cf, sf, ci, si = tables
    nk = cf.shape[0] // FREQ_TILE
    z_arr = uc if zprev is None else zprev
    return pl.pallas_call(
        functools.partial(_hy_conv_kernel, bb=bb, nk=nk),
        out_shape=jax.ShapeDtypeStruct((b, l, c), F32),
        grid=(b // bb, nk),
        in_specs=[
            pl.BlockSpec((bb, l, c), lambda g, k: (g, 0, z_col)),
            pl.BlockSpec((bb, l, c), lambda g, k: (g, 0, gate_col)),
            pl.BlockSpec((l, c), lambda g, k: (0, h_col)),
            pl.BlockSpec((1, c), lambda g, k: (0, skip_col)),
            pl.BlockSpec((FREQ_TILE, l), lambda g, k: (k, 0)),
            pl.BlockSpec((FREQ_TILE, l), lambda g, k: (k, 0)),
            pl.BlockSpec((l, FREQ_TILE), lambda g, k: (0, k)),
            pl.BlockSpec((l, FREQ_TILE), lambda g, k: (0, k)),
        ],
        out_specs=pl.BlockSpec((bb, l, c), lambda g, k: (g, 0, 0)),
        compiler_params=_params(("parallel", "arbitrary")),
        name="hy_long_conv",
    )(z_arr, uc, h, skip, cf, sf, ci, si)


def _dft_tables(l):
    n = 3 * l // 2
    nf = n // 2 + 1
    nfp = -(-nf // FREQ_TILE) * FREQ_TILE
    k = jnp.arange(nfp, dtype=jnp.int32)
    t = jnp.arange(l, dtype=jnp.int32)
    live = (k < nf)
    scale = 2.0 * math.pi / n
    ang = ((k[:, None] * t[None, :]) % n).astype(F32) * scale
    cf = jnp.where(live[:, None], jnp.cos(ang), 0.0).astype(BF16)
    sf = jnp.where(live[:, None], jnp.sin(ang), 0.0).astype(BF16)
    wk = jnp.where((k == 0) | (k == n // 2), 1.0, 2.0) / n
    wk = jnp.where(live, wk, 0.0).astype(F32)
    ang2 = (((t[:, None] + l // 2) * k[None, :]) % n).astype(F32) * scale
    ci = (wk[None, :] * jnp.cos(ang2)).astype(BF16)
    si = (wk[None, :] * jnp.sin(ang2)).astype(BF16)
    return cf, sf, ci, si


def _hy_pos_features(l):
    pos = jnp.arange(l, dtype=F32)
    t = pos / (l - 1)
    ang = (2.0 * jnp.pi * pos / l)[:, None] * jnp.linspace(1e-4, HY_POS_BANDS - 1, HY_POS_BANDS, dtype=F32)[None, :]
    half = l // 2
    dist = jnp.abs(pos - half) / half
    pe = jnp.concatenate([t[:, None], jnp.cos(ang), -jnp.sin(ang)], axis=-1)
    pad = jnp.zeros((l, LANE - 1 - pe.shape[1]), F32)
    return jnp.concatenate([pe, pad, dist[:, None]], axis=-1)


def _grid_position_embedding(n_tokens, dim):
    rows = n_tokens // GRID_W
    row = jnp.broadcast_to(jnp.arange(rows, dtype=F32)[:, None], (rows, GRID_W)).reshape(-1)
    col = jnp.broadcast_to(jnp.arange(GRID_W, dtype=F32)[None, :], (rows, GRID_W)).reshape(-1)
    quarter = dim // 4
    omega = 1.0 / (POS_BASE ** (jnp.arange(quarter, dtype=F32) / quarter))

    def enc(pos):
        ang = pos[:, None] * omega[None, :]
        return jnp.concatenate([jnp.sin(ang), jnp.cos(ang)], axis=-1)

    return jnp.concatenate([enc(row), enc(col)], axis=-1)


def _blockdiag2(a, b):
    z = jnp.zeros_like(a)
    return jnp.concatenate([jnp.concatenate([a, z], axis=1), jnp.concatenate([z, b], axis=1)], axis=0)


def _layer_params(l, hg_lb, norm1_g, norm2_g, w_in, w_out, gla_w_gate, gla_b_gate, gla_norm_g,
                  rg_conv_w, rg_conv_b, rg_w_a, rg_b_a, rg_w_x, rg_b_x, rg_lambda,
                  hy_conv_w, hy_conv_b, hy_w1, hy_b1, hy_w2, hy_b2, hy_w3, hy_decay, hy_skip,
                  hg_norm_g, ffn_w1, ffn_w3, ffn_w2):
    d, w = D_MODEL, GROUP_W
    n_gla = 4 * w + GLA_LOWRANK
    wi = w_in[l]
    w_proj = jnp.concatenate([wi[:, :n_gla], jnp.zeros((d, LR_PAD - GLA_LOWRANK), F32), wi[:, n_gla:]], axis=1)
    wg = jnp.concatenate([gla_w_gate[l], jnp.zeros((2, LR_PAD - GLA_LOWRANK, w), F32)], axis=1)
    zrow = jnp.zeros((w,), F32)
    par_gla = jnp.stack([gla_b_gate[l, 0], gla_b_gate[l, 1], zrow, gla_norm_g[l]] + [zrow] * 4)
    lb = hg_lb[l]
    par_hg = jnp.stack([1.0 - lb, jnp.log(lb), jnp.log1p(-lb), hg_norm_g[l]] + [zrow] * 4)
    wa, wx = rg_w_a[l], rg_w_x[l]
    rg_w = jnp.stack([
        jnp.concatenate([_blockdiag2(m[dd, 2 * j], m[dd, 2 * j + 1]) for dd in range(2) for m in (wa, wx)], axis=1)
        for j in range(2)])
    ba, bx = rg_b_a[l], rg_b_x[l]
    rg_bias = jnp.stack([
        jnp.concatenate([v[dd, j * RG_HALF:(j + 1) * RG_HALF] for dd in range(2) for v in (ba, bx)])[None, :]
        for j in range(2)])
    nsp = -RG_C * jax.nn.softplus(-rg_lambda[l])
    w1p = jnp.concatenate([hy_w1[l], jnp.zeros((LANE - hy_w1.shape[1], HY_FFN_W), F32)], axis=0)
    return dict(
        norm1=norm1_g[l][None, :], norm2=norm2_g[l][None, :],
        w_proj=w_proj.astype(BF16), w_out=w_out[l].astype(BF16),
        wg=wg.astype(BF16), par_gla=par_gla, par_hg=par_hg,
        rg_cw=rg_conv_w[l], rg_cb=rg_conv_b[l][None, :], rg_w=rg_w.astype(BF16), rg_bias=rg_bias, rg_nsp=nsp,
        hy_cw=hy_conv_w[l], hy_cb=hy_conv_b[l][None, :], hy_w1=w1p, hy_b1=hy_b1[l][None, :],
        hy_w2=hy_w2[l], hy_b2=hy_b2[l][None, :], hy_w3=hy_w3[l], hy_decay=hy_decay[l][None, :],
        hy_skip=hy_skip[l][None, :],
        w1=ffn_w1[l].astype(BF16), w3=ffn_w3[l].astype(BF16), w2=ffn_w2[l].astype(BF16))


def _trunk_layer(x, p, mod4, layer, row0, seq_shape, s_gla, s_rg, s_hg, stream_consts, final, final_g):
    bm, lm, d = x.shape
    b, l = seq_shape
    gated_consts, pe, tables, bb = stream_consts
    pa, pb, pc, pd = _norm_proj(x, mod4, layer, row0, p["norm1"], p["w_proj"])
    pa, pb, pc, pd = (t.reshape(b, l, t.shape[-1]) for t in (pa, pb, pc, pd))

    ya, st_a = _gated_mixer(pa, p["wg"], p["par_gla"], _state_to_blockdiag_t(s_gla), gated_consts, "gla")
    yd, st_d = _gated_mixer(pd, p["wg"], p["par_hg"], _state_to_blockdiag_t(s_hg), gated_consts, "hg")
    yb, st_b = _rglru(pb, p["rg_cw"], p["rg_cb"], p["rg_w"], p["rg_bias"], p["rg_nsp"], s_rg)

    uc = _hy_short_conv(pc, p["hy_cw"], p["hy_cb"])
    filt = _hy_filters(pe, p["hy_w1"], p["hy_b1"], p["hy_w2"], p["hy_b2"], p["hy_w3"], p["hy_decay"])
    z1 = _hy_long_conv(uc, None, 0, 1, filt, 0, p["hy_skip"], 0, tables, bb)
    yc = _hy_long_conv(uc, z1, 0, 2, filt, 1, p["hy_skip"], 1, tables, bb)

    ys = [t.reshape(bm, lm, GROUP_W) for t in (ya, yb, yc, yd)]
    x = _out_ffn(x, ys, mod4, layer, row0, p["norm2"], final_g, p["w_out"], p["w1"], p["w3"], p["w2"], final)
    return x, (_blockdiag_t_to_state(st_a), st_b, _blockdiag_t_to_state(st_d))


def kernel(x_prompt, x_sample, state_gla, state_rglru, state_hgrn, c, c_ctx, norm1_g, norm2_g, final_norm_g, w_mod, b_mod, w_in, w_out, gla_w_gate, gla_b_gate, gla_norm_g, rg_conv_w, rg_conv_b, rg_w_a, rg_b_a, rg_w_x, rg_b_x, rg_lambda, hy_conv_w, hy_conv_b, hy_w1, hy_b1, hy_w2, hy_b2, hy_w3, hy_decay, hy_skip, hg_lower, hg_norm_g, ffn_w1, ffn_w3, ffn_w2):
    depth = w_in.shape[0]
    nb, seq, d = x_prompt.shape
    db, dseq, _ = x_sample.shape

    hg_lb = jnp.cumsum(jax.nn.softmax(hg_lower.astype(F32), axis=0), axis=0)
    hg_lb = hg_lb - hg_lb[0:1]

    cvec = jnp.concatenate([c_ctx[None, :], c, jnp.zeros((SUBLANE - 1 - db, d), F32)], axis=0)
    mod4 = _modulation(cvec, w_mod, b_mod).reshape(depth, SUBLANE, 1, N_MOD * d)

    gated_consts = _gated_consts()
    consts_p = (gated_consts, _hy_pos_features(seq), _dft_tables(seq), 8)
    consts_s = (gated_consts, _hy_pos_features(dseq), _dft_tables(dseq), 1)

    xp = x_prompt.reshape(1, nb * seq, d)
    xs = _add_pos(x_sample, _grid_position_embedding(dseq, d))
    zero_gla = jnp.zeros((nb, 2, N_HEADS, HEAD_D, HEAD_D), F32)
    zero_rg = jnp.zeros((nb, 2, GROUP_W), F32)
    final_g = final_norm_g[None, :]

    gla_states, rg_states, hg_states = [], [], []
    for l in range(depth):
        p = _layer_params(l, hg_lb, norm1_g, norm2_g, w_in, w_out, gla_w_gate, gla_b_gate, gla_norm_g,
                          rg_conv_w, rg_conv_b, rg_w_a, rg_b_a, rg_w_x, rg_b_x, rg_lambda,
                          hy_conv_w, hy_conv_b, hy_w1, hy_b1, hy_w2, hy_b2, hy_w3, hy_decay, hy_skip,
                          hg_norm_g, ffn_w1, ffn_w3, ffn_w2)
        final = l == depth - 1
        xp, (sg, sr, sh) = _trunk_layer(xp, p, mod4, l, 0, (nb, seq), zero_gla, zero_rg, zero_gla,
                                        consts_p, final, final_g)
        xs, _ = _trunk_layer(xs, p, mod4, l, 1, (db, dseq), state_gla[:, l], state_rglru[:, l],
                             state_hgrn[:, l], consts_s, final, final_g)
        gla_states.append(sg)
        rg_states.append(sr)
        hg_states.append(sh)

    return (xp.reshape(nb, seq, d), xs,
            jnp.stack(gla_states, axis=1), jnp.stack(rg_states, axis=1), jnp.stack(hg_states, axis=1))
```

```python
import functools
import math

import jax
import jax.numpy as jnp
from jax import lax
from jax.experimental import pallas as pl
from jax.experimental.pallas import tpu as pltpu

F32 = jnp.float32
BF16 = jnp.bfloat16

D_MODEL = 1024
N_MOD = 6
GROUP_W = 256
N_HEADS = 4
HEAD_D = GROUP_W // N_HEADS
GLA_LOWRANK = 16
GLA_GATE_TEMP = 16.0
RG_C = 8.0
RG_CONV_W = 4
HY_POS_BANDS = 16
HY_FFN_W = 64
D_FF = 2816
EPS = 1e-6
GRID_W = 64
POS_BASE = 10000.0

LANE = 128
SUBLANE = 8
VMEM_LIMIT = 56 * 1024 * 1024

LOG2E = 1.0 / math.log(2.0)
SUB = 16
LC = 256
N_SUB = LC // SUB
LR_PAD = LANE
W_GLA = 4 * GROUP_W + LR_PAD
W_RG = 2 * GROUP_W
W_HY = 3 * GROUP_W
W_HG = 5 * GROUP_W
W_PROJ = W_GLA + W_RG + W_HY + W_HG
TM = 512
TF = D_FF // 2
TN_MOD = 512
RG_HALF = GROUP_W // 2
RG_ROWS = 256
FREQ_TILE = 256
TWID = 64


def _dot(a, b):
    return jnp.dot(a, b, preferred_element_type=F32)


def _split(x):
    hi = x.astype(BF16)
    lo = (x - hi.astype(F32)).astype(BF16)
    return hi, lo


def _dot_x2(x, w):
    hi, lo = _split(x)
    return _dot(hi, w) + _dot(lo, w)


def _dot_w2(w, x):
    hi, lo = _split(x)
    return _dot(w, hi) + _dot(w, lo)


def _dot3(a, b):
    ah, al = _split(a)
    bh, bl = _split(b)
    return _dot(ah, bh) + _dot(al, bh) + _dot(ah, bl)


def _sigmoid(x):
    return 1.0 / (1.0 + jnp.exp(-x))


def _silu(x):
    return x * _sigmoid(x)


def _log1p_exp_neg_abs(x):
    return jnp.log(1.0 + jnp.exp(-jnp.abs(x)))


def _log_sigmoid(x):
    return jnp.minimum(x, 0.0) - _log1p_exp_neg_abs(x)


def _rms(x):
    return x * lax.rsqrt(jnp.mean(x * x, axis=-1, keepdims=True) + EPS)


def _params(sem, vmem=VMEM_LIMIT):
    return pltpu.CompilerParams(dimension_semantics=sem, vmem_limit_bytes=vmem)


def _mod_kernel(c_ref, w_ref, b_ref, o_ref):
    c = c_ref[...]
    o_ref[0] = _dot3(_silu(c), w_ref[0]) + b_ref[0]


def _modulation(cvec, w_mod, b_mod):
    depth = w_mod.shape[0]
    n = N_MOD * D_MODEL
    return pl.pallas_call(
        _mod_kernel,
        out_shape=jax.ShapeDtypeStruct((depth, SUBLANE, n), F32),
        grid=(depth, n // TN_MOD),
        in_specs=[
            pl.BlockSpec((SUBLANE, D_MODEL), lambda l, j: (0, 0)),
            pl.BlockSpec((1, D_MODEL, TN_MOD), lambda l, j: (l, 0, j)),
            pl.BlockSpec((1, 1, TN_MOD), lambda l, j: (l, 0, j)),
        ],
        out_specs=pl.BlockSpec((1, SUBLANE, TN_MOD), lambda l, j: (l, 0, j)),
        compiler_params=_params(("parallel", "parallel")),
        name="modulation",
    )(cvec, w_mod, b_mod.reshape(depth, 1, n))


def _add_kernel(x_ref, p_ref, o_ref):
    o_ref[0] = x_ref[0] + p_ref[...]


def _add_pos(x, pos):
    b, l, d = x.shape
    return pl.pallas_call(
        _add_kernel,
        out_shape=jax.ShapeDtypeStruct(x.shape, F32),
        grid=(b, l // TM),
        in_specs=[pl.BlockSpec((1, TM, d), lambda i, j: (i, j, 0)),
                  pl.BlockSpec((TM, d), lambda i, j: (j, 0))],
        out_specs=pl.BlockSpec((1, TM, d), lambda i, j: (i, j, 0)),
        compiler_params=_params(("parallel", "parallel")),
        name="add_pos",
    )(x, pos)


def _proj_kernel(x_ref, mod_ref, g_ref, w_ref, oa_ref, ob_ref, oc_ref, od_ref):
    x = x_ref[0]
    m = mod_ref[0, 0]
    sh = m[:, 0:D_MODEL]
    sc = m[:, D_MODEL:2 * D_MODEL]
    u = _rms(x) * g_ref[...] * (1.0 + sc) + sh
    p = _dot(u.astype(BF16), w_ref[...])
    oa_ref[0] = p[:, 0:W_GLA]
    ob_ref[0] = p[:, W_GLA:W_GLA + W_RG]
    oc_ref[0] = p[:, W_GLA + W_RG:W_GLA + W_RG + W_HY]
    od_ref[0] = p[:, W_GLA + W_RG + W_HY:W_PROJ]


def _norm_proj(x, mod4, layer, row0, gain, w):
    bm, lm, d = x.shape
    widths = (W_GLA, W_RG, W_HY, W_HG)
    return pl.pallas_call(
        _proj_kernel,
        out_shape=[jax.ShapeDtypeStruct((bm, lm, wd), F32) for wd in widths],
        grid=(bm, lm // TM),
        in_specs=[
            pl.BlockSpec((1, TM, d), lambda i, j: (i, j, 0)),
            pl.BlockSpec((1, 1, 1, N_MOD * d), lambda i, j: (layer, row0 + i, 0, 0)),
            pl.BlockSpec((1, d), lambda i, j: (0, 0)),
            pl.BlockSpec((d, W_PROJ), lambda i, j: (0, 0)),
        ],
        out_specs=[pl.BlockSpec((1, TM, wd), lambda i, j: (i, j, 0)) for wd in widths],
        compiler_params=_params(("parallel", "parallel")),
        name="norm_proj",
    )(x, mod4, gain, w)


def _ffn_kernel(x_ref, ya_ref, yb_ref, yc_ref, yd_ref, mod_ref, g2_ref, gf_ref,
                wo_ref, w1_ref, w3_ref, w2_ref, o_ref, x1_s, u_s, acc_s, *, nf, final):
    f = pl.program_id(2)
    d = D_MODEL

    @pl.when(f == 0)
    def _():
        m = mod_ref[0, 0]
        g1 = m[:, 2 * d:3 * d]
        sh2 = m[:, 3 * d:4 * d]
        sc2 = m[:, 4 * d:5 * d]
        y = jnp.concatenate([ya_ref[0], yb_ref[0], yc_ref[0], yd_ref[0]], axis=-1)
        x1 = x_ref[0] + g1 * _dot(y.astype(BF16), wo_ref[...])
        x1_s[...] = x1
        u_s[...] = (_rms(x1) * g2_ref[...] * (1.0 + sc2) + sh2).astype(BF16)
        acc_s[...] = jnp.zeros_like(acc_s)

    u = u_s[...]
    h = _silu(_dot(u, w1_ref[...])) * _dot(u, w3_ref[...])
    acc_s[...] += _dot(h.astype(BF16), w2_ref[...])

    @pl.when(f == nf - 1)
    def _():
        g2 = mod_ref[0, 0][:, 5 * d:6 * d]
        xo = x1_s[...] + g2 * acc_s[...]
        if final:
            xo = _rms(xo) * gf_ref[...]
        o_ref[0] = xo


def _out_ffn(x, ys, mod4, layer, row0, g2, gf, wo, w1, w3, w2, final):
    bm, lm, d = x.shape
    nf = D_FF // TF
    tok = lambda wd: pl.BlockSpec((1, TM, wd), lambda i, j, f: (i, j, 0))
    return pl.pallas_call(
        functools.partial(_ffn_kernel, nf=nf, final=final),
        out_shape=jax.ShapeDtypeStruct(x.shape, F32),
        grid=(bm, lm // TM, nf),
        in_specs=[
            tok(d), tok(GROUP_W), tok(GROUP_W), tok(GROUP_W), tok(GROUP_W),
            pl.BlockSpec((1, 1, 1, N_MOD * d), lambda i, j, f: (layer, row0 + i, 0, 0)),
            pl.BlockSpec((1, d), lambda i, j, f: (0, 0)),
            pl.BlockSpec((1, d), lambda i, j, f: (0, 0)),
            pl.BlockSpec((d, d), lambda i, j, f: (0, 0)),
            pl.BlockSpec((d, TF), lambda i, j, f: (0, f)),
            pl.BlockSpec((d, TF), lambda i, j, f: (0, f)),
            pl.BlockSpec((TF, d), lambda i, j, f: (f, 0)),
        ],
        out_specs=tok(d),
        scratch_shapes=[pltpu.VMEM((TM, d), F32), pltpu.VMEM((TM, d), BF16), pltpu.VMEM((TM, d), F32)],
        compiler_params=_params(("parallel", "parallel", "arbitrary")),
        name="out_ffn",
    )(x, *ys, mod4, g2, gf, wo, w1, w3, w2)


def _gated_prelude(blk, d, wg_ref, par_ref, tri_ref, q_s, k_s, v_s, b_s, *, mode):
    w = GROUP_W
    if mode == "gla":
        q = blk[:, 0:w] * (HEAD_D ** -0.5)
        k = blk[:, w:2 * w]
        v = blk[:, 2 * w:3 * w]
        x = _dot(blk[:, 4 * w:4 * w + LR_PAD].astype(BF16), wg_ref[d]) + par_ref[d:d + 1, :]
        la = _log_sigmoid(x) * (1.0 / GLA_GATE_TEMP)
    else:
        q = _silu(blk[:, 0:w])
        f = blk[:, (1 + d) * w:(2 + d) * w]
        v = blk[:, 3 * w:4 * w]
        k = par_ref[0:1, :] * _sigmoid(-f)
        y = par_ref[2:3, :] + _log_sigmoid(f)
        lb = par_ref[1:2, :]
        la = jnp.maximum(lb, y) + _log1p_exp_neg_abs(lb - y)
    q_s[d] = q
    k_s[d] = k
    v_s[d] = v
    b_s[d] = _dot_w2(tri_ref[d], la * LOG2E)


def _gated_subblock(d, i, ones_ref, mbd_ref, q_s, k_s, v_s, b_s, p_s, o_s, st_s):
    w = GROUP_W
    rev = d == 1
    tio = lax.broadcasted_iota(jnp.int32, (SUB, w), 0)
    edge = 0 if rev else SUB - 1
    rows = pl.ds(pl.multiple_of(i * SUB, SUB), SUB)
    bb = b_s[d, rows, :]
    qb = q_s[d, rows, :]
    kb = k_s[d, rows, :]
    vb = v_s[d, rows, :]

    def row(a, s):
        return jnp.broadcast_to(a[s:s + 1, :], (SUB, w))

    for s in range(SUB):
        valid = (tio <= s) if rev else (tio >= s)
        e = jnp.exp2(jnp.where(valid, bb - row(bb, s), -jnp.inf))
        p_s[d, s * SUB:(s + 1) * SUB, :] = (e * qb * row(kb, s)).astype(BF16)
    r = _dot(p_s[d], ones_ref[...])
    od = jnp.zeros((SUB, w), F32)
    for s in range(SUB):
        od = od + r[s * SUB:(s + 1) * SUB, :] * row(vb, s)
    bend = row(bb, edge)
    st = st_s[d]
    qt = (qb * jnp.exp2(bb)).astype(BF16)
    oi = lax.dot_general(qt, st.astype(BF16), (((1,), (1,)), ((), ())), preferred_element_type=F32)
    kt = (kb * jnp.exp2(bend - bb)).astype(BF16)
    kv = lax.dot_general(vb.astype(BF16), kt, (((0,), (0,)), ((), ())), preferred_element_type=F32)
    st_s[d] = st * jnp.exp2(bend[0:1, :]) + kv * mbd_ref[...]
    o_s[d, rows, :] = od + oi


def _gated_kernel(pf_ref, pb_ref, wg_ref, par_ref, s0_ref, tri_ref, ones_ref, mbd_ref, y_ref, st_ref,
                  q_s, k_s, v_s, b_s, p_s, o_s, st_s, *, mode, nc):
    j = pl.program_id(1)
    w = GROUP_W

    @pl.when(j == 0)
    def _():
        st_s[0] = s0_ref[0, 0]
        st_s[1] = s0_ref[0, 1]

    _gated_prelude(pf_ref[0], 0, wg_ref, par_ref, tri_ref, q_s, k_s, v_s, b_s, mode=mode)
    _gated_prelude(pb_ref[0], 1, wg_ref, par_ref, tri_ref, q_s, k_s, v_s, b_s, mode=mode)

    def body(it, carry):
        scratch = (q_s, k_s, v_s, b_s, p_s, o_s, st_s)
        _gated_subblock(0, it, ones_ref, mbd_ref, *scratch)
        _gated_subblock(1, N_SUB - 1 - it, ones_ref, mbd_ref, *scratch)
        return carry

    lax.fori_loop(0, N_SUB, body, 0)

    @pl.when(j == nc - 1)
    def _():
        st_ref[0, 0] = st_s[0]
        st_ref[0, 1] = st_s[1]

    gcol = 3 * w if mode == "gla" else 4 * w

    def finish(o, p_ref):
        ms = _dot_x2(o * o, ones_ref[...]) * (1.0 / HEAD_D)
        return o * lax.rsqrt(ms + EPS) * par_ref[3:4, :] * _silu(p_ref[0][:, gcol:gcol + w])

    rows_f = pl.ds(pl.multiple_of(j * LC, LC), LC)
    rows_b = pl.ds(pl.multiple_of((nc - 1 - j) * LC, LC), LC)
    if nc == 1:
        y_ref[0] = finish(o_s[0] + o_s[1], pf_ref)
    else:
        @pl.when(j < nc // 2)
        def _():
            y_ref[0, rows_f, :] = o_s[0]
            y_ref[0, rows_b, :] = o_s[1]

        @pl.when(j >= nc // 2)
        def _():
            y_ref[0, rows_f, :] = finish(y_ref[0, rows_f, :] + o_s[0], pf_ref)
            y_ref[0, rows_b, :] = finish(y_ref[0, rows_b, :] + o_s[1], pb_ref)


def _gated_consts():
    r = jnp.arange(LC)
    same = (r[:, None] // SUB) == (r[None, :] // SUB)
    lower = (same & (r[None, :] <= r[:, None])).astype(BF16)
    upper = (same & (r[None, :] >= r[:, None])).astype(BF16)
    head = (r[:, None] // HEAD_D) == (r[None, :] // HEAD_D)
    return jnp.stack([lower, upper]), head.astype(BF16), head.astype(F32)


def _gated_mixer(p, wg, par, s0t, consts, mode):
    b, l, width = p.shape
    nc = l // LC
    assert nc == 1 or nc % 2 == 0
    tri, ones, mbd = consts
    w = GROUP_W
    const2 = lambda shape: pl.BlockSpec(shape, lambda i, j: (0,) * len(shape))
    return pl.pallas_call(
        functools.partial(_gated_kernel, mode=mode, nc=nc),
        out_shape=[jax.ShapeDtypeStruct((b, l, w), F32), jax.ShapeDtypeStruct((b, 2, w, w), F32)],
        grid=(b, nc),
        in_specs=[
            pl.BlockSpec((1, LC, width), lambda i, j: (i, j, 0)),
            pl.BlockSpec((1, LC, width), lambda i, j: (i, nc - 1 - j, 0)),
            const2(wg.shape), const2(par.shape),
            pl.BlockSpec((1, 2, w, w), lambda i, j: (i, 0, 0, 0)),
            const2(tri.shape), const2(ones.shape), const2(mbd.shape),
        ],
        out_specs=[pl.BlockSpec((1, l, w), lambda i, j: (i, 0, 0)),
                   pl.BlockSpec((1, 2, w, w), lambda i, j: (i, 0, 0, 0))],
        scratch_shapes=[pltpu.VMEM((2, LC, w), F32), pltpu.VMEM((2, LC, w), F32), pltpu.VMEM((2, LC, w), F32),
                        pltpu.VMEM((2, LC, w), F32), pltpu.VMEM((2, SUB * SUB, w), BF16),
                        pltpu.VMEM((2, LC, w), F32), pltpu.VMEM((2, w, w), F32)],
        compiler_params=_params(("parallel", "arbitrary")),
        name="gated_" + mode,
    )(p, p, wg, par, s0t, tri, ones, mbd)


def _state_to_blockdiag_t(s):
    b = s.shape[0]
    st = jnp.swapaxes(s, -1, -2)
    eye = jnp.eye(N_HEADS, dtype=s.dtype)
    full = st[:, :, :, :, None, :] * eye[None, None, :, None, :, None]
    return full.reshape(b, 2, GROUP_W, GROUP_W)


def _blockdiag_t_to_state(st):
    b = st.shape[0]
    x = st.reshape(b, 2, N_HEADS, HEAD_D, N_HEADS, HEAD_D)
    diag = jnp.stack([x[:, :, h, :, h, :] for h in range(N_HEADS)], axis=2)
    return jnp.swapaxes(diag, -1, -2)


def _rglru_kernel(x_ref, g_ref, cw_ref, cb_ref, w_ref, bias_ref, nsp_ref, h0_ref, y_ref, hT_ref,
                  xc_s, a_s, u_s, h_s, *, l):
    c = RG_HALF
    x = x_ref[0]
    row = lax.broadcasted_iota(jnp.int32, (l, c), 0)
    xc = x * cw_ref[2:3, :] + cb_ref[...]
    xc = xc + jnp.where(row >= 2, pltpu.roll(x, 2, 0), 0.0) * cw_ref[0:1, :]
    xc = xc + jnp.where(row >= 1, pltpu.roll(x, 1, 0), 0.0) * cw_ref[1:2, :]
    xc = xc + jnp.where(row <= l - 2, pltpu.roll(x, l - 1, 0), 0.0) * cw_ref[3:4, :]
    xc_s[...] = xc

    nslab = l // RG_ROWS
    grp = (RG_ROWS // SUBLANE, SUBLANE, c)
    sub = lax.broadcasted_iota(jnp.int32, grp, 1)

    def slab(n, carry):
        rows = pl.ds(pl.multiple_of(n * RG_ROWS, RG_ROWS), RG_ROWS)
        xs = xc_s[rows, :]
        gates = _sigmoid(_dot(xs.astype(BF16), w_ref[...]) + bias_ref[...])
        for d in range(2):
            r = gates[:, (2 * d) * c:(2 * d + 1) * c]
            i = gates[:, (2 * d + 1) * c:(2 * d + 2) * c]
            log_a = r * nsp_ref[d:d + 1, :]
            a = jnp.exp(log_a).reshape(grp)
            u = (jnp.sqrt(1.0 - jnp.exp(2.0 * log_a)) * (i * xs)).reshape(grp)
            for sft in (1, 2, 4):
                if d == 0:
                    ok = sub >= sft
                    a_n, u_n = pltpu.roll(a, sft, 1), pltpu.roll(u, sft, 1)
                else:
                    ok = sub <= SUBLANE - 1 - sft
                    a_n, u_n = pltpu.roll(a, SUBLANE - sft, 1), pltpu.roll(u, SUBLANE - sft, 1)
                u = jnp.where(ok, a * u_n + u, u)
                a = jnp.where(ok, a * a_n, a)
            a_s[d, rows, :] = a.reshape(RG_ROWS, c)
            u_s[d, rows, :] = u.reshape(RG_ROWS, c)
        return carry

    lax.fori_loop(0, nslab, slab, 0)

    ngrp = l // SUBLANE

    def fwd(n, h):
        rows = pl.ds(pl.multiple_of(n * SUBLANE, SUBLANE), SUBLANE)
        hh = a_s[0, rows, :] * h + u_s[0, rows, :]
        h_s[rows, :] = hh
        return jnp.broadcast_to(hh[SUBLANE - 1:SUBLANE, :], (SUBLANE, c))

    hf = lax.fori_loop(0, ngrp, fwd, jnp.broadcast_to(h0_ref[0, 0:1, :], (SUBLANE, c)))

    def bwd(n, h):
        rows = pl.ds(pl.multiple_of((ngrp - 1 - n) * SUBLANE, SUBLANE), SUBLANE)
        hh = a_s[1, rows, :] * h + u_s[1, rows, :]
        h_s[rows, :] = h_s[rows, :] + hh
        return jnp.broadcast_to(hh[0:1, :], (SUBLANE, c))

    hb = lax.fori_loop(0, ngrp, bwd, jnp.broadcast_to(h0_ref[0, 1:2, :], (SUBLANE, c)))
    hT_ref[0, 0:1, :] = hf[0:1, :]
    hT_ref[0, 1:2, :] = hb[0:1, :]

    g = g_ref[0]
    gelu = 0.5 * g * (1.0 + jnp.tanh(math.sqrt(2.0 / math.pi) * (g + 0.044715 * (g * g * g))))
    y_ref[0] = h_s[...] * gelu


def _rglru(p, cw, cb, wbd, bias, nsp, h0):
    b, l, _ = p.shape
    c = RG_HALF
    half = lambda shape: pl.BlockSpec(shape, lambda i, j: (0,) * (len(shape) - 1) + (j,))
    return pl.pallas_call(
        functools.partial(_rglru_kernel, l=l),
        out_shape=[jax.ShapeDtypeStruct((b, l, GROUP_W), F32), jax.ShapeDtypeStruct((b, 2, GROUP_W), F32)],
        grid=(b, 2),
        in_specs=[
            pl.BlockSpec((1, l, c), lambda i, j: (i, 0, j)),
            pl.BlockSpec((1, l, c), lambda i, j: (i, 0, 2 + j)),
            half((RG_CONV_W, c)), half((1, c)),
            pl.BlockSpec((None, c, 4 * c), lambda i, j: (j, 0, 0)),
            pl.BlockSpec((None, 1, 4 * c), lambda i, j: (j, 0, 0)),
            half((2, c)),
            pl.BlockSpec((1, 2, c), lambda i, j: (i, 0, j)),
        ],
        out_specs=[pl.BlockSpec((1, l, c), lambda i, j: (i, 0, j)),
                   pl.BlockSpec((1, 2, c), lambda i, j: (i, 0, j))],
        scratch_shapes=[pltpu.VMEM((l, c), F32), pltpu.VMEM((2, l, c), F32), pltpu.VMEM((2, l, c), F32),
                        pltpu.VMEM((l, c), F32)],
        compiler_params=_params(("parallel", "parallel")),
        name="rglru",
    )(p, p, cw, cb, wbd, bias, nsp, h0)


def _hy_pre_kernel(c_ref, w_ref, b_ref, o_ref, *, l):
    x = c_ref[0]
    row = lax.broadcasted_iota(jnp.int32, x.shape, 0)
    o = x * w_ref[1:2, :] + b_ref[...]
    o = o + jnp.where(row >= 1, pltpu.roll(x, 1, 0), 0.0) * w_ref[0:1, :]
    o = o + jnp.where(row <= l - 2, pltpu.roll(x, l - 1, 0), 0.0) * w_ref[2:3, :]
    o_ref[0] = o


def _hy_short_conv(p, w, bias):
    b, l, wd = p.shape
    c = GROUP_W
    return pl.pallas_call(
        functools.partial(_hy_pre_kernel, l=l),
        out_shape=jax.ShapeDtypeStruct(p.shape, F32),
        grid=(b, wd // c),
        in_specs=[pl.BlockSpec((1, l, c), lambda i, j: (i, 0, j)),
                  pl.BlockSpec((3, c), lambda i, j: (0, j)),
                  pl.BlockSpec((1, c), lambda i, j: (0, j))],
        out_specs=pl.BlockSpec((1, l, c), lambda i, j: (i, 0, j)),
        compiler_params=_params(("parallel", "parallel")),
        name="hy_short_conv",
    )(p, w, bias)


def _hy_filter_kernel(pe_ref, w1_ref, b1_ref, w2_ref, b2_ref, w3_ref, dec_ref, o_ref):
    pe = pe_ref[...]
    h = jnp.sin(_dot3(pe, w1_ref[...]) + b1_ref[...])
    h = jnp.sin(_dot3(h, w2_ref[...]) + b2_ref[...])
    h = _dot3(h, w3_ref[...])
    dist = pe[:, LANE - 1:LANE]
    h = h * jnp.exp(-dist * dec_ref[...])
    o_ref[...] = h / jnp.sum(jnp.abs(h), axis=0, keepdims=True)


def _hy_filters(pe, w1, b1, w2, b2, w3, decay):
    l = pe.shape[0]
    n = w3.shape[1]
    args = (pe, w1, b1, w2, b2, w3, decay)
    return pl.pallas_call(
        _hy_filter_kernel,
        out_shape=jax.ShapeDtypeStruct((l, n), F32),
        grid=(1,),
        in_specs=[pl.BlockSpec(a.shape, lambda i: (0, 0)) for a in args],
        out_specs=pl.BlockSpec((l, n), lambda i: (0, 0)),
        compiler_params=_params(("arbitrary",)),
        name="hy_filters",
    )(*args)


def _hy_conv_kernel(z_ref, x_ref, h_ref, skip_ref, cf_ref, sf_ref, ci_ref, si_ref, o_ref, *, bb, nk):
    kt = pl.program_id(1)
    c = GROUP_W

    @pl.when(kt == 0)
    def _():
        o_ref[...] = jnp.zeros_like(o_ref)

    zs = [z_ref[b].astype(BF16) for b in range(bb)] + [h_ref[...].astype(BF16)]
    zcat = jnp.concatenate(zs, axis=-1)
    xc = _dot(cf_ref[...], zcat)
    xs = _dot(sf_ref[...], zcat)
    hc = xc[:, bb * c:]
    hs = xs[:, bb * c:]
    for b in range(bb):
        zc = xc[:, b * c:(b + 1) * c]
        zsn = xs[:, b * c:(b + 1) * c]
        yc = (zc * hc - zsn * hs).astype(BF16)
        ys = (zc * hs + zsn * hc).astype(BF16)
        o_ref[b] += _dot(ci_ref[...], yc) + _dot(si_ref[...], ys)

    @pl.when(kt == nk - 1)
    def _():
        for b in range(bb):
            z = z_ref[b]
            o_ref[b] = x_ref[b] * (o_ref[b] + skip_ref[...] * z)


def _hy_long_conv(uc, zprev, z_col, gate_col, h, h_col, skip, skip_col, tables, bb):
    b, l, _ = uc.shape
    c = GROUP_W
    cf, sf, ci, si = tables
    nk = cf.shape[0] // FREQ_TILE
    z_arr = uc if zprev is None else zprev
    return pl.pallas_call(
        functools.partial(_hy_conv_kernel, bb=bb, nk=nk),
        out_shape=jax.ShapeDtypeStruct((b, l, c), F32),
        grid=(b // bb, nk),
        in_specs=[
            pl.BlockSpec((bb, l, c), lambda g, k: (g, 0, z_col)),
            pl.BlockSpec((bb, l, c), lambda g, k: (g, 0, gate_col)),
            pl.BlockSpec((l, c), lambda g, k: (0, h_col)),
            pl.BlockSpec((1, c), lambda g, k: (0, skip_col)),
            pl.BlockSpec((FREQ_TILE, l), lambda g, k: (k, 0)),
            pl.BlockSpec((FREQ_TILE, l), lambda g, k: (k, 0)),
            pl.BlockSpec((l, FREQ_TILE), lambda g, k: (0, k)),
            pl.BlockSpec((l, FREQ_TILE), lambda g, k: (0, k)),
        ],
        out_specs=pl.BlockSpec((bb, l, c), lambda g, k: (g, 0, 0)),
        compiler_params=_params(("parallel", "arbitrary")),
        name="hy_long_conv",
    )(z_arr, uc, h, skip, cf, sf, ci, si)


def _dft_tables(l):
    n = 3 * l // 2
    nf = n // 2 + 1
    nfp = -(-nf // FREQ_TILE) * FREQ_TILE
    k = jnp.arange(nfp, dtype=jnp.int32)
    ta = jnp.arange(l // TWID, dtype=jnp.int32) * TWID
    tb = jnp.arange(TWID, dtype=jnp.int32)
    live = (k < nf)

    def cos_sin(m):
        ang = (m % n).astype(F32) * (2.0 * math.pi / n)
        return jnp.cos(ang), jnp.sin(ang)

    ca, sa = cos_sin(k[:, None] * ta[None, :])
    cb, sb = cos_sin(k[:, None] * tb[None, :])
    ca, sa = (jnp.where(live[:, None], x, 0.0)[:, :, None] for x in (ca, sa))
    cb, sb = cb[:, None, :], sb[:, None, :]
    cf = (ca * cb - sa * sb).reshape(nfp, l).astype(BF16)
    sf = (sa * cb + ca * sb).reshape(nfp, l).astype(BF16)
    wk = jnp.where((k == 0) | (k == n // 2), 1.0, 2.0) / n
    wk = jnp.where(live, wk, 0.0).astype(F32)
    ca, sa = cos_sin((ta + l // 2)[:, None] * k[None, :])
    cb, sb = cos_sin(tb[:, None] * k[None, :])
    ca, sa = ((wk[None, :] * x)[:, None, :] for x in (ca, sa))
    cb, sb = cb[None, :, :], sb[None, :, :]
    ci = (ca * cb - sa * sb).reshape(l, nfp).astype(BF16)
    si = (sa * cb + ca * sb).reshape(l, nfp).astype(BF16)
    return cf, sf, ci, si


def _hy_pos_features(l):
    pos = jnp.arange(l, dtype=F32)
    t = pos / (l - 1)
    ang = (2.0 * jnp.pi * pos / l)[:, None] * jnp.linspace(1e-4, HY_POS_BANDS - 1, HY_POS_BANDS, dtype=F32)[None, :]
    half = l // 2
    dist = jnp.abs(pos - half) / half
    pe = jnp.concatenate([t[:, None], jnp.cos(ang), -jnp.sin(ang)], axis=-1)
    pad = jnp.zeros((l, LANE - 1 - pe.shape[1]), F32)
    return jnp.concatenate([pe, pad, dist[:, None]], axis=-1)


def _grid_position_embedding(n_tokens, dim):
    rows = n_tokens // GRID_W
    row = jnp.broadcast_to(jnp.arange(rows, dtype=F32)[:, None], (rows, GRID_W)).reshape(-1)
    col = jnp.broadcast_to(jnp.arange(GRID_W, dtype=F32)[None, :], (rows, GRID_W)).reshape(-1)
    quarter = dim // 4
    omega = 1.0 / (POS_BASE ** (jnp.arange(quarter, dtype=F32) / quarter))

    def enc(pos):
        ang = pos[:, None] * omega[None, :]
        return jnp.concatenate([jnp.sin(ang), jnp.cos(ang)], axis=-1)

    return jnp.concatenate([enc(row), enc(col)], axis=-1)


def _blockdiag2(a, b):
    z = jnp.zeros_like(a)
    return jnp.concatenate([jnp.concatenate([a, z], axis=1), jnp.concatenate([z, b], axis=1)], axis=0)


def _layer_params(l, hg_lb, norm1_g, norm2_g, w_in, w_out, gla_w_gate, gla_b_gate, gla_norm_g,
                  rg_conv_w, rg_conv_b, rg_w_a, rg_b_a, rg_w_x, rg_b_x, rg_lambda,
                  hy_conv_w, hy_conv_b, hy_w1, hy_b1, hy_w2, hy_b2, hy_w3, hy_decay, hy_skip,
                  hg_norm_g, ffn_w1, ffn_w3, ffn_w2):
    d, w = D_MODEL, GROUP_W
    n_gla = 4 * w + GLA_LOWRANK
    wi = w_in[l]
    w_proj = jnp.concatenate([wi[:, :n_gla], jnp.zeros((d, LR_PAD - GLA_LOWRANK), F32), wi[:, n_gla:]], axis=1)
    wg = jnp.concatenate([gla_w_gate[l], jnp.zeros((2, LR_PAD - GLA_LOWRANK, w), F32)], axis=1)
    zrow = jnp.zeros((w,), F32)
    par_gla = jnp.stack([gla_b_gate[l, 0], gla_b_gate[l, 1], zrow, gla_norm_g[l]] + [zrow] * 4)
    lb = hg_lb[l]
    par_hg = jnp.stack([1.0 - lb, jnp.log(lb), jnp.log1p(-lb), hg_norm_g[l]] + [zrow] * 4)
    wa, wx = rg_w_a[l], rg_w_x[l]
    rg_w = jnp.stack([
        jnp.concatenate([_blockdiag2(m[dd, 2 * j], m[dd, 2 * j + 1]) for dd in range(2) for m in (wa, wx)], axis=1)
        for j in range(2)])
    ba, bx = rg_b_a[l], rg_b_x[l]
    rg_bias = jnp.stack([
        jnp.concatenate([v[dd, j * RG_HALF:(j + 1) * RG_HALF] for dd in range(2) for v in (ba, bx)])[None, :]
        for j in range(2)])
    nsp = -RG_C * jax.nn.softplus(-rg_lambda[l])
    w1p = jnp.concatenate([hy_w1[l], jnp.zeros((LANE - hy_w1.shape[1], HY_FFN_W), F32)], axis=0)
    return dict(
        norm1=norm1_g[l][None, :], norm2=norm2_g[l][None, :],
        w_proj=w_proj.astype(BF16), w_out=w_out[l].astype(BF16),
        wg=wg.astype(BF16), par_gla=par_gla, par_hg=par_hg,
        rg_cw=rg_conv_w[l], rg_cb=rg_conv_b[l][None, :], rg_w=rg_w.astype(BF16), rg_bias=rg_bias, rg_nsp=nsp,
        hy_cw=hy_conv_w[l], hy_cb=hy_conv_b[l][None, :], hy_w1=w1p, hy_b1=hy_b1[l][None, :],
        hy_w2=hy_w2[l], hy_b2=hy_b2[l][None, :], hy_w3=hy_w3[l], hy_decay=hy_decay[l][None, :],
        hy_skip=hy_skip[l][None, :],
        w1=ffn_w1[l].astype(BF16), w3=ffn_w3[l].astype(BF16), w2=ffn_w2[l].astype(BF16))


def _trunk_layer(x, p, mod4, layer, row0, seq_shape, s_gla, s_rg, s_hg, stream_consts, final, final_g):
    bm, lm, d = x.shape
    b, l = seq_shape
    gated_consts, pe, tables, bb = stream_consts
    pa, pb, pc, pd = _norm_proj(x, mod4, layer, row0, p["norm1"], p["w_proj"])
    pa, pb, pc, pd = (t.reshape(b, l, t.shape[-1]) for t in (pa, pb, pc, pd))

    ya, st_a = _gated_mixer(pa, p["wg"], p["par_gla"], _state_to_blockdiag_t(s_gla), gated_consts, "gla")
    yd, st_d = _gated_mixer(pd, p["wg"], p["par_hg"], _state_to_blockdiag_t(s_hg), gated_consts, "hg")
    yb, st_b = _rglru(pb, p["rg_cw"], p["rg_cb"], p["rg_w"], p["rg_bias"], p["rg_nsp"], s_rg)

    uc = _hy_short_conv(pc, p["hy_cw"], p["hy_cb"])
    filt = _hy_filters(pe, p["hy_w1"], p["hy_b1"], p["hy_w2"], p["hy_b2"], p["hy_w3"], p["hy_decay"])
    z1 = _hy_long_conv(uc, None, 0, 1, filt, 0, p["hy_skip"], 0, tables, bb)
    yc = _hy_long_conv(uc, z1, 0, 2, filt, 1, p["hy_skip"], 1, tables, bb)

    ys = [t.reshape(bm, lm, GROUP_W) for t in (ya, yb, yc, yd)]
    x = _out_ffn(x, ys, mod4, layer, row0, p["norm2"], final_g, p["w_out"], p["w1"], p["w3"], p["w2"], final)
    return x, (_blockdiag_t_to_state(st_a), st_b, _blockdiag_t_to_state(st_d))


def kernel(x_prompt, x_sample, state_gla, state_rglru, state_hgrn, c, c_ctx, norm1_g, norm2_g, final_norm_g, w_mod, b_mod, w_in, w_out, gla_w_gate, gla_b_gate, gla_norm_g, rg_conv_w, rg_conv_b, rg_w_a, rg_b_a, rg_w_x, rg_b_x, rg_lambda, hy_conv_w, hy_conv_b, hy_w1, hy_b1, hy_w2, hy_b2, hy_w3, hy_decay, hy_skip, hg_lower, hg_norm_g, ffn_w1, ffn_w3, ffn_w2):
    depth = w_in.shape[0]
    nb, seq, d = x_prompt.shape
    db, dseq, _ = x_sample.shape

    hg_lb = jnp.cumsum(jax.nn.softmax(hg_lower.astype(F32), axis=0), axis=0)
    hg_lb = hg_lb - hg_lb[0:1]

    cvec = jnp.concatenate([c_ctx[None, :], c, jnp.zeros((SUBLANE - 1 - db, d), F32)], axis=0)
    mod4 = _modulation(cvec, w_mod, b_mod).reshape(depth, SUBLANE, 1, N_MOD * d)

    gated_consts = _gated_consts()
    consts_p = (gated_consts, _hy_pos_features(seq), _dft_tables(seq), 8)
    consts_s = (gated_consts, _hy_pos_features(dseq), _dft_tables(dseq), 1)

    xp = x_prompt.reshape(1, nb * seq, d)
    xs = _add_pos(x_sample, _grid_position_embedding(dseq, d))
    zero_gla = jnp.zeros((nb, 2, N_HEADS, HEAD_D, HEAD_D), F32)
    zero_rg = jnp.zeros((nb, 2, GROUP_W), F32)
    final_g = final_norm_g[None, :]

    gla_states, rg_states, hg_states = [], [], []
    for l in range(depth):
        p = _layer_params(l, hg_lb, norm1_g, norm2_g, w_in, w_out, gla_w_gate, gla_b_gate, gla_norm_g,
                          rg_conv_w, rg_conv_b, rg_w_a, rg_b_a, rg_w_x, rg_b_x, rg_lambda,
                          hy_conv_w, hy_conv_b, hy_w1, hy_b1, hy_w2, hy_b2, hy_w3, hy_decay, hy_skip,
                          hg_norm_g, ffn_w1, ffn_w3, ffn_w2)
        final = l == depth - 1
        xp, (sg, sr, sh) = _trunk_layer(xp, p, mod4, l, 0, (nb, seq), zero_gla, zero_rg, zero_gla,
                                        consts_p, final, final_g)
        xs, _ = _trunk_layer(xs, p, mod4, l, 1, (db, dseq), state_gla[:, l], state_rglru[:, l],
                             state_hgrn[:, l], consts_s, final, final_g)
        gla_states.append(sg)
        rg_states.append(sr)
        hg_states.append(sh)

    return (xp.reshape(nb, seq, d), xs,
            jnp.stack(gla_states, axis=1), jnp.stack(rg_states, axis=1), jnp.stack(hg_states, axis=1))
```

```python
import functools
import math

import jax
import jax.numpy as jnp
from jax import lax
from jax.experimental import pallas as pl
from jax.experimental.pallas import tpu as pltpu

F32 = jnp.float32
BF16 = jnp.bfloat16

D_MODEL = 1024
N_MOD = 6
GROUP_W = 256
N_HEADS = 4
HEAD_D = GROUP_W // N_HEADS
GLA_LOWRANK = 16
GLA_GATE_TEMP = 16.0
RG_C = 8.0
RG_CONV_W = 4
HY_POS_BANDS = 16
HY_FFN_W = 64
D_FF = 2816
EPS = 1e-6
GRID_W = 64
POS_BASE = 10000.0

LANE = 128
SUBLANE = 8
VMEM_LIMIT = 56 * 1024 * 1024

LOG2E = 1.0 / math.log(2.0)
FAST_LOG2_RANGE = 96.0
SUB = 16
LC = 256
N_SUB = LC // SUB
BLK = 64
LR_PAD = LANE
W_GLA = 4 * GROUP_W + LR_PAD
W_RG = 2 * GROUP_W
W_HY = 3 * GROUP_W
W_HG = 5 * GROUP_W
W_PROJ = W_GLA + W_RG + W_HY + W_HG
TM = 512
TF = D_FF // 2
TN_MOD = 512
RG_HALF = GROUP_W // 2
RG_ROWS = 256
FREQ_TILE = 256
TWID = 64


def _dot(a, b):
    return jnp.dot(a, b, preferred_element_type=F32)


def _split(x):
    hi = x.astype(BF16)
    lo = (x - hi.astype(F32)).astype(BF16)
    return hi, lo


def _dot_x2(x, w):
    hi, lo = _split(x)
    return _dot(hi, w) + _dot(lo, w)


def _dot_w2(w, x):
    hi, lo = _split(x)
    return _dot(w, hi) + _dot(w, lo)


def _dot3(a, b):
    ah, al = _split(a)
    bh, bl = _split(b)
    return _dot(ah, bh) + _dot(al, bh) + _dot(ah, bl)


def _sigmoid(x):
    return 1.0 / (1.0 + jnp.exp(-x))


def _silu(x):
    return x * _sigmoid(x)


def _log1p_exp_neg_abs(x):
    return jnp.log(1.0 + jnp.exp(-jnp.abs(x)))


def _log_sigmoid(x):
    return jnp.minimum(x, 0.0) - _log1p_exp_neg_abs(x)


def _rms(x):
    return x * lax.rsqrt(jnp.mean(x * x, axis=-1, keepdims=True) + EPS)


def _params(sem, vmem=VMEM_LIMIT):
    return pltpu.CompilerParams(dimension_semantics=sem, vmem_limit_bytes=vmem)


def _mod_kernel(c_ref, w_ref, b_ref, o_ref):
    c = c_ref[...]
    o_ref[0] = _dot3(_silu(c), w_ref[0]) + b_ref[0]


def _modulation(cvec, w_mod, b_mod):
    depth = w_mod.shape[0]
    n = N_MOD * D_MODEL
    return pl.pallas_call(
        _mod_kernel,
        out_shape=jax.ShapeDtypeStruct((depth, SUBLANE, n), F32),
        grid=(depth, n // TN_MOD),
        in_specs=[
            pl.BlockSpec((SUBLANE, D_MODEL), lambda l, j: (0, 0)),
            pl.BlockSpec((1, D_MODEL, TN_MOD), lambda l, j: (l, 0, j)),
            pl.BlockSpec((1, 1, TN_MOD), lambda l, j: (l, 0, j)),
        ],
        out_specs=pl.BlockSpec((1, SUBLANE, TN_MOD), lambda l, j: (l, 0, j)),
        compiler_params=_params(("parallel", "parallel")),
        name="modulation",
    )(cvec, w_mod, b_mod.reshape(depth, 1, n))


def _add_kernel(x_ref, p_ref, o_ref):
    o_ref[0] = x_ref[0] + p_ref[...]


def _add_pos(x, pos):
    b, l, d = x.shape
    return pl.pallas_call(
        _add_kernel,
        out_shape=jax.ShapeDtypeStruct(x.shape, F32),
        grid=(b, l // TM),
        in_specs=[pl.BlockSpec((1, TM, d), lambda i, j: (i, j, 0)),
                  pl.BlockSpec((TM, d), lambda i, j: (j, 0))],
        out_specs=pl.BlockSpec((1, TM, d), lambda i, j: (i, j, 0)),
        compiler_params=_params(("parallel", "parallel")),
        name="add_pos",
    )(x, pos)


def _proj_kernel(x_ref, mod_ref, g_ref, w_ref, oa_ref, ob_ref, oc_ref, od_ref):
    x = x_ref[0]
    m = mod_ref[0, 0]
    sh = m[:, 0:D_MODEL]
    sc = m[:, D_MODEL:2 * D_MODEL]
    u = _rms(x) * g_ref[...] * (1.0 + sc) + sh
    p = _dot(u.astype(BF16), w_ref[...])
    oa_ref[0] = p[:, 0:W_GLA]
    ob_ref[0] = p[:, W_GLA:W_GLA + W_RG]
    oc_ref[0] = p[:, W_GLA + W_RG:W_GLA + W_RG + W_HY]
    od_ref[0] = p[:, W_GLA + W_RG + W_HY:W_PROJ]


def _norm_proj(x, mod4, layer, row0, gain, w):
    bm, lm, d = x.shape
    widths = (W_GLA, W_RG, W_HY, W_HG)
    return pl.pallas_call(
        _proj_kernel,
        out_shape=[jax.ShapeDtypeStruct((bm, lm, wd), F32) for wd in widths],
        grid=(bm, lm // TM),
        in_specs=[
            pl.BlockSpec((1, TM, d), lambda i, j: (i, j, 0)),
            pl.BlockSpec((1, 1, 1, N_MOD * d), lambda i, j: (layer, row0 + i, 0, 0)),
            pl.BlockSpec((1, d), lambda i, j: (0, 0)),
            pl.BlockSpec((d, W_PROJ), lambda i, j: (0, 0)),
        ],
        out_specs=[pl.BlockSpec((1, TM, wd), lambda i, j: (i, j, 0)) for wd in widths],
        compiler_params=_params(("parallel", "parallel")),
        name="norm_proj",
    )(x, mod4, gain, w)


def _ffn_kernel(x_ref, ya_ref, yb_ref, yc_ref, yd_ref, mod_ref, g2_ref, gf_ref,
                wo_ref, w1_ref, w3_ref, w2_ref, o_ref, x1_s, u_s, acc_s, *, nf, final):
    f = pl.program_id(2)
    d = D_MODEL

    @pl.when(f == 0)
    def _():
        m = mod_ref[0, 0]
        g1 = m[:, 2 * d:3 * d]
        sh2 = m[:, 3 * d:4 * d]
        sc2 = m[:, 4 * d:5 * d]
        y = jnp.concatenate([ya_ref[0], yb_ref[0], yc_ref[0], yd_ref[0]], axis=-1)
        x1 = x_ref[0] + g1 * _dot(y.astype(BF16), wo_ref[...])
        x1_s[...] = x1
        u_s[...] = (_rms(x1) * g2_ref[...] * (1.0 + sc2) + sh2).astype(BF16)
        acc_s[...] = jnp.zeros_like(acc_s)

    u = u_s[...]
    h = _silu(_dot(u, w1_ref[...])) * _dot(u, w3_ref[...])
    acc_s[...] += _dot(h.astype(BF16), w2_ref[...])

    @pl.when(f == nf - 1)
    def _():
        g2 = mod_ref[0, 0][:, 5 * d:6 * d]
        xo = x1_s[...] + g2 * acc_s[...]
        if final:
            xo = _rms(xo) * gf_ref[...]
        o_ref[0] = xo


def _out_ffn(x, ys, mod4, layer, row0, g2, gf, wo, w1, w3, w2, final):
    bm, lm, d = x.shape
    nf = D_FF // TF
    tok = lambda wd: pl.BlockSpec((1, TM, wd), lambda i, j, f: (i, j, 0))
    return pl.pallas_call(
        functools.partial(_ffn_kernel, nf=nf, final=final),
        out_shape=jax.ShapeDtypeStruct(x.shape, F32),
        grid=(bm, lm // TM, nf),
        in_specs=[
            tok(d), tok(GROUP_W), tok(GROUP_W), tok(GROUP_W), tok(GROUP_W),
            pl.BlockSpec((1, 1, 1, N_MOD * d), lambda i, j, f: (layer, row0 + i, 0, 0)),
            pl.BlockSpec((1, d), lambda i, j, f: (0, 0)),
            pl.BlockSpec((1, d), lambda i, j, f: (0, 0)),
            pl.BlockSpec((d, d), lambda i, j, f: (0, 0)),
            pl.BlockSpec((d, TF), lambda i, j, f: (0, f)),
            pl.BlockSpec((d, TF), lambda i, j, f: (0, f)),
            pl.BlockSpec((TF, d), lambda i, j, f: (f, 0)),
        ],
        out_specs=tok(d),
        scratch_shapes=[pltpu.VMEM((TM, d), F32), pltpu.VMEM((TM, d), BF16), pltpu.VMEM((TM, d), F32)],
        compiler_params=_params(("parallel", "parallel", "arbitrary")),
        name="out_ffn",
    )(x, *ys, mod4, g2, gf, wo, w1, w3, w2)


def _gated_prelude(blk, d, wg_ref, par_ref, tri_ref, q_s, k_s, v_s, la_s, b_s, *, mode):
    w = GROUP_W
    if mode == "gla":
        q = blk[:, 0:w] * (HEAD_D ** -0.5)
        k = blk[:, w:2 * w]
        v = blk[:, 2 * w:3 * w]
        x = _dot(blk[:, 4 * w:4 * w + LR_PAD].astype(BF16), wg_ref[d]) + par_ref[d:d + 1, :]
        la = _log_sigmoid(x) * (1.0 / GLA_GATE_TEMP)
    else:
        q = _silu(blk[:, 0:w])
        f = blk[:, (1 + d) * w:(2 + d) * w]
        v = blk[:, 3 * w:4 * w]
        k = par_ref[0:1, :] * _sigmoid(-f)
        y = par_ref[2:3, :] + _log_sigmoid(f)
        lb = par_ref[1:2, :]
        la = jnp.maximum(lb, y) + _log1p_exp_neg_abs(lb - y)
    q_s[d] = q
    k_s[d] = k
    v_s[d] = v
    la_s[d] = la * LOG2E
    b_s[d] = _dot_w2(tri_ref[d], la_s[d])


def _gated_subblock(d, i, ones_ref, mbd_ref, q_s, k_s, v_s, b_s, p_s, o_s, st_s):
    w = GROUP_W
    rev = d == 1
    tio = lax.broadcasted_iota(jnp.int32, (SUB, w), 0)
    edge = 0 if rev else SUB - 1
    rows = pl.ds(pl.multiple_of(i * SUB, SUB), SUB)
    bb = b_s[d, rows, :]
    qb = q_s[d, rows, :]
    kb = k_s[d, rows, :]
    vb = v_s[d, rows, :]

    def row(a, s):
        return jnp.broadcast_to(a[s:s + 1, :], (SUB, w))

    for s in range(SUB):
        valid = (tio <= s) if rev else (tio >= s)
        e = jnp.exp2(jnp.where(valid, bb - row(bb, s), -jnp.inf))
        p_s[d, s * SUB:(s + 1) * SUB, :] = (e * qb * row(kb, s)).astype(BF16)
    r = _dot(p_s[d], ones_ref[...])
    od = jnp.zeros((SUB, w), F32)
    for s in range(SUB):
        od = od + r[s * SUB:(s + 1) * SUB, :] * row(vb, s)
    bend = row(bb, edge)
    st = st_s[d]
    qt = (qb * jnp.exp2(bb)).astype(BF16)
    oi = lax.dot_general(qt, st.astype(BF16), (((1,), (1,)), ((), ())), preferred_element_type=F32)
    kt = (kb * jnp.exp2(bend - bb)).astype(BF16)
    kv = lax.dot_general(vb.astype(BF16), kt, (((0,), (0,)), ((), ())), preferred_element_type=F32)
    st_s[d] = st * jnp.exp2(bend[0:1, :]) + kv * mbd_ref[...]
    o_s[d, rows, :] = od + oi


def _stack_heads(x):
    lane_head = lax.broadcasted_iota(jnp.int32, x.shape, 1) // HEAD_D
    return jnp.concatenate([jnp.where(lane_head == h, x, 0.0) for h in range(N_HEADS)], axis=0).astype(BF16)


def _block_centered(bb):
    return bb - jnp.broadcast_to(bb[BLK // 2:BLK // 2 + 1, :], bb.shape)


def _gated_block_bounded(d, g, q_s, k_s, v_s, b_s, o_s, st_s):
    w = GROUP_W
    rev = d == 1
    edge = 0 if rev else BLK - 1
    rows = slice(g * BLK, (g + 1) * BLK)
    bb = b_s[d, rows, :]
    qb = q_s[d, rows, :]
    kb = k_s[d, rows, :]
    cc = _block_centered(bb)
    vm = _stack_heads(v_s[d, rows, :])
    km = _stack_heads(kb * jnp.exp2(-cc))
    sc = lax.dot_general((qb * jnp.exp2(cc)).astype(BF16), km, (((1,), (1,)), ((), ())),
                         preferred_element_type=F32)
    t_i = lax.broadcasted_iota(jnp.int32, sc.shape, 0)
    s_i = lax.broadcasted_iota(jnp.int32, sc.shape, 1) % BLK
    od = _dot(jnp.where((t_i <= s_i) if rev else (t_i >= s_i), sc, 0.0).astype(BF16), vm)
    bend = jnp.broadcast_to(bb[edge:edge + 1, :], (BLK, w))
    st = st_s[d]
    oi = lax.dot_general((qb * jnp.exp2(bb)).astype(BF16), st.astype(BF16), (((1,), (1,)), ((), ())),
                         preferred_element_type=F32)
    ktm = _stack_heads(kb * jnp.exp2(bend - bb))
    kv = lax.dot_general(vm, ktm, (((0,), (0,)), ((), ())), preferred_element_type=F32)
    st_s[d] = st * jnp.exp2(bend[0:1, :]) + kv
    o_s[d, rows, :] = od + oi


def _gated_kernel(pf_ref, pb_ref, wg_ref, par_ref, s0_ref, tri_ref, tri_sub_ref, ones_ref, mbd_ref,
                  y_ref, st_ref, q_s, k_s, v_s, la_s, b_s, p_s, o_s, st_s, *, mode, nc):
    j = pl.program_id(1)
    w = GROUP_W

    @pl.when(j == 0)
    def _():
        st_s[0] = s0_ref[0, 0]
        st_s[1] = s0_ref[0, 1]

    _gated_prelude(pf_ref[0], 0, wg_ref, par_ref, tri_ref, q_s, k_s, v_s, la_s, b_s, mode=mode)
    _gated_prelude(pb_ref[0], 1, wg_ref, par_ref, tri_ref, q_s, k_s, v_s, la_s, b_s, mode=mode)

    span = jnp.float32(0.0)
    for d in range(2):
        for g in range(LC // BLK):
            span = jnp.maximum(span, jnp.max(jnp.abs(_block_centered(b_s[d, g * BLK:(g + 1) * BLK, :]))))
    bounded = span < FAST_LOG2_RANGE

    @pl.when(bounded)
    def _():
        for g in range(LC // BLK):
            _gated_block_bounded(0, g, q_s, k_s, v_s, b_s, o_s, st_s)
            _gated_block_bounded(1, LC // BLK - 1 - g, q_s, k_s, v_s, b_s, o_s, st_s)

    @pl.when(jnp.logical_not(bounded))
    def _():
        for d in range(2):
            b_s[d] = _dot_w2(tri_sub_ref[d], la_s[d])

        def body(it, carry):
            scratch = (q_s, k_s, v_s, b_s, p_s, o_s, st_s)
            _gated_subblock(0, it, ones_ref, mbd_ref, *scratch)
            _gated_subblock(1, N_SUB - 1 - it, ones_ref, mbd_ref, *scratch)
            return carry

        lax.fori_loop(0, N_SUB, body, 0)

    @pl.when(j == nc - 1)
    def _():
        for d in range(2):
            for h in range(N_HEADS):
                blk = pl.ds(h * HEAD_D, HEAD_D)
                st_ref[0, d, h] = st_s[d, blk, blk]

    gcol = 3 * w if mode == "gla" else 4 * w

    def finish(o, p_ref):
        ms = _dot_x2(o * o, ones_ref[...]) * (1.0 / HEAD_D)
        return o * lax.rsqrt(ms + EPS) * par_ref[3:4, :] * _silu(p_ref[0][:, gcol:gcol + w])

    rows_f = pl.ds(pl.multiple_of(j * LC, LC), LC)
    rows_b = pl.ds(pl.multiple_of((nc - 1 - j) * LC, LC), LC)
    if nc == 1:
        y_ref[0] = finish(o_s[0] + o_s[1], pf_ref)
    else:
        @pl.when(j < nc // 2)
        def _():
            y_ref[0, rows_f, :] = o_s[0]
            y_ref[0, rows_b, :] = o_s[1]

        @pl.when(j >= nc // 2)
        def _():
            y_ref[0, rows_f, :] = finish(y_ref[0, rows_f, :] + o_s[0], pf_ref)
            y_ref[0, rows_b, :] = finish(y_ref[0, rows_b, :] + o_s[1], pb_ref)


def _gated_consts():
    r = jnp.arange(LC)

    def tri(block):
        same = (r[:, None] // block) == (r[None, :] // block)
        lower = (same & (r[None, :] <= r[:, None])).astype(BF16)
        upper = (same & (r[None, :] >= r[:, None])).astype(BF16)
        return jnp.stack([lower, upper])

    head = (r[:, None] // HEAD_D) == (r[None, :] // HEAD_D)
    return tri(BLK), tri(SUB), head.astype(BF16), head.astype(F32)


def _gated_mixer(p, wg, par, s0t, consts, mode):
    b, l, width = p.shape
    nc = l // LC
    assert nc == 1 or nc % 2 == 0
    tri, tri_sub, ones, mbd = consts
    w = GROUP_W
    const2 = lambda shape: pl.BlockSpec(shape, lambda i, j: (0,) * len(shape))
    return pl.pallas_call(
        functools.partial(_gated_kernel, mode=mode, nc=nc),
        out_shape=[jax.ShapeDtypeStruct((b, l, w), F32),
                   jax.ShapeDtypeStruct((b, 2, N_HEADS, HEAD_D, HEAD_D), F32)],
        grid=(b, nc),
        in_specs=[
            pl.BlockSpec((1, LC, width), lambda i, j: (i, j, 0)),
            pl.BlockSpec((1, LC, width), lambda i, j: (i, nc - 1 - j, 0)),
            const2(wg.shape), const2(par.shape),
            pl.BlockSpec((1, 2, w, w), lambda i, j: (i, 0, 0, 0)),
            const2(tri.shape), const2(tri_sub.shape), const2(ones.shape), const2(mbd.shape),
        ],
        out_specs=[pl.BlockSpec((1, l, w), lambda i, j: (i, 0, 0)),
                   pl.BlockSpec((1, 2, N_HEADS, HEAD_D, HEAD_D), lambda i, j: (i, 0, 0, 0, 0))],
        scratch_shapes=[pltpu.VMEM((2, LC, w), F32), pltpu.VMEM((2, LC, w), F32), pltpu.VMEM((2, LC, w), F32),
                        pltpu.VMEM((2, LC, w), F32), pltpu.VMEM((2, LC, w), F32),
                        pltpu.VMEM((2, SUB * SUB, w), BF16),
                        pltpu.VMEM((2, LC, w), F32), pltpu.VMEM((2, w, w), F32)],
        compiler_params=_params(("parallel", "arbitrary")),
        name="gated_" + mode,
    )(p, p, wg, par, s0t, tri, tri_sub, ones, mbd)


def _state_to_blockdiag_t(s):
    b = s.shape[0]
    st = jnp.swapaxes(s, -1, -2)
    eye = jnp.eye(N_HEADS, dtype=s.dtype)
    full = st[:, :, :, :, None, :] * eye[None, None, :, None, :, None]
    return full.reshape(b, 2, GROUP_W, GROUP_W)


def _blockdiag_t_to_state(st):
    return jnp.swapaxes(st, -1, -2)


def _rglru_kernel(x_ref, g_ref, cw_ref, cb_ref, w_ref, bias_ref, nsp_ref, h0_ref, y_ref, hT_ref,
                  xc_s, a_s, u_s, h_s, *, l):
    c = RG_HALF
    x = x_ref[0]
    row = lax.broadcasted_iota(jnp.int32, (l, c), 0)
    xc = x * cw_ref[2:3, :] + cb_ref[...]
    xc = xc + jnp.where(row >= 2, pltpu.roll(x, 2, 0), 0.0) * cw_ref[0:1, :]
    xc = xc + jnp.where(row >= 1, pltpu.roll(x, 1, 0), 0.0) * cw_ref[1:2, :]
    xc = xc + jnp.where(row <= l - 2, pltpu.roll(x, l - 1, 0), 0.0) * cw_ref[3:4, :]
    xc_s[...] = xc

    nslab = l // RG_ROWS
    grp = (RG_ROWS // SUBLANE, SUBLANE, c)
    sub = lax.broadcasted_iota(jnp.int32, grp, 1)

    def slab(n, carry):
        rows = pl.ds(pl.multiple_of(n * RG_ROWS, RG_ROWS), RG_ROWS)
        xs = xc_s[rows, :]
        gates = _sigmoid(_dot(xs.astype(BF16), w_ref[...]) + bias_ref[...])
        for d in range(2):
            r = gates[:, (2 * d) * c:(2 * d + 1) * c]
            i = gates[:, (2 * d + 1) * c:(2 * d + 2) * c]
            log_a = r * nsp_ref[d:d + 1, :]
            a = jnp.exp(log_a).reshape(grp)
            u = (jnp.sqrt(1.0 - jnp.exp(2.0 * log_a)) * (i * xs)).reshape(grp)
            for sft in (1, 2, 4):
                if d == 0:
                    ok = sub >= sft
                    a_n, u_n = pltpu.roll(a, sft, 1), pltpu.roll(u, sft, 1)
                else:
                    ok = sub <= SUBLANE - 1 - sft
                    a_n, u_n = pltpu.roll(a, SUBLANE - sft, 1), pltpu.roll(u, SUBLANE - sft, 1)
                u = jnp.where(ok, a * u_n + u, u)
                a = jnp.where(ok, a * a_n, a)
            a_s[d, rows, :] = a.reshape(RG_ROWS, c)
            u_s[d, rows, :] = u.reshape(RG_ROWS, c)
        return carry

    lax.fori_loop(0, nslab, slab, 0)

    ngrp = l // SUBLANE

    def fwd(n, h):
        rows = pl.ds(pl.multiple_of(n * SUBLANE, SUBLANE), SUBLANE)
        hh = a_s[0, rows, :] * h + u_s[0, rows, :]
        h_s[rows, :] = hh
        return jnp.broadcast_to(hh[SUBLANE - 1:SUBLANE, :], (SUBLANE, c))

    hf = lax.fori_loop(0, ngrp, fwd, jnp.broadcast_to(h0_ref[0, 0:1, :], (SUBLANE, c)))

    def bwd(n, h):
        rows = pl.ds(pl.multiple_of((ngrp - 1 - n) * SUBLANE, SUBLANE), SUBLANE)
        hh = a_s[1, rows, :] * h + u_s[1, rows, :]
        h_s[rows, :] = h_s[rows, :] + hh
        return jnp.broadcast_to(hh[0:1, :], (SUBLANE, c))

    hb = lax.fori_loop(0, ngrp, bwd, jnp.broadcast_to(h0_ref[0, 1:2, :], (SUBLANE, c)))
    hT_ref[0, 0:1, :] = hf[0:1, :]
    hT_ref[0, 1:2, :] = hb[0:1, :]

    g = g_ref[0]
    gelu = 0.5 * g * (1.0 + jnp.tanh(math.sqrt(2.0 / math.pi) * (g + 0.044715 * (g * g * g))))
    y_ref[0] = h_s[...] * gelu


def _rglru(p, cw, cb, wbd, bias, nsp, h0):
    b, l, _ = p.shape
    c = RG_HALF
    half = lambda shape: pl.BlockSpec(shape, lambda i, j: (0,) * (len(shape) - 1) + (j,))
    return pl.pallas_call(
        functools.partial(_rglru_kernel, l=l),
        out_shape=[jax.ShapeDtypeStruct((b, l, GROUP_W), F32), jax.ShapeDtypeStruct((b, 2, GROUP_W), F32)],
        grid=(b, 2),
        in_specs=[
            pl.BlockSpec((1, l, c), lambda i, j: (i, 0, j)),
            pl.BlockSpec((1, l, c), lambda i, j: (i, 0, 2 + j)),
            half((RG_CONV_W, c)), half((1, c)),
            pl.BlockSpec((None, c, 4 * c), lambda i, j: (j, 0, 0)),
            pl.BlockSpec((None, 1, 4 * c), lambda i, j: (j, 0, 0)),
            half((2, c)),
            pl.BlockSpec((1, 2, c), lambda i, j: (i, 0, j)),
        ],
        out_specs=[pl.BlockSpec((1, l, c), lambda i, j: (i, 0, j)),
                   pl.BlockSpec((1, 2, c), lambda i, j: (i, 0, j))],
        scratch_shapes=[pltpu.VMEM((l, c), F32), pltpu.VMEM((2, l, c), F32), pltpu.VMEM((2, l, c), F32),
                        pltpu.VMEM((l, c), F32)],
        compiler_params=_params(("parallel", "parallel")),
        name="rglru",
    )(p, p, cw, cb, wbd, bias, nsp, h0)


def _hy_pre_kernel(c_ref, w_ref, b_ref, o_ref, *, l):
    x = c_ref[0]
    row = lax.broadcasted_iota(jnp.int32, x.shape, 0)
    o = x * w_ref[1:2, :] + b_ref[...]
    o = o + jnp.where(row >= 1, pltpu.roll(x, 1, 0), 0.0) * w_ref[0:1, :]
    o = o + jnp.where(row <= l - 2, pltpu.roll(x, l - 1, 0), 0.0) * w_ref[2:3, :]
    o_ref[0] = o


def _hy_short_conv(p, w, bias):
    b, l, wd = p.shape
    c = GROUP_W
    return pl.pallas_call(
        functools.partial(_hy_pre_kernel, l=l),
        out_shape=jax.ShapeDtypeStruct(p.shape, F32),
        grid=(b, wd // c),
        in_specs=[pl.BlockSpec((1, l, c), lambda i, j: (i, 0, j)),
                  pl.BlockSpec((3, c), lambda i, j: (0, j)),
                  pl.BlockSpec((1, c), lambda i, j: (0, j))],
        out_specs=pl.BlockSpec((1, l, c), lambda i, j: (i, 0, j)),
        compiler_params=_params(("parallel", "parallel")),
        name="hy_short_conv",
    )(p, w, bias)


def _hy_filter_kernel(pe_ref, w1_ref, b1_ref, w2_ref, b2_ref, w3_ref, dec_ref, o_ref):
    pe = pe_ref[...]
    h = jnp.sin(_dot3(pe, w1_ref[...]) + b1_ref[...])
    h = jnp.sin(_dot3(h, w2_ref[...]) + b2_ref[...])
    h = _dot3(h, w3_ref[...])
    dist = pe[:, LANE - 1:LANE]
    h = h * jnp.exp(-dist * dec_ref[...])
    o_ref[...] = h / jnp.sum(jnp.abs(h), axis=0, keepdims=True)


def _hy_filters(pe, w1, b1, w2, b2, w3, decay):
    l = pe.shape[0]
    n = w3.shape[1]
    args = (pe, w1, b1, w2, b2, w3, decay)
    return pl.pallas_call(
        _hy_filter_kernel,
        out_shape=jax.ShapeDtypeStruct((l, n), F32),
        grid=(1,),
        in_specs=[pl.BlockSpec(a.shape, lambda i: (0, 0)) for a in args],
        out_specs=pl.BlockSpec((l, n), lambda i: (0, 0)),
        compiler_params=_params(("arbitrary",)),
        name="hy_filters",
    )(*args)


def _hy_conv_kernel(z_ref, x_ref, h_ref, skip_ref, cf_ref, sf_ref, ci_ref, si_ref, o_ref, *, bb, nk):
    kt = pl.program_id(1)
    c = GROUP_W

    @pl.when(kt == 0)
    def _():
        o_ref[...] = jnp.zeros_like(o_ref)

    zs = [z_ref[b].astype(BF16) for b in range(bb)] + [h_ref[...].astype(BF16)]
    zcat = jnp.concatenate(zs, axis=-1)
    xc = _dot(cf_ref[...], zcat)
    xs = _dot(sf_ref[...], zcat)
    hc = xc[:, bb * c:]
    hs = xs[:, bb * c:]
    for b in range(bb):
        zc = xc[:, b * c:(b + 1) * c]
        zsn = xs[:, b * c:(b + 1) * c]
        yc = (zc * hc - zsn * hs).astype(BF16)
        ys = (zc * hs + zsn * hc).astype(BF16)
        o_ref[b] += _dot(ci_ref[...], yc) + _dot(si_ref[...], ys)

    @pl.when(kt == nk - 1)
    def _():
        for b in range(bb):
            z = z_ref[b]
            o_ref[b] = x_ref[b] * (o_ref[b] + skip_ref[...] * z)


def _hy_long_conv(uc, zprev, z_col, gate_col, h, h_col, skip, skip_col, tables, bb):
    b, l, _ = uc.shape
    c = GROUP_W
    cf, sf, ci, si = tables
    nk = cf.shape[0] // FREQ_TILE
    z_arr = uc if zprev is None else zprev
    return pl.pallas_call(
        functools.partial(_hy_conv_kernel, bb=bb, nk=nk),
        out_shape=jax.ShapeDtypeStruct((b, l, c), F32),
        grid=(b // bb, nk),
        in_specs=[
            pl.BlockSpec((bb, l, c), lambda g, k: (g, 0, z_col)),
            pl.BlockSpec((bb, l, c), lambda g, k: (g, 0, gate_col)),
            pl.BlockSpec((l, c), lambda g, k: (0, h_col)),
            pl.BlockSpec((1, c), lambda g, k: (0, skip_col)),
            pl.BlockSpec((FREQ_TILE, l), lambda g, k: (k, 0)),
            pl.BlockSpec((FREQ_TILE, l), lambda g, k: (k, 0)),
            pl.BlockSpec((l, FREQ_TILE), lambda g, k: (0, k)),
            pl.BlockSpec((l, FREQ_TILE), lambda g, k: (0, k)),
        ],
        out_specs=pl.BlockSpec((bb, l, c), lambda g, k: (g, 0, 0)),
        compiler_params=_params(("parallel", "arbitrary")),
        name="hy_long_conv",
    )(z_arr, uc, h, skip, cf, sf, ci, si)


def _dft_tables(l):
    n = 3 * l // 2
    nf = n // 2 + 1
    nfp = -(-nf // FREQ_TILE) * FREQ_TILE
    k = jnp.arange(nfp, dtype=jnp.int32)
    ta = jnp.arange(l // TWID, dtype=jnp.int32) * TWID
    tb = jnp.arange(TWID, dtype=jnp.int32)
    live = (k < nf)

    def cos_sin(m):
        ang = (m % n).astype(F32) * (2.0 * math.pi / n)
        return jnp.cos(ang), jnp.sin(ang)

    ca, sa = cos_sin(k[:, None] * ta[None, :])
    cb, sb = cos_sin(k[:, None] * tb[None, :])
    ca, sa = (jnp.where(live[:, None], x, 0.0)[:, :, None] for x in (ca, sa))
    cb, sb = cb[:, None, :], sb[:, None, :]
    cf = (ca * cb - sa * sb).reshape(nfp, l).astype(BF16)
    sf = (sa * cb + ca * sb).reshape(nfp, l).astype(BF16)
    wk = jnp.where((k == 0) | (k == n // 2), 1.0, 2.0) / n
    wk = jnp.where(live, wk, 0.0).astype(F32)
    ca, sa = cos_sin((ta + l // 2)[:, None] * k[None, :])
    cb, sb = cos_sin(tb[:, None] * k[None, :])
    ca, sa = ((wk[None, :] * x)[:, None, :] for x in (ca, sa))
    cb, sb = cb[None, :, :], sb[None, :, :]
    ci = (ca * cb - sa * sb).reshape(l, nfp).astype(BF16)
    si = (sa * cb + ca * sb).reshape(l, nfp).astype(BF16)
    return cf, sf, ci, si


def _hy_pos_features(l):
    pos = jnp.arange(l, dtype=F32)
    t = pos / (l - 1)
    ang = (2.0 * jnp.pi * pos / l)[:, None] * jnp.linspace(1e-4, HY_POS_BANDS - 1, HY_POS_BANDS, dtype=F32)[None, :]
    half = l // 2
    dist = jnp.abs(pos - half) / half
    pe = jnp.concatenate([t[:, None], jnp.cos(ang), -jnp.sin(ang)], axis=-1)
    pad = jnp.zeros((l, LANE - 1 - pe.shape[1]), F32)
    return jnp.concatenate([pe, pad, dist[:, None]], axis=-1)


def _grid_position_embedding(n_tokens, dim):
    rows = n_tokens // GRID_W
    row = jnp.broadcast_to(jnp.arange(rows, dtype=F32)[:, None], (rows, GRID_W)).reshape(-1)
    col = jnp.broadcast_to(jnp.arange(GRID_W, dtype=F32)[None, :], (rows, GRID_W)).reshape(-1)
    quarter = dim // 4
    omega = 1.0 / (POS_BASE ** (jnp.arange(quarter, dtype=F32) / quarter))

    def enc(pos):
        ang = pos[:, None] * omega[None, :]
        return jnp.concatenate([jnp.sin(ang), jnp.cos(ang)], axis=-1)

    return jnp.concatenate([enc(row), enc(col)], axis=-1)


def _blockdiag2(a, b):
    z = jnp.zeros_like(a)
    return jnp.concatenate([jnp.concatenate([a, z], axis=1), jnp.concatenate([z, b], axis=1)], axis=0)


def _layer_params(l, hg_lb, norm1_g, norm2_g, w_in, w_out, gla_w_gate, gla_b_gate, gla_norm_g,
                  rg_conv_w, rg_conv_b, rg_w_a, rg_b_a, rg_w_x, rg_b_x, rg_lambda,
                  hy_conv_w, hy_conv_b, hy_w1, hy_b1, hy_w2, hy_b2, hy_w3, hy_decay, hy_skip,
                  hg_norm_g, ffn_w1, ffn_w3, ffn_w2):
    d, w = D_MODEL, GROUP_W
    n_gla = 4 * w + GLA_LOWRANK
    wi = w_in[l]
    w_proj = jnp.concatenate([wi[:, :n_gla], jnp.zeros((d, LR_PAD - GLA_LOWRANK), F32), wi[:, n_gla:]], axis=1)
    wg = jnp.concatenate([gla_w_gate[l], jnp.zeros((2, LR_PAD - GLA_LOWRANK, w), F32)], axis=1)
    zrow = jnp.zeros((w,), F32)
    par_gla = jnp.stack([gla_b_gate[l, 0], gla_b_gate[l, 1], zrow, gla_norm_g[l]] + [zrow] * 4)
    lb = hg_lb[l]
    par_hg = jnp.stack([1.0 - lb, jnp.log(lb), jnp.log1p(-lb), hg_norm_g[l]] + [zrow] * 4)
    wa, wx = rg_w_a[l], rg_w_x[l]
    rg_w = jnp.stack([
        jnp.concatenate([_blockdiag2(m[dd, 2 * j], m[dd, 2 * j + 1]) for dd in range(2) for m in (wa, wx)], axis=1)
        for j in range(2)])
    ba, bx = rg_b_a[l], rg_b_x[l]
    rg_bias = jnp.stack([
        jnp.concatenate([v[dd, j * RG_HALF:(j + 1) * RG_HALF] for dd in range(2) for v in (ba, bx)])[None, :]
        for j in range(2)])
    nsp = -RG_C * jax.nn.softplus(-rg_lambda[l])
    w1p = jnp.concatenate([hy_w1[l], jnp.zeros((LANE - hy_w1.shape[1], HY_FFN_W), F32)], axis=0)
    return dict(
        norm1=norm1_g[l][None, :], norm2=norm2_g[l][None, :],
        w_proj=w_proj.astype(BF16), w_out=w_out[l].astype(BF16),
        wg=wg.astype(BF16), par_gla=par_gla, par_hg=par_hg,
        rg_cw=rg_conv_w[l], rg_cb=rg_conv_b[l][None, :], rg_w=rg_w.astype(BF16), rg_bias=rg_bias, rg_nsp=nsp,
        hy_cw=hy_conv_w[l], hy_cb=hy_conv_b[l][None, :], hy_w1=w1p, hy_b1=hy_b1[l][None, :],
        hy_w2=hy_w2[l], hy_b2=hy_b2[l][None, :], hy_w3=hy_w3[l], hy_decay=hy_decay[l][None, :],
        hy_skip=hy_skip[l][None, :],
        w1=ffn_w1[l].astype(BF16), w3=ffn_w3[l].astype(BF16), w2=ffn_w2[l].astype(BF16))


def _trunk_layer(x, p, mod4, layer, row0, seq_shape, s_gla, s_rg, s_hg, stream_consts, final, final_g):
    bm, lm, d = x.shape
    b, l = seq_shape
    gated_consts, pe, tables, bb = stream_consts
    pa, pb, pc, pd = _norm_proj(x, mod4, layer, row0, p["norm1"], p["w_proj"])
    pa, pb, pc, pd = (t.reshape(b, l, t.shape[-1]) for t in (pa, pb, pc, pd))

    ya, st_a = _gated_mixer(pa, p["wg"], p["par_gla"], _state_to_blockdiag_t(s_gla), gated_consts, "gla")
    yd, st_d = _gated_mixer(pd, p["wg"], p["par_hg"], _state_to_blockdiag_t(s_hg), gated_consts, "hg")
    yb, st_b = _rglru(pb, p["rg_cw"], p["rg_cb"], p["rg_w"], p["rg_bias"], p["rg_nsp"], s_rg)

    uc = _hy_short_conv(pc, p["hy_cw"], p["hy_cb"])
    filt = _hy_filters(pe, p["hy_w1"], p["hy_b1"], p["hy_w2"], p["hy_b2"], p["hy_w3"], p["hy_decay"])
    z1 = _hy_long_conv(uc, None, 0, 1, filt, 0, p["hy_skip"], 0, tables, bb)
    yc = _hy_long_conv(uc, z1, 0, 2, filt, 1, p["hy_skip"], 1, tables, bb)

    ys = [t.reshape(bm, lm, GROUP_W) for t in (ya, yb, yc, yd)]
    x = _out_ffn(x, ys, mod4, layer, row0, p["norm2"], final_g, p["w_out"], p["w1"], p["w3"], p["w2"], final)
    return x, (_blockdiag_t_to_state(st_a), st_b, _blockdiag_t_to_state(st_d))


def kernel(x_prompt, x_sample, state_gla, state_rglru, state_hgrn, c, c_ctx, norm1_g, norm2_g, final_norm_g, w_mod, b_mod, w_in, w_out, gla_w_gate, gla_b_gate, gla_norm_g, rg_conv_w, rg_conv_b, rg_w_a, rg_b_a, rg_w_x, rg_b_x, rg_lambda, hy_conv_w, hy_conv_b, hy_w1, hy_b1, hy_w2, hy_b2, hy_w3, hy_decay, hy_skip, hg_lower, hg_norm_g, ffn_w1, ffn_w3, ffn_w2):
    depth = w_in.shape[0]
    nb, seq, d = x_prompt.shape
    db, dseq, _ = x_sample.shape

    hg_lb = jnp.cumsum(jax.nn.softmax(hg_lower.astype(F32), axis=0), axis=0)
    hg_lb = hg_lb - hg_lb[0:1]

    cvec = jnp.concatenate([c_ctx[None, :], c, jnp.zeros((SUBLANE - 1 - db, d), F32)], axis=0)
    mod4 = _modulation(cvec, w_mod, b_mod).reshape(depth, SUBLANE, 1, N_MOD * d)

    gated_consts = _gated_consts()
    consts_p = (gated_consts, _hy_pos_features(seq), _dft_tables(seq), 8)
    consts_s = (gated_consts, _hy_pos_features(dseq), _dft_tables(dseq), 1)

    xp = x_prompt.reshape(1, nb * seq, d)
    xs = _add_pos(x_sample, _grid_position_embedding(dseq, d))
    zero_gla = jnp.zeros((nb, 2, N_HEADS, HEAD_D, HEAD_D), F32)
    zero_rg = jnp.zeros((nb, 2, GROUP_W), F32)
    final_g = final_norm_g[None, :]

    gla_states, rg_states, hg_states = [], [], []
    for l in range(depth):
        p = _layer_params(l, hg_lb, norm1_g, norm2_g, w_in, w_out, gla_w_gate, gla_b_gate, gla_norm_g,
                          rg_conv_w, rg_conv_b, rg_w_a, rg_b_a, rg_w_x, rg_b_x, rg_lambda,
                          hy_conv_w, hy_conv_b, hy_w1, hy_b1, hy_w2, hy_b2, hy_w3, hy_decay, hy_skip,
                          hg_norm_g, ffn_w1, ffn_w3, ffn_w2)
        final = l == depth - 1
        xp, (sg, sr, sh) = _trunk_layer(xp, p, mod4, l, 0, (nb, seq), zero_gla, zero_rg, zero_gla,
                                        consts_p, final, final_g)
        xs, _ = _trunk_layer(xs, p, mod4, l, 1, (db, dseq), state_gla[:, l], state_rglru[:, l],
                             state_hgrn[:, l], consts_s, final, final_g)
        gla_states.append(sg)
        rg_states.append(sr)
        hg_states.append(sh)

    return (xp.reshape(nb, seq, d), xs,
            jnp.stack(gla_states, axis=1), jnp.stack(rg_states, axis=1), jnp.stack(hg_states, axis=1))
```

```python
import functools
import math

import jax
import jax.numpy as jnp
from jax import lax
from jax.experimental import pallas as pl
from jax.experimental.pallas import tpu as pltpu

F32 = jnp.float32
BF16 = jnp.bfloat16

D_MODEL = 1024
N_MOD = 6
GROUP_W = 256
N_HEADS = 4
HEAD_D = GROUP_W // N_HEADS
GLA_LOWRANK = 16
GLA_GATE_TEMP = 16.0
RG_C = 8.0
RG_CONV_W = 4
HY_POS_BANDS = 16
HY_FFN_W = 64
D_FF = 2816
EPS = 1e-6
GRID_W = 64
POS_BASE = 10000.0

LANE = 128
SUBLANE = 8
VMEM_LIMIT = 56 * 1024 * 1024

LOG2E = 1.0 / math.log(2.0)
FAST_LOG2_RANGE = 96.0
SUB = 16
LC = 256
N_SUB = LC // SUB
BOUNDED_BLKS = (64, 32)
LR_PAD = LANE
W_GLA = 4 * GROUP_W + LR_PAD
W_RG = 2 * GROUP_W
W_HY = 3 * GROUP_W
W_HG = 5 * GROUP_W
W_PROJ = W_GLA + W_RG + W_HY + W_HG
TM = 512
TF = D_FF // 2
TN_MOD = 512
RG_HALF = GROUP_W // 2
RG_ROWS = 256
FREQ_TILE = 256
TWID = 64


def _dot(a, b):
    return jnp.dot(a, b, preferred_element_type=F32)


def _split(x):
    hi = x.astype(BF16)
    lo = (x - hi.astype(F32)).astype(BF16)
    return hi, lo


def _dot_x2(x, w):
    hi, lo = _split(x)
    return _dot(hi, w) + _dot(lo, w)


def _dot_w2(w, x):
    hi, lo = _split(x)
    return _dot(w, hi) + _dot(w, lo)


def _dot3(a, b):
    ah, al = _split(a)
    bh, bl = _split(b)
    return _dot(ah, bh) + _dot(al, bh) + _dot(ah, bl)


def _sigmoid(x):
    return 1.0 / (1.0 + jnp.exp(-x))


def _silu(x):
    return x * _sigmoid(x)


def _log1p_exp_neg_abs(x):
    return jnp.log(1.0 + jnp.exp(-jnp.abs(x)))


def _log_sigmoid(x):
    return jnp.minimum(x, 0.0) - _log1p_exp_neg_abs(x)


def _rms(x):
    return x * lax.rsqrt(jnp.mean(x * x, axis=-1, keepdims=True) + EPS)


def _params(sem, vmem=VMEM_LIMIT):
    return pltpu.CompilerParams(dimension_semantics=sem, vmem_limit_bytes=vmem)


def _mod_kernel(c_ref, w_ref, b_ref, o_ref):
    c = c_ref[...]
    o_ref[0] = _dot3(_silu(c), w_ref[0]) + b_ref[0]


def _modulation(cvec, w_mod, b_mod):
    depth = w_mod.shape[0]
    n = N_MOD * D_MODEL
    return pl.pallas_call(
        _mod_kernel,
        out_shape=jax.ShapeDtypeStruct((depth, SUBLANE, n), F32),
        grid=(depth, n // TN_MOD),
        in_specs=[
            pl.BlockSpec((SUBLANE, D_MODEL), lambda l, j: (0, 0)),
            pl.BlockSpec((1, D_MODEL, TN_MOD), lambda l, j: (l, 0, j)),
            pl.BlockSpec((1, 1, TN_MOD), lambda l, j: (l, 0, j)),
        ],
        out_specs=pl.BlockSpec((1, SUBLANE, TN_MOD), lambda l, j: (l, 0, j)),
        compiler_params=_params(("parallel", "parallel")),
        name="modulation",
    )(cvec, w_mod, b_mod.reshape(depth, 1, n))


def _add_kernel(x_ref, e_ref, o_ref):
    j = pl.program_id(1)
    half = D_MODEL // 2
    enc = e_ref[0:GRID_W, :]
    for r in range(TM // GRID_W):
        rows = slice(r * GRID_W, (r + 1) * GRID_W)
        enc_row = e_ref[pl.ds(j * (TM // GRID_W) + r, 1), :]
        o_ref[0, rows, 0:half] = x_ref[0, rows, 0:half] + enc_row
        o_ref[0, rows, half:] = x_ref[0, rows, half:] + enc


def _add_pos(x, enc):
    b, l, d = x.shape
    assert l // GRID_W <= enc.shape[0]
    return pl.pallas_call(
        _add_kernel,
        out_shape=jax.ShapeDtypeStruct(x.shape, F32),
        grid=(b, l // TM),
        in_specs=[pl.BlockSpec((1, TM, d), lambda i, j: (i, j, 0)),
                  pl.BlockSpec(enc.shape, lambda i, j: (0, 0))],
        out_specs=pl.BlockSpec((1, TM, d), lambda i, j: (i, j, 0)),
        compiler_params=_params(("parallel", "parallel")),
        name="add_pos",
    )(x, enc)


def _proj_kernel(x_ref, mod_ref, g_ref, w_ref, oa_ref, ob_ref, oc_ref, od_ref):
    x = x_ref[0]
    m = mod_ref[0, 0]
    sh = m[:, 0:D_MODEL]
    sc = m[:, D_MODEL:2 * D_MODEL]
    u = _rms(x) * g_ref[...] * (1.0 + sc) + sh
    p = _dot(u.astype(BF16), w_ref[...])
    oa_ref[0] = p[:, 0:W_GLA]
    ob_ref[0] = p[:, W_GLA:W_GLA + W_RG]
    oc_ref[0] = p[:, W_GLA + W_RG:W_GLA + W_RG + W_HY]
    od_ref[0] = p[:, W_GLA + W_RG + W_HY:W_PROJ]


def _norm_proj(x, mod4, layer, row0, gain, w):
    bm, lm, d = x.shape
    widths = (W_GLA, W_RG, W_HY, W_HG)
    return pl.pallas_call(
        _proj_kernel,
        out_shape=[jax.ShapeDtypeStruct((bm, lm, wd), F32) for wd in widths],
        grid=(bm, lm // TM),
        in_specs=[
            pl.BlockSpec((1, TM, d), lambda i, j: (i, j, 0)),
            pl.BlockSpec((1, 1, 1, N_MOD * d), lambda i, j: (layer, row0 + i, 0, 0)),
            pl.BlockSpec((1, d), lambda i, j: (0, 0)),
            pl.BlockSpec((d, W_PROJ), lambda i, j: (0, 0)),
        ],
        out_specs=[pl.BlockSpec((1, TM, wd), lambda i, j: (i, j, 0)) for wd in widths],
        compiler_params=_params(("parallel", "parallel")),
        name="norm_proj",
    )(x, mod4, gain, w)


def _ffn_kernel(x_ref, ya_ref, yb_ref, yc_ref, yd_ref, mod_ref, g2_ref, gf_ref,
                wo_ref, w1_ref, w3_ref, w2_ref, o_ref, x1_s, u_s, acc_s, *, nf, final):
    f = pl.program_id(2)
    d = D_MODEL

    @pl.when(f == 0)
    def _():
        m = mod_ref[0, 0]
        g1 = m[:, 2 * d:3 * d]
        sh2 = m[:, 3 * d:4 * d]
        sc2 = m[:, 4 * d:5 * d]
        y = jnp.concatenate([ya_ref[0], yb_ref[0], yc_ref[0], yd_ref[0]], axis=-1)
        x1 = x_ref[0] + g1 * _dot(y.astype(BF16), wo_ref[...])
        x1_s[...] = x1
        u_s[...] = (_rms(x1) * g2_ref[...] * (1.0 + sc2) + sh2).astype(BF16)
        acc_s[...] = jnp.zeros_like(acc_s)

    u = u_s[...]
    h = _silu(_dot(u, w1_ref[...])) * _dot(u, w3_ref[...])
    acc_s[...] += _dot(h.astype(BF16), w2_ref[...])

    @pl.when(f == nf - 1)
    def _():
        g2 = mod_ref[0, 0][:, 5 * d:6 * d]
        xo = x1_s[...] + g2 * acc_s[...]
        if final:
            xo = _rms(xo) * gf_ref[...]
        o_ref[0] = xo


def _out_ffn(x, ys, mod4, layer, row0, g2, gf, wo, w1, w3, w2, final):
    bm, lm, d = x.shape
    nf = D_FF // TF
    tok = lambda wd: pl.BlockSpec((1, TM, wd), lambda i, j, f: (i, j, 0))
    return pl.pallas_call(
        functools.partial(_ffn_kernel, nf=nf, final=final),
        out_shape=jax.ShapeDtypeStruct(x.shape, F32),
        grid=(bm, lm // TM, nf),
        in_specs=[
            tok(d), tok(GROUP_W), tok(GROUP_W), tok(GROUP_W), tok(GROUP_W),
            pl.BlockSpec((1, 1, 1, N_MOD * d), lambda i, j, f: (layer, row0 + i, 0, 0)),
            pl.BlockSpec((1, d), lambda i, j, f: (0, 0)),
            pl.BlockSpec((1, d), lambda i, j, f: (0, 0)),
            pl.BlockSpec((d, d), lambda i, j, f: (0, 0)),
            pl.BlockSpec((d, TF), lambda i, j, f: (0, f)),
            pl.BlockSpec((d, TF), lambda i, j, f: (0, f)),
            pl.BlockSpec((TF, d), lambda i, j, f: (f, 0)),
        ],
        out_specs=tok(d),
        scratch_shapes=[pltpu.VMEM((TM, d), F32), pltpu.VMEM((TM, d), BF16), pltpu.VMEM((TM, d), F32)],
        compiler_params=_params(("parallel", "parallel", "arbitrary")),
        name="out_ffn",
    )(x, *ys, mod4, g2, gf, wo, w1, w3, w2)


def _gated_prelude(blk, d, wg_ref, par_ref, tri_ref, q_s, k_s, v_s, la_s, b_s, *, mode):
    w = GROUP_W
    if mode == "gla":
        q = blk[:, 0:w] * (HEAD_D ** -0.5)
        k = blk[:, w:2 * w]
        v = blk[:, 2 * w:3 * w]
        x = _dot(blk[:, 4 * w:4 * w + LR_PAD].astype(BF16), wg_ref[d]) + par_ref[d:d + 1, :]
        la = _log_sigmoid(x) * (1.0 / GLA_GATE_TEMP)
    else:
        q = _silu(blk[:, 0:w])
        f = blk[:, (1 + d) * w:(2 + d) * w]
        v = blk[:, 3 * w:4 * w]
        e = jnp.exp(-jnp.abs(f))
        k = par_ref[0:1, :] * (jnp.where(f >= 0.0, e, 1.0) / (1.0 + e))
        y = par_ref[2:3, :] + (jnp.minimum(f, 0.0) - jnp.log(1.0 + e))
        lb = par_ref[1:2, :]
        la = jnp.maximum(lb, y) + _log1p_exp_neg_abs(lb - y)
    q_s[d] = q
    k_s[d] = k
    v_s[d] = v
    la_s[d] = la * LOG2E
    b_s[d] = _dot_w2(tri_ref[d], la_s[d])


def _gated_subblock(d, i, ones_ref, mbd_ref, q_s, k_s, v_s, b_s, p_s, o_s, st_s):
    w = GROUP_W
    rev = d == 1
    tio = lax.broadcasted_iota(jnp.int32, (SUB, w), 0)
    edge = 0 if rev else SUB - 1
    rows = pl.ds(pl.multiple_of(i * SUB, SUB), SUB)
    bb = b_s[d, rows, :]
    qb = q_s[d, rows, :]
    kb = k_s[d, rows, :]
    vb = v_s[d, rows, :]

    def row(a, s):
        return jnp.broadcast_to(a[s:s + 1, :], (SUB, w))

    for s in range(SUB):
        valid = (tio <= s) if rev else (tio >= s)
        e = jnp.exp2(jnp.where(valid, bb - row(bb, s), -jnp.inf))
        p_s[d, s * SUB:(s + 1) * SUB, :] = (e * qb * row(kb, s)).astype(BF16)
    r = _dot(p_s[d], ones_ref[...])
    od = jnp.zeros((SUB, w), F32)
    for s in range(SUB):
        od = od + r[s * SUB:(s + 1) * SUB, :] * row(vb, s)
    bend = row(bb, edge)
    st = st_s[d]
    qt = (qb * jnp.exp2(bb)).astype(BF16)
    oi = lax.dot_general(qt, st.astype(BF16), (((1,), (1,)), ((), ())), preferred_element_type=F32)
    kt = (kb * jnp.exp2(bend - bb)).astype(BF16)
    kv = lax.dot_general(vb.astype(BF16), kt, (((0,), (0,)), ((), ())), preferred_element_type=F32)
    st_s[d] = st * jnp.exp2(bend[0:1, :]) + kv * mbd_ref[...]
    o_s[d, rows, :] = od + oi


def _stack_heads(x):
    lane_head = lax.broadcasted_iota(jnp.int32, x.shape, 1) // HEAD_D
    return jnp.concatenate([jnp.where(lane_head == h, x, 0.0) for h in range(N_HEADS)], axis=0).astype(BF16)


def _centred_span(b_s, blk):
    span = None
    for d in range(2):
        for g in range(LC // blk):
            r0, rm, r1 = g * blk, g * blk + blk // 2, (g + 1) * blk - 1
            mid = b_s[d, rm:rm + 1, :]
            m = jnp.maximum(jnp.abs(b_s[d, r0:r0 + 1, :] - mid), jnp.abs(b_s[d, r1:r1 + 1, :] - mid))
            span = m if span is None else jnp.maximum(span, m)
    return jnp.max(span)


def _gated_block_bounded(d, g, blk, q_s, k_s, v_s, b_s, o_s, st_s):
    w = GROUP_W
    rev = d == 1
    edge = 0 if rev else blk - 1
    rows = slice(g * blk, (g + 1) * blk)
    bb = b_s[d, rows, :]
    qb = q_s[d, rows, :]
    kb = k_s[d, rows, :]
    cc = bb - jnp.broadcast_to(bb[blk // 2:blk // 2 + 1, :], bb.shape)
    vm = _stack_heads(v_s[d, rows, :])
    km = _stack_heads(kb * jnp.exp2(-cc))
    sc = lax.dot_general((qb * jnp.exp2(cc)).astype(BF16), km, (((1,), (1,)), ((), ())),
                         preferred_element_type=F32)
    t_i = lax.broadcasted_iota(jnp.int32, sc.shape, 0)
    s_i = lax.broadcasted_iota(jnp.int32, sc.shape, 1) % blk
    od = _dot(jnp.where((t_i <= s_i) if rev else (t_i >= s_i), sc, 0.0).astype(BF16), vm)
    bend = jnp.broadcast_to(bb[edge:edge + 1, :], (blk, w))
    st = st_s[d]
    oi = lax.dot_general((qb * jnp.exp2(bb)).astype(BF16), st.astype(BF16), (((1,), (1,)), ((), ())),
                         preferred_element_type=F32)
    ktm = _stack_heads(kb * jnp.exp2(bend - bb))
    kv = lax.dot_general(vm, ktm, (((0,), (0,)), ((), ())), preferred_element_type=F32)
    st_s[d] = st * jnp.exp2(bend[0:1, :]) + kv
    o_s[d, rows, :] = od + oi


def _gated_kernel(pf_ref, pb_ref, wg_ref, par_ref, s0_ref, tri_ref, ones_ref, mbd_ref,
                  y_ref, st_ref, q_s, k_s, v_s, la_s, b_s, p_s, o_s, st_s, *, mode, nc):
    j = pl.program_id(1)
    w = GROUP_W

    @pl.when(j == 0)
    def _():
        st_s[0] = s0_ref[0, 0]
        st_s[1] = s0_ref[0, 1]

    _gated_prelude(pf_ref[0], 0, wg_ref, par_ref, tri_ref.at[0], q_s, k_s, v_s, la_s, b_s, mode=mode)
    _gated_prelude(pb_ref[0], 1, wg_ref, par_ref, tri_ref.at[0], q_s, k_s, v_s, la_s, b_s, mode=mode)

    def run_bounded(blk):
        for g in range(LC // blk):
            _gated_block_bounded(0, g, blk, q_s, k_s, v_s, b_s, o_s, st_s)
            _gated_block_bounded(1, LC // blk - 1 - g, blk, q_s, k_s, v_s, b_s, o_s, st_s)

    def recumulate(level):
        for d in range(2):
            b_s[d] = _dot_w2(tri_ref[level, d], la_s[d])

    def run_exact():
        def body(it, carry):
            scratch = (q_s, k_s, v_s, b_s, p_s, o_s, st_s)
            _gated_subblock(0, it, ones_ref, mbd_ref, *scratch)
            _gated_subblock(1, N_SUB - 1 - it, ones_ref, mbd_ref, *scratch)
            return carry

        lax.fori_loop(0, N_SUB, body, 0)

    wide = _centred_span(b_s, BOUNDED_BLKS[0]) < FAST_LOG2_RANGE

    @pl.when(wide)
    def _():
        run_bounded(BOUNDED_BLKS[0])

    @pl.when(jnp.logical_not(wide))
    def _():
        recumulate(1)
        narrow = _centred_span(b_s, BOUNDED_BLKS[1]) < FAST_LOG2_RANGE

        @pl.when(narrow)
        def _():
            run_bounded(BOUNDED_BLKS[1])

        @pl.when(jnp.logical_not(narrow))
        def _():
            recumulate(2)
            run_exact()

    @pl.when(j == nc - 1)
    def _():
        for d in range(2):
            for h in range(N_HEADS):
                blk = pl.ds(h * HEAD_D, HEAD_D)
                st_ref[0, d, h] = st_s[d, blk, blk]

    gcol = 3 * w if mode == "gla" else 4 * w

    def finish(o, p_ref):
        ms = _dot_x2(o * o, ones_ref[...]) * (1.0 / HEAD_D)
        return o * lax.rsqrt(ms + EPS) * par_ref[3:4, :] * _silu(p_ref[0][:, gcol:gcol + w])

    rows_f = pl.ds(pl.multiple_of(j * LC, LC), LC)
    rows_b = pl.ds(pl.multiple_of((nc - 1 - j) * LC, LC), LC)
    if nc == 1:
        y_ref[0] = finish(o_s[0] + o_s[1], pf_ref)
    else:
        @pl.when(j < nc // 2)
        def _():
            y_ref[0, rows_f, :] = o_s[0]
            y_ref[0, rows_b, :] = o_s[1]

        @pl.when(j >= nc // 2)
        def _():
            y_ref[0, rows_f, :] = finish(y_ref[0, rows_f, :] + o_s[0], pf_ref)
            y_ref[0, rows_b, :] = finish(y_ref[0, rows_b, :] + o_s[1], pb_ref)


def _gated_consts():
    r = jnp.arange(LC)

    def tri(block):
        same = (r[:, None] // block) == (r[None, :] // block)
        lower = (same & (r[None, :] <= r[:, None])).astype(BF16)
        upper = (same & (r[None, :] >= r[:, None])).astype(BF16)
        return jnp.stack([lower, upper])

    head = (r[:, None] // HEAD_D) == (r[None, :] // HEAD_D)
    tris = jnp.stack([tri(blk) for blk in BOUNDED_BLKS + (SUB,)])
    return tris, head.astype(BF16), head.astype(F32)


def _gated_mixer(p, wg, par, s0t, consts, mode):
    b, l, width = p.shape
    nc = l // LC
    assert nc == 1 or nc % 2 == 0
    tri, ones, mbd = consts
    w = GROUP_W
    const2 = lambda shape: pl.BlockSpec(shape, lambda i, j: (0,) * len(shape))
    return pl.pallas_call(
        functools.partial(_gated_kernel, mode=mode, nc=nc),
        out_shape=[jax.ShapeDtypeStruct((b, l, w), F32),
                   jax.ShapeDtypeStruct((b, 2, N_HEADS, HEAD_D, HEAD_D), F32)],
        grid=(b, nc),
        in_specs=[
            pl.BlockSpec((1, LC, width), lambda i, j: (i, j, 0)),
            pl.BlockSpec((1, LC, width), lambda i, j: (i, nc - 1 - j, 0)),
            const2(wg.shape), const2(par.shape),
            pl.BlockSpec((1, 2, w, w), lambda i, j: (i, 0, 0, 0)),
            const2(tri.shape), const2(ones.shape), const2(mbd.shape),
        ],
        out_specs=[pl.BlockSpec((1, l, w), lambda i, j: (i, 0, 0)),
                   pl.BlockSpec((1, 2, N_HEADS, HEAD_D, HEAD_D), lambda i, j: (i, 0, 0, 0, 0))],
        scratch_shapes=[pltpu.VMEM((2, LC, w), F32), pltpu.VMEM((2, LC, w), F32), pltpu.VMEM((2, LC, w), F32),
                        pltpu.VMEM((2, LC, w), F32), pltpu.VMEM((2, LC, w), F32),
                        pltpu.VMEM((2, SUB * SUB, w), BF16),
                        pltpu.VMEM((2, LC, w), F32), pltpu.VMEM((2, w, w), F32)],
        compiler_params=_params(("parallel", "arbitrary")),
        name="gated_" + mode,
    )(p, p, wg, par, s0t, tri, ones, mbd)


def _state_to_blockdiag_t(s):
    b = s.shape[0]
    st = jnp.swapaxes(s, -1, -2)
    eye = jnp.eye(N_HEADS, dtype=s.dtype)
    full = st[:, :, :, :, None, :] * eye[None, None, :, None, :, None]
    return full.reshape(b, 2, GROUP_W, GROUP_W)


def _blockdiag_t_to_state(st):
    return jnp.swapaxes(st, -1, -2)


def _rglru_kernel(x_ref, g_ref, cw_ref, cb_ref, w_ref, bias_ref, nsp_ref, h0_ref, y_ref, hT_ref,
                  xc_s, a_s, u_s, h_s, *, l):
    c = RG_HALF
    x = x_ref[0]
    row = lax.broadcasted_iota(jnp.int32, (l, c), 0)
    xc = x * cw_ref[2:3, :] + cb_ref[...]
    xc = xc + jnp.where(row >= 2, pltpu.roll(x, 2, 0), 0.0) * cw_ref[0:1, :]
    xc = xc + jnp.where(row >= 1, pltpu.roll(x, 1, 0), 0.0) * cw_ref[1:2, :]
    xc = xc + jnp.where(row <= l - 2, pltpu.roll(x, l - 1, 0), 0.0) * cw_ref[3:4, :]
    xc_s[...] = xc

    nslab = l // RG_ROWS
    grp = (RG_ROWS // SUBLANE, SUBLANE, c)
    sub = lax.broadcasted_iota(jnp.int32, grp, 1)

    def slab(n, carry):
        rows = pl.ds(pl.multiple_of(n * RG_ROWS, RG_ROWS), RG_ROWS)
        xs = xc_s[rows, :]
        gates = _sigmoid(_dot(xs.astype(BF16), w_ref[...]) + bias_ref[...])
        for d in range(2):
            r = gates[:, (2 * d) * c:(2 * d + 1) * c]
            i = gates[:, (2 * d + 1) * c:(2 * d + 2) * c]
            log_a = r * nsp_ref[d:d + 1, :]
            a = jnp.exp(log_a).reshape(grp)
            u = (jnp.sqrt(1.0 - jnp.exp(2.0 * log_a)) * (i * xs)).reshape(grp)
            for sft in (1, 2, 4):
                if d == 0:
                    ok = sub >= sft
                    a_n, u_n = pltpu.roll(a, sft, 1), pltpu.roll(u, sft, 1)
                else:
                    ok = sub <= SUBLANE - 1 - sft
                    a_n, u_n = pltpu.roll(a, SUBLANE - sft, 1), pltpu.roll(u, SUBLANE - sft, 1)
                u = jnp.where(ok, a * u_n + u, u)
                a = jnp.where(ok, a * a_n, a)
            a_s[d, rows, :] = a.reshape(RG_ROWS, c)
            u_s[d, rows, :] = u.reshape(RG_ROWS, c)
        return carry

    lax.fori_loop(0, nslab, slab, 0)

    ngrp = l // SUBLANE

    def fwd(n, h):
        rows = pl.ds(pl.multiple_of(n * SUBLANE, SUBLANE), SUBLANE)
        hh = a_s[0, rows, :] * h + u_s[0, rows, :]
        h_s[rows, :] = hh
        return jnp.broadcast_to(hh[SUBLANE - 1:SUBLANE, :], (SUBLANE, c))

    hf = lax.fori_loop(0, ngrp, fwd, jnp.broadcast_to(h0_ref[0, 0:1, :], (SUBLANE, c)))

    def bwd(n, h):
        rows = pl.ds(pl.multiple_of((ngrp - 1 - n) * SUBLANE, SUBLANE), SUBLANE)
        hh = a_s[1, rows, :] * h + u_s[1, rows, :]
        h_s[rows, :] = h_s[rows, :] + hh
        return jnp.broadcast_to(hh[0:1, :], (SUBLANE, c))

    hb = lax.fori_loop(0, ngrp, bwd, jnp.broadcast_to(h0_ref[0, 1:2, :], (SUBLANE, c)))
    hT_ref[0, 0:1, :] = hf[0:1, :]
    hT_ref[0, 1:2, :] = hb[0:1, :]

    g = g_ref[0]
    gelu = 0.5 * g * (1.0 + jnp.tanh(math.sqrt(2.0 / math.pi) * (g + 0.044715 * (g * g * g))))
    y_ref[0] = h_s[...] * gelu


def _rglru(p, cw, cb, wbd, bias, nsp, h0):
    b, l, _ = p.shape
    c = RG_HALF
    half = lambda shape: pl.BlockSpec(shape, lambda i, j: (0,) * (len(shape) - 1) + (j,))
    return pl.pallas_call(
        functools.partial(_rglru_kernel, l=l),
        out_shape=[jax.ShapeDtypeStruct((b, l, GROUP_W), F32), jax.ShapeDtypeStruct((b, 2, GROUP_W), F32)],
        grid=(b, 2),
        in_specs=[
            pl.BlockSpec((1, l, c), lambda i, j: (i, 0, j)),
            pl.BlockSpec((1, l, c), lambda i, j: (i, 0, 2 + j)),
            half((RG_CONV_W, c)), half((1, c)),
            pl.BlockSpec((None, c, 4 * c), lambda i, j: (j, 0, 0)),
            pl.BlockSpec((None, 1, 4 * c), lambda i, j: (j, 0, 0)),
            half((2, c)),
            pl.BlockSpec((1, 2, c), lambda i, j: (i, 0, j)),
        ],
        out_specs=[pl.BlockSpec((1, l, c), lambda i, j: (i, 0, j)),
                   pl.BlockSpec((1, 2, c), lambda i, j: (i, 0, j))],
        scratch_shapes=[pltpu.VMEM((l, c), F32), pltpu.VMEM((2, l, c), F32), pltpu.VMEM((2, l, c), F32),
                        pltpu.VMEM((l, c), F32)],
        compiler_params=_params(("parallel", "parallel")),
        name="rglru",
    )(p, p, cw, cb, wbd, bias, nsp, h0)


def _short_conv3(x, w_ref, b_ref):
    l = x.shape[0]
    row = lax.broadcasted_iota(jnp.int32, x.shape, 0)
    o = x * w_ref[1:2, :] + b_ref[...]
    o = o + jnp.where(row >= 1, pltpu.roll(x, 1, 0), 0.0) * w_ref[0:1, :]
    return o + jnp.where(row <= l - 2, pltpu.roll(x, l - 1, 0), 0.0) * w_ref[2:3, :]


def _hy_filter_kernel(pe_ref, w1_ref, b1_ref, w2_ref, b2_ref, w3_ref, dec_ref, o_ref):
    pe = pe_ref[...]
    h = jnp.sin(_dot3(pe, w1_ref[...]) + b1_ref[...])
    h = jnp.sin(_dot3(h, w2_ref[...]) + b2_ref[...])
    h = _dot3(h, w3_ref[...])
    dist = pe[:, LANE - 1:LANE]
    h = h * jnp.exp(-dist * dec_ref[...])
    o_ref[...] = h / jnp.sum(jnp.abs(h), axis=0, keepdims=True)


def _hy_filters(pe, w1, b1, w2, b2, w3, decay):
    l = pe.shape[0]
    n = w3.shape[1]
    args = (pe, w1, b1, w2, b2, w3, decay)
    return pl.pallas_call(
        _hy_filter_kernel,
        out_shape=jax.ShapeDtypeStruct((l, n), F32),
        grid=(1,),
        in_specs=[pl.BlockSpec(a.shape, lambda i: (0, 0)) for a in args],
        out_specs=pl.BlockSpec((l, n), lambda i: (0, 0)),
        compiler_params=_params(("arbitrary",)),
        name="hy_filters",
    )(*args)


def _hy_conv_kernel(z_ref, x_ref, h_ref, skip_ref, wz_ref, bz_ref, wx_ref, bx_ref,
                    cf_ref, sf_ref, ci_ref, si_ref, o_ref, zb_s, *, bb, nk, z_raw):
    kt = pl.program_id(1)
    c = GROUP_W

    def z_of(b):
        return _short_conv3(z_ref[b], wz_ref, bz_ref) if z_raw else z_ref[b]

    @pl.when(kt == 0)
    def _():
        o_ref[...] = jnp.zeros_like(o_ref)
        for b in range(bb):
            zb_s[b] = z_of(b).astype(BF16)

    zcat = jnp.concatenate([zb_s[b] for b in range(bb)] + [h_ref[...].astype(BF16)], axis=-1)
    xc = _dot(cf_ref[...], zcat)
    xs = _dot(sf_ref[...], zcat)
    hc = xc[:, bb * c:]
    hs = xs[:, bb * c:]
    for b in range(bb):
        zc = xc[:, b * c:(b + 1) * c]
        zsn = xs[:, b * c:(b + 1) * c]
        yc = (zc * hc - zsn * hs).astype(BF16)
        ys = (zc * hs + zsn * hc).astype(BF16)
        o_ref[b] += _dot(ci_ref[...], yc) + _dot(si_ref[...], ys)

    @pl.when(kt == nk - 1)
    def _():
        for b in range(bb):
            gate = _short_conv3(x_ref[b], wx_ref, bx_ref)
            o_ref[b] = gate * (o_ref[b] + skip_ref[...] * z_of(b))


def _hy_long_conv(pc, zprev, z_col, gate_col, cw, cb, h, h_col, skip, skip_col, tables, bb):
    b, l, _ = pc.shape
    c = GROUP_W
    cf, sf, ci, si = tables
    nk = cf.shape[0] // FREQ_TILE
    z_raw = zprev is None
    z_arr = pc if z_raw else zprev
    return pl.pallas_call(
        functools.partial(_hy_conv_kernel, bb=bb, nk=nk, z_raw=z_raw),
        out_shape=jax.ShapeDtypeStruct((b, l, c), F32),
        grid=(b // bb, nk),
        in_specs=[
            pl.BlockSpec((bb, l, c), lambda g, k: (g, 0, z_col)),
            pl.BlockSpec((bb, l, c), lambda g, k: (g, 0, gate_col)),
            pl.BlockSpec((l, c), lambda g, k: (0, h_col)),
            pl.BlockSpec((1, c), lambda g, k: (0, skip_col)),
            pl.BlockSpec((3, c), lambda g, k: (0, z_col)),
            pl.BlockSpec((1, c), lambda g, k: (0, z_col)),
            pl.BlockSpec((3, c), lambda g, k: (0, gate_col)),
            pl.BlockSpec((1, c), lambda g, k: (0, gate_col)),
            pl.BlockSpec((FREQ_TILE, l), lambda g, k: (k, 0)),
            pl.BlockSpec((FREQ_TILE, l), lambda g, k: (k, 0)),
            pl.BlockSpec((l, FREQ_TILE), lambda g, k: (0, k)),
            pl.BlockSpec((l, FREQ_TILE), lambda g, k: (0, k)),
        ],
        out_specs=pl.BlockSpec((bb, l, c), lambda g, k: (g, 0, 0)),
        scratch_shapes=[pltpu.VMEM((bb, l, c), BF16)],
        compiler_params=_params(("parallel", "arbitrary")),
        name="hy_long_conv",
    )(z_arr, pc, h, skip, cw, cb, cw, cb, cf, sf, ci, si)


def _dft_tables(l):
    n = 3 * l // 2
    nf = n // 2 + 1
    nfp = -(-nf // FREQ_TILE) * FREQ_TILE
    k = jnp.arange(nfp, dtype=jnp.int32)
    ta = jnp.arange(l // TWID, dtype=jnp.int32) * TWID
    tb = jnp.arange(TWID, dtype=jnp.int32)
    live = (k < nf)

    def cos_sin(m):
        ang = (m % n).astype(F32) * (2.0 * math.pi / n)
        return jnp.cos(ang), jnp.sin(ang)

    ca, sa = cos_sin(k[:, None] * ta[None, :])
    cb, sb = cos_sin(k[:, None] * tb[None, :])
    ca, sa = (jnp.where(live[:, None], x, 0.0)[:, :, None] for x in (ca, sa))
    cb, sb = cb[:, None, :], sb[:, None, :]
    cf = (ca * cb - sa * sb).reshape(nfp, l).astype(BF16)
    sf = (sa * cb + ca * sb).reshape(nfp, l).astype(BF16)
    wk = jnp.where((k == 0) | (k == n // 2), 1.0, 2.0) / n
    wk = jnp.where(live, wk, 0.0).astype(F32)
    ca, sa = cos_sin((ta + l // 2)[:, None] * k[None, :])
    cb, sb = cos_sin(tb[:, None] * k[None, :])
    ca, sa = ((wk[None, :] * x)[:, None, :] for x in (ca, sa))
    cb, sb = cb[None, :, :], sb[None, :, :]
    ci = (ca * cb - sa * sb).reshape(l, nfp).astype(BF16)
    si = (sa * cb + ca * sb).reshape(l, nfp).astype(BF16)
    return cf, sf, ci, si


def _hy_pos_features(l):
    pos = jnp.arange(l, dtype=F32)
    t = pos / (l - 1)
    ang = (2.0 * jnp.pi * pos / l)[:, None] * jnp.linspace(1e-4, HY_POS_BANDS - 1, HY_POS_BANDS, dtype=F32)[None, :]
    half = l // 2
    dist = jnp.abs(pos - half) / half
    pe = jnp.concatenate([t[:, None], jnp.cos(ang), -jnp.sin(ang)], axis=-1)
    pad = jnp.zeros((l, LANE - 1 - pe.shape[1]), F32)
    return jnp.concatenate([pe, pad, dist[:, None]], axis=-1)


def _grid_position_encoding(n_pos, dim):
    quarter = dim // 4
    omega = 1.0 / (POS_BASE ** (jnp.arange(quarter, dtype=F32) / quarter))
    ang = jnp.arange(n_pos, dtype=F32)[:, None] * omega[None, :]
    return jnp.concatenate([jnp.sin(ang), jnp.cos(ang)], axis=-1)


def _blockdiag2(a, b):
    z = jnp.zeros_like(a)
    return jnp.concatenate([jnp.concatenate([a, z], axis=1), jnp.concatenate([z, b], axis=1)], axis=0)


def _layer_params(l, hg_lb, norm1_g, norm2_g, w_in, w_out, gla_w_gate, gla_b_gate, gla_norm_g,
                  rg_conv_w, rg_conv_b, rg_w_a, rg_b_a, rg_w_x, rg_b_x, rg_lambda,
                  hy_conv_w, hy_conv_b, hy_w1, hy_b1, hy_w2, hy_b2, hy_w3, hy_decay, hy_skip,
                  hg_norm_g, ffn_w1, ffn_w3, ffn_w2):
    d, w = D_MODEL, GROUP_W
    n_gla = 4 * w + GLA_LOWRANK
    wi = w_in[l]
    w_proj = jnp.concatenate([wi[:, :n_gla], jnp.zeros((d, LR_PAD - GLA_LOWRANK), F32), wi[:, n_gla:]], axis=1)
    wg = jnp.concatenate([gla_w_gate[l], jnp.zeros((2, LR_PAD - GLA_LOWRANK, w), F32)], axis=1)
    zrow = jnp.zeros((w,), F32)
    par_gla = jnp.stack([gla_b_gate[l, 0], gla_b_gate[l, 1], zrow, gla_norm_g[l]] + [zrow] * 4)
    lb = hg_lb[l]
    par_hg = jnp.stack([1.0 - lb, jnp.log(lb), jnp.log1p(-lb), hg_norm_g[l]] + [zrow] * 4)
    wa, wx = rg_w_a[l], rg_w_x[l]
    rg_w = jnp.stack([
        jnp.concatenate([_blockdiag2(m[dd, 2 * j], m[dd, 2 * j + 1]) for dd in range(2) for m in (wa, wx)], axis=1)
        for j in range(2)])
    ba, bx = rg_b_a[l], rg_b_x[l]
    rg_bias = jnp.stack([
        jnp.concatenate([v[dd, j * RG_HALF:(j + 1) * RG_HALF] for dd in range(2) for v in (ba, bx)])[None, :]
        for j in range(2)])
    nsp = -RG_C * jax.nn.softplus(-rg_lambda[l])
    w1p = jnp.concatenate([hy_w1[l], jnp.zeros((LANE - hy_w1.shape[1], HY_FFN_W), F32)], axis=0)
    return dict(
        norm1=norm1_g[l][None, :], norm2=norm2_g[l][None, :],
        w_proj=w_proj.astype(BF16), w_out=w_out[l].astype(BF16),
        wg=wg.astype(BF16), par_gla=par_gla, par_hg=par_hg,
        rg_cw=rg_conv_w[l], rg_cb=rg_conv_b[l][None, :], rg_w=rg_w.astype(BF16), rg_bias=rg_bias, rg_nsp=nsp,
        hy_cw=hy_conv_w[l], hy_cb=hy_conv_b[l][None, :], hy_w1=w1p, hy_b1=hy_b1[l][None, :],
        hy_w2=hy_w2[l], hy_b2=hy_b2[l][None, :], hy_w3=hy_w3[l], hy_decay=hy_decay[l][None, :],
        hy_skip=hy_skip[l][None, :],
        w1=ffn_w1[l].astype(BF16), w3=ffn_w3[l].astype(BF16), w2=ffn_w2[l].astype(BF16))


def _trunk_layer(x, p, mod4, layer, row0, seq_shape, s_gla, s_rg, s_hg, stream_consts, final, final_g):
    bm, lm, d = x.shape
    b, l = seq_shape
    gated_consts, pe, tables, bb = stream_consts
    pa, pb, pc, pd = _norm_proj(x, mod4, layer, row0, p["norm1"], p["w_proj"])
    pa, pb, pc, pd = (t.reshape(b, l, t.shape[-1]) for t in (pa, pb, pc, pd))

    ya, st_a = _gated_mixer(pa, p["wg"], p["par_gla"], _state_to_blockdiag_t(s_gla), gated_consts, "gla")
    yd, st_d = _gated_mixer(pd, p["wg"], p["par_hg"], _state_to_blockdiag_t(s_hg), gated_consts, "hg")
    yb, st_b = _rglru(pb, p["rg_cw"], p["rg_cb"], p["rg_w"], p["rg_bias"], p["rg_nsp"], s_rg)

    filt = _hy_filters(pe, p["hy_w1"], p["hy_b1"], p["hy_w2"], p["hy_b2"], p["hy_w3"], p["hy_decay"])
    z1 = _hy_long_conv(pc, None, 0, 1, p["hy_cw"], p["hy_cb"], filt, 0, p["hy_skip"], 0, tables, bb)
    yc = _hy_long_conv(pc, z1, 0, 2, p["hy_cw"], p["hy_cb"], filt, 1, p["hy_skip"], 1, tables, bb)

    ys = [t.reshape(bm, lm, GROUP_W) for t in (ya, yb, yc, yd)]
    x = _out_ffn(x, ys, mod4, layer, row0, p["norm2"], final_g, p["w_out"], p["w1"], p["w3"], p["w2"], final)
    return x, (_blockdiag_t_to_state(st_a), st_b, _blockdiag_t_to_state(st_d))


def kernel(x_prompt, x_sample, state_gla, state_rglru, state_hgrn, c, c_ctx, norm1_g, norm2_g, final_norm_g, w_mod, b_mod, w_in, w_out, gla_w_gate, gla_b_gate, gla_norm_g, rg_conv_w, rg_conv_b, rg_w_a, rg_b_a, rg_w_x, rg_b_x, rg_lambda, hy_conv_w, hy_conv_b, hy_w1, hy_b1, hy_w2, hy_b2, hy_w3, hy_decay, hy_skip, hg_lower, hg_norm_g, ffn_w1, ffn_w3, ffn_w2):
    depth = w_in.shape[0]
    nb, seq, d = x_prompt.shape
    db, dseq, _ = x_sample.shape

    hg_lb = jnp.cumsum(jax.nn.softmax(hg_lower.astype(F32), axis=0), axis=0)
    hg_lb = hg_lb - hg_lb[0:1]

    cvec = jnp.concatenate([c_ctx[None, :], c, jnp.zeros((SUBLANE - 1 - db, d), F32)], axis=0)
    mod4 = _modulation(cvec, w_mod, b_mod).reshape(depth, SUBLANE, 1, N_MOD * d)

    gated_consts = _gated_consts()
    consts_p = (gated_consts, _hy_pos_features(seq), _dft_tables(seq), 8)
    consts_s = (gated_consts, _hy_pos_features(dseq), _dft_tables(dseq), 1)

    xp = x_prompt.reshape(1, nb * seq, d)
    xs = _add_pos(x_sample, _grid_position_encoding(max(dseq // GRID_W, GRID_W), d))
    zero_gla = jnp.zeros((nb, 2, N_HEADS, HEAD_D, HEAD_D), F32)
    zero_rg = jnp.zeros((nb, 2, GROUP_W), F32)
    final_g = final_norm_g[None, :]

    gla_states, rg_states, hg_states = [], [], []
    for l in range(depth):
        p = _layer_params(l, hg_lb, norm1_g, norm2_g, w_in, w_out, gla_w_gate, gla_b_gate, gla_norm_g,
                          rg_conv_w, rg_conv_b, rg_w_a, rg_b_a, rg_w_x, rg_b_x, rg_lambda,
                          hy_conv_w, hy_conv_b, hy_w1, hy_b1, hy_w2, hy_b2, hy_w3, hy_decay, hy_skip,
                          hg_norm_g, ffn_w1, ffn_w3, ffn_w2)
        final = l == depth - 1
        xp, (sg, sr, sh) = _trunk_layer(xp, p, mod4, l, 0, (nb, seq), zero_gla, zero_rg, zero_gla,
                                        consts_p, final, final_g)
        xs, _ = _trunk_layer(xs, p, mod4, l, 1, (db, dseq), state_gla[:, l], state_rglru[:, l],
                             state_hgrn[:, l], consts_s, final, final_g)
        gla_states.append(sg)
        rg_states.append(sr)
        hg_states.append(sh)

    return (xp.reshape(nb, seq, d), xs,
            jnp.stack(gla_states, axis=1), jnp.stack(rg_states, axis=1), jnp.stack(hg_states, axis=1))
```

```python
import functools
import math

import jax
import jax.numpy as jnp
from jax import lax
from jax.experimental import pallas as pl
from jax.experimental.pallas import tpu as pltpu

F32 = jnp.float32
BF16 = jnp.bfloat16

D_MODEL = 1024
N_MOD = 6
GROUP_W = 256
N_HEADS = 4
HEAD_D = GROUP_W // N_HEADS
GLA_LOWRANK = 16
GLA_GATE_TEMP = 16.0
RG_C = 8.0
RG_CONV_W = 4
HY_POS_BANDS = 16
HY_FFN_W = 64
D_FF = 2816
EPS = 1e-6
GRID_W = 64
POS_BASE = 10000.0

LANE = 128
SUBLANE = 8
VMEM_LIMIT = 56 * 1024 * 1024

LOG2E = 1.0 / math.log(2.0)
FAST_LOG2_RANGE = 96.0
SUB = 16
LC = 256
N_SUB = LC // SUB
BOUNDED_BLKS = (64, 32)
LR_PAD = LANE
W_GLA = 4 * GROUP_W + LR_PAD
W_RG = 2 * GROUP_W
W_HY = 3 * GROUP_W
W_HG = 5 * GROUP_W
W_PROJ = W_GLA + W_RG + W_HY + W_HG
TM = 512
TF = D_FF // 2
TN_MOD = 512
RG_HALF = GROUP_W // 2
RG_ROWS = 256
FREQ_TILE = 256
TWID = 64


def _dot(a, b):
    return jnp.dot(a, b, preferred_element_type=F32)


def _split(x):
    hi = x.astype(BF16)
    lo = (x - hi.astype(F32)).astype(BF16)
    return hi, lo


def _dot_x2(x, w):
    hi, lo = _split(x)
    return _dot(hi, w) + _dot(lo, w)


def _dot_w2(w, x):
    hi, lo = _split(x)
    return _dot(w, hi) + _dot(w, lo)


def _dot3(a, b):
    ah, al = _split(a)
    bh, bl = _split(b)
    return _dot(ah, bh) + _dot(al, bh) + _dot(ah, bl)


def _sigmoid(x):
    return 0.5 * jnp.tanh(0.5 * x) + 0.5


def _silu(x):
    return x * _sigmoid(x)


def _log1p_exp_neg_abs(x):
    return jnp.log(1.0 + jnp.exp(-jnp.abs(x)))


def _log_sigmoid(x):
    return jnp.minimum(x, 0.0) - _log1p_exp_neg_abs(x)


def _rms(x):
    return x * lax.rsqrt(jnp.mean(x * x, axis=-1, keepdims=True) + EPS)


def _params(sem, vmem=VMEM_LIMIT):
    return pltpu.CompilerParams(dimension_semantics=sem, vmem_limit_bytes=vmem)


def _mod_kernel(c_ref, w_ref, b_ref, o_ref):
    c = c_ref[...]
    o_ref[0] = _dot3(_silu(c), w_ref[0]) + b_ref[0]


def _modulation(cvec, w_mod, b_mod):
    depth = w_mod.shape[0]
    n = N_MOD * D_MODEL
    return pl.pallas_call(
        _mod_kernel,
        out_shape=jax.ShapeDtypeStruct((depth, SUBLANE, n), F32),
        grid=(depth, n // TN_MOD),
        in_specs=[
            pl.BlockSpec((SUBLANE, D_MODEL), lambda l, j: (0, 0)),
            pl.BlockSpec((1, D_MODEL, TN_MOD), lambda l, j: (l, 0, j)),
            pl.BlockSpec((1, 1, TN_MOD), lambda l, j: (l, 0, j)),
        ],
        out_specs=pl.BlockSpec((1, SUBLANE, TN_MOD), lambda l, j: (l, 0, j)),
        compiler_params=_params(("parallel", "parallel")),
        name="modulation",
    )(cvec, w_mod, b_mod.reshape(depth, 1, n))


def _add_kernel(x_ref, e_ref, o_ref):
    j = pl.program_id(1)
    half = D_MODEL // 2
    enc = e_ref[0:GRID_W, :]
    for r in range(TM // GRID_W):
        rows = slice(r * GRID_W, (r + 1) * GRID_W)
        enc_row = e_ref[pl.ds(j * (TM // GRID_W) + r, 1), :]
        o_ref[0, rows, 0:half] = x_ref[0, rows, 0:half] + enc_row
        o_ref[0, rows, half:] = x_ref[0, rows, half:] + enc


def _add_pos(x, enc):
    b, l, d = x.shape
    assert l // GRID_W <= enc.shape[0]
    return pl.pallas_call(
        _add_kernel,
        out_shape=jax.ShapeDtypeStruct(x.shape, F32),
        grid=(b, l // TM),
        in_specs=[pl.BlockSpec((1, TM, d), lambda i, j: (i, j, 0)),
                  pl.BlockSpec(enc.shape, lambda i, j: (0, 0))],
        out_specs=pl.BlockSpec((1, TM, d), lambda i, j: (i, j, 0)),
        compiler_params=_params(("parallel", "parallel")),
        name="add_pos",
    )(x, enc)


def _proj_kernel(x_ref, mod_ref, g_ref, w_ref, oa_ref, ob_ref, oc_ref, od_ref):
    x = x_ref[0]
    m = mod_ref[0, 0]
    sh = m[:, 0:D_MODEL]
    sc = m[:, D_MODEL:2 * D_MODEL]
    u = _rms(x) * g_ref[...] * (1.0 + sc) + sh
    p = _dot(u.astype(BF16), w_ref[...])
    oa_ref[0] = p[:, 0:W_GLA]
    ob_ref[0] = p[:, W_GLA:W_GLA + W_RG]
    oc_ref[0] = p[:, W_GLA + W_RG:W_GLA + W_RG + W_HY]
    od_ref[0] = p[:, W_GLA + W_RG + W_HY:W_PROJ]


def _norm_proj(x, mod4, layer, row0, gain, w):
    bm, lm, d = x.shape
    widths = (W_GLA, W_RG, W_HY, W_HG)
    return pl.pallas_call(
        _proj_kernel,
        out_shape=[jax.ShapeDtypeStruct((bm, lm, wd), F32) for wd in widths],
        grid=(bm, lm // TM),
        in_specs=[
            pl.BlockSpec((1, TM, d), lambda i, j: (i, j, 0)),
            pl.BlockSpec((1, 1, 1, N_MOD * d), lambda i, j: (layer, row0 + i, 0, 0)),
            pl.BlockSpec((1, d), lambda i, j: (0, 0)),
            pl.BlockSpec((d, W_PROJ), lambda i, j: (0, 0)),
        ],
        out_specs=[pl.BlockSpec((1, TM, wd), lambda i, j: (i, j, 0)) for wd in widths],
        compiler_params=_params(("parallel", "parallel")),
        name="norm_proj",
    )(x, mod4, gain, w)


def _ffn_kernel(x_ref, ya_ref, yb_ref, yc_ref, yd_ref, mod_ref, g2_ref, gf_ref,
                wo_ref, w1_ref, w3_ref, w2_ref, o_ref, x1_s, u_s, acc_s, *, nf, final):
    f = pl.program_id(2)
    d = D_MODEL

    @pl.when(f == 0)
    def _():
        m = mod_ref[0, 0]
        g1 = m[:, 2 * d:3 * d]
        sh2 = m[:, 3 * d:4 * d]
        sc2 = m[:, 4 * d:5 * d]
        y = jnp.concatenate([ya_ref[0], yb_ref[0], yc_ref[0], yd_ref[0]], axis=-1)
        x1 = x_ref[0] + g1 * _dot(y.astype(BF16), wo_ref[...])
        x1_s[...] = x1
        u_s[...] = (_rms(x1) * g2_ref[...] * (1.0 + sc2) + sh2).astype(BF16)
        acc_s[...] = jnp.zeros_like(acc_s)

    u = u_s[...]
    h = _silu(_dot(u, w1_ref[...])) * _dot(u, w3_ref[...])
    acc_s[...] += _dot(h.astype(BF16), w2_ref[...])

    @pl.when(f == nf - 1)
    def _():
        g2 = mod_ref[0, 0][:, 5 * d:6 * d]
        xo = x1_s[...] + g2 * acc_s[...]
        if final:
            xo = _rms(xo) * gf_ref[...]
        o_ref[0] = xo


def _out_ffn(x, ys, mod4, layer, row0, g2, gf, wo, w1, w3, w2, final):
    bm, lm, d = x.shape
    nf = D_FF // TF
    tok = lambda wd: pl.BlockSpec((1, TM, wd), lambda i, j, f: (i, j, 0))
    return pl.pallas_call(
        functools.partial(_ffn_kernel, nf=nf, final=final),
        out_shape=jax.ShapeDtypeStruct(x.shape, F32),
        grid=(bm, lm // TM, nf),
        in_specs=[
            tok(d), tok(GROUP_W), tok(GROUP_W), tok(GROUP_W), tok(GROUP_W),
            pl.BlockSpec((1, 1, 1, N_MOD * d), lambda i, j, f: (layer, row0 + i, 0, 0)),
            pl.BlockSpec((1, d), lambda i, j, f: (0, 0)),
            pl.BlockSpec((1, d), lambda i, j, f: (0, 0)),
            pl.BlockSpec((d, d), lambda i, j, f: (0, 0)),
            pl.BlockSpec((d, TF), lambda i, j, f: (0, f)),
            pl.BlockSpec((d, TF), lambda i, j, f: (0, f)),
            pl.BlockSpec((TF, d), lambda i, j, f: (f, 0)),
        ],
        out_specs=tok(d),
        scratch_shapes=[pltpu.VMEM((TM, d), F32), pltpu.VMEM((TM, d), BF16), pltpu.VMEM((TM, d), F32)],
        compiler_params=_params(("parallel", "parallel", "arbitrary")),
        name="out_ffn",
    )(x, *ys, mod4, g2, gf, wo, w1, w3, w2)


def _gated_prelude(blk, d, wg_ref, par_ref, tri_ref, q_s, k_s, v_s, la_s, b_s, *, mode):
    w = GROUP_W
    if mode == "gla":
        q = blk[:, 0:w] * (HEAD_D ** -0.5)
        k = blk[:, w:2 * w]
        v = blk[:, 2 * w:3 * w]
        x = _dot(blk[:, 4 * w:4 * w + LR_PAD].astype(BF16), wg_ref[d]) + par_ref[d:d + 1, :]
        la = _log_sigmoid(x) * (1.0 / GLA_GATE_TEMP)
    else:
        q = _silu(blk[:, 0:w])
        f = blk[:, (1 + d) * w:(2 + d) * w]
        v = blk[:, 3 * w:4 * w]
        e = jnp.exp(-jnp.abs(f))
        k = par_ref[0:1, :] * (jnp.where(f >= 0.0, e, 1.0) / (1.0 + e))
        y = par_ref[2:3, :] + (jnp.minimum(f, 0.0) - jnp.log(1.0 + e))
        lb = par_ref[1:2, :]
        la = jnp.maximum(lb, y) + _log1p_exp_neg_abs(lb - y)
    q_s[d] = q
    k_s[d] = k
    v_s[d] = v
    la_s[d] = la * LOG2E
    b_s[d] = _dot_w2(tri_ref[d], la_s[d])


def _gated_subblock(d, i, ones_ref, mbd_ref, q_s, k_s, v_s, b_s, p_s, o_s, st_s):
    w = GROUP_W
    rev = d == 1
    tio = lax.broadcasted_iota(jnp.int32, (SUB, w), 0)
    edge = 0 if rev else SUB - 1
    rows = pl.ds(pl.multiple_of(i * SUB, SUB), SUB)
    bb = b_s[d, rows, :]
    qb = q_s[d, rows, :]
    kb = k_s[d, rows, :]
    vb = v_s[d, rows, :]

    def row(a, s):
        return jnp.broadcast_to(a[s:s + 1, :], (SUB, w))

    for s in range(SUB):
        valid = (tio <= s) if rev else (tio >= s)
        e = jnp.exp2(jnp.where(valid, bb - row(bb, s), -jnp.inf))
        p_s[d, s * SUB:(s + 1) * SUB, :] = (e * qb * row(kb, s)).astype(BF16)
    r = _dot(p_s[d], ones_ref[...])
    od = jnp.zeros((SUB, w), F32)
    for s in range(SUB):
        od = od + r[s * SUB:(s + 1) * SUB, :] * row(vb, s)
    bend = row(bb, edge)
    st = st_s[d]
    qt = (qb * jnp.exp2(bb)).astype(BF16)
    oi = lax.dot_general(qt, st.astype(BF16), (((1,), (1,)), ((), ())), preferred_element_type=F32)
    kt = (kb * jnp.exp2(bend - bb)).astype(BF16)
    kv = lax.dot_general(vb.astype(BF16), kt, (((0,), (0,)), ((), ())), preferred_element_type=F32)
    st_s[d] = st * jnp.exp2(bend[0:1, :]) + kv * mbd_ref[...]
    o_s[d, rows, :] = od + oi


def _stack_heads(x):
    lane_head = lax.broadcasted_iota(jnp.int32, x.shape, 1) // HEAD_D
    return jnp.concatenate([jnp.where(lane_head == h, x, 0.0) for h in range(N_HEADS)], axis=0).astype(BF16)


def _centred_span(b_s, blk):
    span = None
    for d in range(2):
        for g in range(LC // blk):
            r0, rm, r1 = g * blk, g * blk + blk // 2, (g + 1) * blk - 1
            mid = b_s[d, rm:rm + 1, :]
            m = jnp.maximum(jnp.abs(b_s[d, r0:r0 + 1, :] - mid), jnp.abs(b_s[d, r1:r1 + 1, :] - mid))
            span = m if span is None else jnp.maximum(span, m)
    return jnp.max(span)


def _gated_block_bounded(d, g, blk, q_s, k_s, v_s, b_s, o_s, st_s):
    w = GROUP_W
    rev = d == 1
    edge = 0 if rev else blk - 1
    rows = slice(g * blk, (g + 1) * blk)
    bb = b_s[d, rows, :]
    qb = q_s[d, rows, :]
    kb = k_s[d, rows, :]
    cc = bb - jnp.broadcast_to(bb[blk // 2:blk // 2 + 1, :], bb.shape)
    vm = _stack_heads(v_s[d, rows, :])
    km = _stack_heads(kb * jnp.exp2(-cc))
    sc = lax.dot_general((qb * jnp.exp2(cc)).astype(BF16), km, (((1,), (1,)), ((), ())),
                         preferred_element_type=F32)
    t_i = lax.broadcasted_iota(jnp.int32, sc.shape, 0)
    s_i = lax.broadcasted_iota(jnp.int32, sc.shape, 1) % blk
    od = _dot(jnp.where((t_i <= s_i) if rev else (t_i >= s_i), sc, 0.0).astype(BF16), vm)
    bend = jnp.broadcast_to(bb[edge:edge + 1, :], (blk, w))
    st = st_s[d]
    oi = lax.dot_general((qb * jnp.exp2(bb)).astype(BF16), st.astype(BF16), (((1,), (1,)), ((), ())),
                         preferred_element_type=F32)
    ktm = _stack_heads(kb * jnp.exp2(bend - bb))
    kv = lax.dot_general(vm, ktm, (((0,), (0,)), ((), ())), preferred_element_type=F32)
    st_s[d] = st * jnp.exp2(bend[0:1, :]) + kv
    o_s[d, rows, :] = od + oi


def _gated_kernel(pf_ref, pb_ref, wg_ref, par_ref, s0_ref, tri_ref, ones_ref, mbd_ref,
                  y_ref, st_ref, q_s, k_s, v_s, la_s, b_s, p_s, o_s, st_s, *, mode, nc):
    j = pl.program_id(1)
    w = GROUP_W

    @pl.when(j == 0)
    def _():
        st_s[0] = s0_ref[0, 0]
        st_s[1] = s0_ref[0, 1]

    _gated_prelude(pf_ref[0], 0, wg_ref, par_ref, tri_ref.at[0], q_s, k_s, v_s, la_s, b_s, mode=mode)
    _gated_prelude(pb_ref[0], 1, wg_ref, par_ref, tri_ref.at[0], q_s, k_s, v_s, la_s, b_s, mode=mode)

    def run_bounded(blk):
        for g in range(LC // blk):
            _gated_block_bounded(0, g, blk, q_s, k_s, v_s, b_s, o_s, st_s)
            _gated_block_bounded(1, LC // blk - 1 - g, blk, q_s, k_s, v_s, b_s, o_s, st_s)

    def recumulate(level):
        for d in range(2):
            b_s[d] = _dot_w2(tri_ref[level, d], la_s[d])

    def run_exact():
        def body(it, carry):
            scratch = (q_s, k_s, v_s, b_s, p_s, o_s, st_s)
            _gated_subblock(0, it, ones_ref, mbd_ref, *scratch)
            _gated_subblock(1, N_SUB - 1 - it, ones_ref, mbd_ref, *scratch)
            return carry

        lax.fori_loop(0, N_SUB, body, 0)

    wide = _centred_span(b_s, BOUNDED_BLKS[0]) < FAST_LOG2_RANGE

    @pl.when(wide)
    def _():
        run_bounded(BOUNDED_BLKS[0])

    @pl.when(jnp.logical_not(wide))
    def _():
        recumulate(1)
        narrow = _centred_span(b_s, BOUNDED_BLKS[1]) < FAST_LOG2_RANGE

        @pl.when(narrow)
        def _():
            run_bounded(BOUNDED_BLKS[1])

        @pl.when(jnp.logical_not(narrow))
        def _():
            recumulate(2)
            run_exact()

    @pl.when(j == nc - 1)
    def _():
        for d in range(2):
            for h in range(N_HEADS):
                blk = pl.ds(h * HEAD_D, HEAD_D)
                st_ref[0, d, h] = st_s[d, blk, blk]

    gcol = 3 * w if mode == "gla" else 4 * w

    def finish(o, p_ref):
        ms = _dot_x2(o * o, ones_ref[...]) * (1.0 / HEAD_D)
        return o * lax.rsqrt(ms + EPS) * par_ref[3:4, :] * _silu(p_ref[0][:, gcol:gcol + w])

    rows_f = pl.ds(pl.multiple_of(j * LC, LC), LC)
    rows_b = pl.ds(pl.multiple_of((nc - 1 - j) * LC, LC), LC)
    if nc == 1:
        y_ref[0] = finish(o_s[0] + o_s[1], pf_ref)
    else:
        @pl.when(j < nc // 2)
        def _():
            y_ref[0, rows_f, :] = o_s[0]
            y_ref[0, rows_b, :] = o_s[1]

        @pl.when(j >= nc // 2)
        def _():
            y_ref[0, rows_f, :] = finish(y_ref[0, rows_f, :] + o_s[0], pf_ref)
            y_ref[0, rows_b, :] = finish(y_ref[0, rows_b, :] + o_s[1], pb_ref)


def _gated_consts():
    r = jnp.arange(LC)

    def tri(block):
        same = (r[:, None] // block) == (r[None, :] // block)
        lower = (same & (r[None, :] <= r[:, None])).astype(BF16)
        upper = (same & (r[None, :] >= r[:, None])).astype(BF16)
        return jnp.stack([lower, upper])

    head = (r[:, None] // HEAD_D) == (r[None, :] // HEAD_D)
    tris = jnp.stack([tri(blk) for blk in BOUNDED_BLKS + (SUB,)])
    return tris, head.astype(BF16), head.astype(F32)


def _gated_mixer(p, wg, par, s0t, consts, mode):
    b, l, width = p.shape
    nc = l // LC
    assert nc == 1 or nc % 2 == 0
    tri, ones, mbd = consts
    w = GROUP_W
    const2 = lambda shape: pl.BlockSpec(shape, lambda i, j: (0,) * len(shape))
    return pl.pallas_call(
        functools.partial(_gated_kernel, mode=mode, nc=nc),
        out_shape=[jax.ShapeDtypeStruct((b, l, w), F32),
                   jax.ShapeDtypeStruct((b, 2, N_HEADS, HEAD_D, HEAD_D), F32)],
        grid=(b, nc),
        in_specs=[
            pl.BlockSpec((1, LC, width), lambda i, j: (i, j, 0)),
            pl.BlockSpec((1, LC, width), lambda i, j: (i, nc - 1 - j, 0)),
            const2(wg.shape), const2(par.shape),
            pl.BlockSpec((1, 2, w, w), lambda i, j: (i, 0, 0, 0)),
            const2(tri.shape), const2(ones.shape), const2(mbd.shape),
        ],
        out_specs=[pl.BlockSpec((1, l, w), lambda i, j: (i, 0, 0)),
                   pl.BlockSpec((1, 2, N_HEADS, HEAD_D, HEAD_D), lambda i, j: (i, 0, 0, 0, 0))],
        scratch_shapes=[pltpu.VMEM((2, LC, w), F32), pltpu.VMEM((2, LC, w), F32), pltpu.VMEM((2, LC, w), F32),
                        pltpu.VMEM((2, LC, w), F32), pltpu.VMEM((2, LC, w), F32),
                        pltpu.VMEM((2, SUB * SUB, w), BF16),
                        pltpu.VMEM((2, LC, w), F32), pltpu.VMEM((2, w, w), F32)],
        compiler_params=_params(("parallel", "arbitrary")),
        name="gated_" + mode,
    )(p, p, wg, par, s0t, tri, ones, mbd)


def _state_to_blockdiag_t(s):
    b = s.shape[0]
    st = jnp.swapaxes(s, -1, -2)
    eye = jnp.eye(N_HEADS, dtype=s.dtype)
    full = st[:, :, :, :, None, :] * eye[None, None, :, None, :, None]
    return full.reshape(b, 2, GROUP_W, GROUP_W)


def _blockdiag_t_to_state(st):
    return jnp.swapaxes(st, -1, -2)


def _rglru_kernel(x_ref, g_ref, cw_ref, cb_ref, w_ref, bias_ref, nsp_ref, h0_ref, y_ref, hT_ref,
                  xc_s, a_s, u_s, *, l):
    c = RG_HALF
    x = x_ref[0]
    row = lax.broadcasted_iota(jnp.int32, (l, c), 0)
    xc = x * cw_ref[2:3, :] + cb_ref[...]
    xc = xc + jnp.where(row >= 2, pltpu.roll(x, 2, 0), 0.0) * cw_ref[0:1, :]
    xc = xc + jnp.where(row >= 1, pltpu.roll(x, 1, 0), 0.0) * cw_ref[1:2, :]
    xc = xc + jnp.where(row <= l - 2, pltpu.roll(x, l - 1, 0), 0.0) * cw_ref[3:4, :]
    xc_s[...] = xc

    nslab = l // RG_ROWS
    grp = (RG_ROWS // SUBLANE, SUBLANE, c)
    sub = lax.broadcasted_iota(jnp.int32, grp, 1)

    def slab(n, carry):
        rows = pl.ds(pl.multiple_of(n * RG_ROWS, RG_ROWS), RG_ROWS)
        xs = xc_s[rows, :]
        gates = _sigmoid(_dot(xs.astype(BF16), w_ref[...]) + bias_ref[...])
        for d in range(2):
            r = gates[:, (2 * d) * c:(2 * d + 1) * c]
            i = gates[:, (2 * d + 1) * c:(2 * d + 2) * c]
            log_a = r * nsp_ref[d:d + 1, :]
            a_flat = jnp.exp(log_a)
            a = a_flat.reshape(grp)
            u = (jnp.sqrt(1.0 - a_flat * a_flat) * (i * xs)).reshape(grp)
            for sft in (1, 2, 4):
                if d == 0:
                    ok = sub >= sft
                    a_n, u_n = pltpu.roll(a, sft, 1), pltpu.roll(u, sft, 1)
                else:
                    ok = sub <= SUBLANE - 1 - sft
                    a_n, u_n = pltpu.roll(a, SUBLANE - sft, 1), pltpu.roll(u, SUBLANE - sft, 1)
                u = jnp.where(ok, a * u_n + u, u)
                a = jnp.where(ok, a * a_n, a)
            a_s[d, rows, :] = a.reshape(RG_ROWS, c)
            u_s[d, rows, :] = u.reshape(RG_ROWS, c)
        return carry

    lax.fori_loop(0, nslab, slab, 0)

    ngrp = l // SUBLANE

    def carry_step(n, hs):
        hf, hb = hs
        rows_f = pl.ds(pl.multiple_of(n * SUBLANE, SUBLANE), SUBLANE)
        rows_b = pl.ds(pl.multiple_of((ngrp - 1 - n) * SUBLANE, SUBLANE), SUBLANE)
        hh_f = a_s[0, rows_f, :] * hf + u_s[0, rows_f, :]
        hh_b = a_s[1, rows_b, :] * hb + u_s[1, rows_b, :]
        u_s[0, rows_f, :] = hh_f
        u_s[1, rows_b, :] = hh_b
        return (jnp.broadcast_to(hh_f[SUBLANE - 1:SUBLANE, :], (SUBLANE, c)),
                jnp.broadcast_to(hh_b[0:1, :], (SUBLANE, c)))

    h_init = (jnp.broadcast_to(h0_ref[0, 0:1, :], (SUBLANE, c)), jnp.broadcast_to(h0_ref[0, 1:2, :], (SUBLANE, c)))
    hf, hb = lax.fori_loop(0, ngrp, carry_step, h_init, unroll=2)
    hT_ref[0, 0:1, :] = hf[0:1, :]
    hT_ref[0, 1:2, :] = hb[0:1, :]

    g = g_ref[0]
    gelu = 0.5 * g * (1.0 + jnp.tanh(math.sqrt(2.0 / math.pi) * (g + 0.044715 * (g * g * g))))
    y_ref[0] = (u_s[0] + u_s[1]) * gelu


def _rglru(p, cw, cb, wbd, bias, nsp, h0):
    b, l, _ = p.shape
    c = RG_HALF
    half = lambda shape: pl.BlockSpec(shape, lambda i, j: (0,) * (len(shape) - 1) + (j,))
    return pl.pallas_call(
        functools.partial(_rglru_kernel, l=l),
        out_shape=[jax.ShapeDtypeStruct((b, l, GROUP_W), F32), jax.ShapeDtypeStruct((b, 2, GROUP_W), F32)],
        grid=(b, 2),
        in_specs=[
            pl.BlockSpec((1, l, c), lambda i, j: (i, 0, j)),
            pl.BlockSpec((1, l, c), lambda i, j: (i, 0, 2 + j)),
            half((RG_CONV_W, c)), half((1, c)),
            pl.BlockSpec((None, c, 4 * c), lambda i, j: (j, 0, 0)),
            pl.BlockSpec((None, 1, 4 * c), lambda i, j: (j, 0, 0)),
            half((2, c)),
            pl.BlockSpec((1, 2, c), lambda i, j: (i, 0, j)),
        ],
        out_specs=[pl.BlockSpec((1, l, c), lambda i, j: (i, 0, j)),
                   pl.BlockSpec((1, 2, c), lambda i, j: (i, 0, j))],
        scratch_shapes=[pltpu.VMEM((l, c), F32), pltpu.VMEM((2, l, c), F32), pltpu.VMEM((2, l, c), F32)],
        compiler_params=_params(("parallel", "parallel")),
        name="rglru",
    )(p, p, cw, cb, wbd, bias, nsp, h0)


def _short_conv3(x, w_ref, b_ref):
    l = x.shape[0]
    row = lax.broadcasted_iota(jnp.int32, x.shape, 0)
    o = x * w_ref[1:2, :] + b_ref[...]
    o = o + jnp.where(row >= 1, pltpu.roll(x, 1, 0), 0.0) * w_ref[0:1, :]
    return o + jnp.where(row <= l - 2, pltpu.roll(x, l - 1, 0), 0.0) * w_ref[2:3, :]


def _hy_filter_kernel(pe_ref, w1_ref, b1_ref, w2_ref, b2_ref, w3_ref, dec_ref, o_ref):
    pe = pe_ref[...]
    h = jnp.sin(_dot3(pe, w1_ref[...]) + b1_ref[...])
    h = jnp.sin(_dot3(h, w2_ref[...]) + b2_ref[...])
    h = _dot3(h, w3_ref[...])
    dist = pe[:, LANE - 1:LANE]
    h = h * jnp.exp(-dist * dec_ref[...])
    o_ref[...] = h / jnp.sum(jnp.abs(h), axis=0, keepdims=True)


def _hy_filters(pe, w1, b1, w2, b2, w3, decay):
    l = pe.shape[0]
    n = w3.shape[1]
    args = (pe, w1, b1, w2, b2, w3, decay)
    return pl.pallas_call(
        _hy_filter_kernel,
        out_shape=jax.ShapeDtypeStruct((l, n), F32),
        grid=(1,),
        in_specs=[pl.BlockSpec(a.shape, lambda i: (0, 0)) for a in args],
        out_specs=pl.BlockSpec((l, n), lambda i: (0, 0)),
        compiler_params=_params(("arbitrary",)),
        name="hy_filters",
    )(*args)


def _hy_conv_kernel(z_ref, x_ref, h_ref, skip_ref, wz_ref, bz_ref, wx_ref, bx_ref,
                    ct_ref, st_ref, wc_ref, ws_ref, o_ref, zt_s, *, bb, nk, z_raw):
    kt = pl.program_id(1)
    c = GROUP_W

    def z_of(b):
        return _short_conv3(z_ref[b], wz_ref, bz_ref) if z_raw else z_ref[b]

    @pl.when(kt == 0)
    def _():
        o_ref[...] = jnp.zeros_like(o_ref)
        for b in range(bb):
            zt_s[b * c:(b + 1) * c, :] = z_of(b).T.astype(BF16)
        zt_s[bb * c:, :] = h_ref[...].T.astype(BF16)

    zt = zt_s[...]
    xc = _dot(zt, ct_ref[...])
    xs = _dot(zt, st_ref[...])
    hc = xc[bb * c:, :]
    hs = xs[bb * c:, :]
    wc = wc_ref[...]
    ws = ws_ref[...]
    for b in range(bb):
        zc = xc[b * c:(b + 1) * c, :]
        zsn = xs[b * c:(b + 1) * c, :]
        yc = zc * hc - zsn * hs
        ys = zc * hs + zsn * hc
        a = (wc * yc + ws * ys).T.astype(BF16)
        bm = (wc * ys - ws * yc).T.astype(BF16)
        o_ref[b] += _dot(ct_ref[...], a) + _dot(st_ref[...], bm)

    @pl.when(kt == nk - 1)
    def _():
        for b in range(bb):
            gate = _short_conv3(x_ref[b], wx_ref, bx_ref)
            o_ref[b] = gate * (o_ref[b] + skip_ref[...] * z_of(b))


def _hy_long_conv(pc, zprev, z_col, gate_col, cw, cb, h, h_col, skip, skip_col, tables, bb):
    b, l, _ = pc.shape
    c = GROUP_W
    ct, st, wc, ws = tables
    nk = ct.shape[1] // FREQ_TILE
    z_raw = zprev is None
    z_arr = pc if z_raw else zprev
    return pl.pallas_call(
        functools.partial(_hy_conv_kernel, bb=bb, nk=nk, z_raw=z_raw),
        out_shape=jax.ShapeDtypeStruct((b, l, c), F32),
        grid=(b // bb, nk),
        in_specs=[
            pl.BlockSpec((bb, l, c), lambda g, k: (g, 0, z_col)),
            pl.BlockSpec((bb, l, c), lambda g, k: (g, 0, gate_col)),
            pl.BlockSpec((l, c), lambda g, k: (0, h_col)),
            pl.BlockSpec((1, c), lambda g, k: (0, skip_col)),
            pl.BlockSpec((3, c), lambda g, k: (0, z_col)),
            pl.BlockSpec((1, c), lambda g, k: (0, z_col)),
            pl.BlockSpec((3, c), lambda g, k: (0, gate_col)),
            pl.BlockSpec((1, c), lambda g, k: (0, gate_col)),
            pl.BlockSpec((l, FREQ_TILE), lambda g, k: (0, k)),
            pl.BlockSpec((l, FREQ_TILE), lambda g, k: (0, k)),
            pl.BlockSpec((1, FREQ_TILE), lambda g, k: (0, k)),
            pl.BlockSpec((1, FREQ_TILE), lambda g, k: (0, k)),
        ],
        out_specs=pl.BlockSpec((bb, l, c), lambda g, k: (g, 0, 0)),
        scratch_shapes=[pltpu.VMEM(((bb + 1) * c, l), BF16)],
        compiler_params=_params(("parallel", "arbitrary")),
        name="hy_long_conv",
    )(z_arr, pc, h, skip, cw, cb, cw, cb, ct, st, wc, ws)


def _dft_tables(l):
    n = 3 * l // 2
    nf = n // 2 + 1
    nfp = -(-nf // FREQ_TILE) * FREQ_TILE
    k = jnp.arange(nfp, dtype=jnp.int32)
    ta = jnp.arange(l // TWID, dtype=jnp.int32) * TWID
    tb = jnp.arange(TWID, dtype=jnp.int32)
    live = (k < nf)

    def cos_sin(m):
        ang = (m % n).astype(F32) * (2.0 * math.pi / n)
        return jnp.cos(ang), jnp.sin(ang)

    ca, sa = cos_sin(ta[:, None] * k[None, :])
    cb, sb = cos_sin(tb[:, None] * k[None, :])
    ca, sa = (jnp.where(live[None, :], x, 0.0)[:, None, :] for x in (ca, sa))
    cb, sb = cb[None, :, :], sb[None, :, :]
    ct = (ca * cb - sa * sb).reshape(l, nfp).astype(BF16)
    st = (sa * cb + ca * sb).reshape(l, nfp).astype(BF16)
    wk = jnp.where((k == 0) | (k == n // 2), 1.0, 2.0) / n
    wk = jnp.where(live, wk, 0.0).astype(F32)
    cp, sp = cos_sin(k * (l // 2))
    return ct, st, (wk * cp)[None, :], (wk * sp)[None, :]


def _hy_pos_features(l):
    pos = jnp.arange(l, dtype=F32)
    t = pos / (l - 1)
    ang = (2.0 * jnp.pi * pos / l)[:, None] * jnp.linspace(1e-4, HY_POS_BANDS - 1, HY_POS_BANDS, dtype=F32)[None, :]
    half = l // 2
    dist = jnp.abs(pos - half) / half
    pe = jnp.concatenate([t[:, None], jnp.cos(ang), -jnp.sin(ang)], axis=-1)
    pad = jnp.zeros((l, LANE - 1 - pe.shape[1]), F32)
    return jnp.concatenate([pe, pad, dist[:, None]], axis=-1)


def _grid_position_encoding(n_pos, dim):
    quarter = dim // 4
    omega = 1.0 / (POS_BASE ** (jnp.arange(quarter, dtype=F32) / quarter))
    ang = jnp.arange(n_pos, dtype=F32)[:, None] * omega[None, :]
    return jnp.concatenate([jnp.sin(ang), jnp.cos(ang)], axis=-1)


def _blockdiag2(a, b):
    z = jnp.zeros_like(a)
    return jnp.concatenate([jnp.concatenate([a, z], axis=1), jnp.concatenate([z, b], axis=1)], axis=0)


def _layer_params(l, hg_lb, norm1_g, norm2_g, w_in, w_out, gla_w_gate, gla_b_gate, gla_norm_g,
                  rg_conv_w, rg_conv_b, rg_w_a, rg_b_a, rg_w_x, rg_b_x, rg_lambda,
                  hy_conv_w, hy_conv_b, hy_w1, hy_b1, hy_w2, hy_b2, hy_w3, hy_decay, hy_skip,
                  hg_norm_g, ffn_w1, ffn_w3, ffn_w2):
    d, w = D_MODEL, GROUP_W
    n_gla = 4 * w + GLA_LOWRANK
    wi = w_in[l]
    w_proj = jnp.concatenate([wi[:, :n_gla], jnp.zeros((d, LR_PAD - GLA_LOWRANK), F32), wi[:, n_gla:]], axis=1)
    wg = jnp.concatenate([gla_w_gate[l], jnp.zeros((2, LR_PAD - GLA_LOWRANK, w), F32)], axis=1)
    zrow = jnp.zeros((w,), F32)
    par_gla = jnp.stack([gla_b_gate[l, 0], gla_b_gate[l, 1], zrow, gla_norm_g[l]] + [zrow] * 4)
    lb = hg_lb[l]
    par_hg = jnp.stack([1.0 - lb, jnp.log(lb), jnp.log1p(-lb), hg_norm_g[l]] + [zrow] * 4)
    wa, wx = rg_w_a[l], rg_w_x[l]
    rg_w = jnp.stack([
        jnp.concatenate([_blockdiag2(m[dd, 2 * j], m[dd, 2 * j + 1]) for dd in range(2) for m in (wa, wx)], axis=1)
        for j in range(2)])
    ba, bx = rg_b_a[l], rg_b_x[l]
    rg_bias = jnp.stack([
        jnp.concatenate([v[dd, j * RG_HALF:(j + 1) * RG_HALF] for dd in range(2) for v in (ba, bx)])[None, :]
        for j in range(2)])
    nsp = -RG_C * jax.nn.softplus(-rg_lambda[l])
    w1p = jnp.concatenate([hy_w1[l], jnp.zeros((LANE - hy_w1.shape[1], HY_FFN_W), F32)], axis=0)
    return dict(
        norm1=norm1_g[l][None, :], norm2=norm2_g[l][None, :],
        w_proj=w_proj.astype(BF16), w_out=w_out[l].astype(BF16),
        wg=wg.astype(BF16), par_gla=par_gla, par_hg=par_hg,
        rg_cw=rg_conv_w[l], rg_cb=rg_conv_b[l][None, :], rg_w=rg_w.astype(BF16), rg_bias=rg_bias, rg_nsp=nsp,
        hy_cw=hy_conv_w[l], hy_cb=hy_conv_b[l][None, :], hy_w1=w1p, hy_b1=hy_b1[l][None, :],
        hy_w2=hy_w2[l], hy_b2=hy_b2[l][None, :], hy_w3=hy_w3[l], hy_decay=hy_decay[l][None, :],
        hy_skip=hy_skip[l][None, :],
        w1=ffn_w1[l].astype(BF16), w3=ffn_w3[l].astype(BF16), w2=ffn_w2[l].astype(BF16))


def _trunk_layer(x, p, mod4, layer, row0, seq_shape, s_gla, s_rg, s_hg, stream_consts, final, final_g):
    bm, lm, d = x.shape
    b, l = seq_shape
    gated_consts, pe, tables, bb = stream_consts
    pa, pb, pc, pd = _norm_proj(x, mod4, layer, row0, p["norm1"], p["w_proj"])
    pa, pb, pc, pd = (t.reshape(b, l, t.shape[-1]) for t in (pa, pb, pc, pd))

    ya, st_a = _gated_mixer(pa, p["wg"], p["par_gla"], _state_to_blockdiag_t(s_gla), gated_consts, "gla")
    yd, st_d = _gated_mixer(pd, p["wg"], p["par_hg"], _state_to_blockdiag_t(s_hg), gated_consts, "hg")
    yb, st_b = _rglru(pb, p["rg_cw"], p["rg_cb"], p["rg_w"], p["rg_bias"], p["rg_nsp"], s_rg)

    filt = _hy_filters(pe, p["hy_w1"], p["hy_b1"], p["hy_w2"], p["hy_b2"], p["hy_w3"], p["hy_decay"])
    z1 = _hy_long_conv(pc, None, 0, 1, p["hy_cw"], p["hy_cb"], filt, 0, p["hy_skip"], 0, tables, bb)
    yc = _hy_long_conv(pc, z1, 0, 2, p["hy_cw"], p["hy_cb"], filt, 1, p["hy_skip"], 1, tables, bb)

    ys = [t.reshape(bm, lm, GROUP_W) for t in (ya, yb, yc, yd)]
    x = _out_ffn(x, ys, mod4, layer, row0, p["norm2"], final_g, p["w_out"], p["w1"], p["w3"], p["w2"], final)
    return x, (_blockdiag_t_to_state(st_a), st_b, _blockdiag_t_to_state(st_d))


def kernel(x_prompt, x_sample, state_gla, state_rglru, state_hgrn, c, c_ctx, norm1_g, norm2_g, final_norm_g, w_mod, b_mod, w_in, w_out, gla_w_gate, gla_b_gate, gla_norm_g, rg_conv_w, rg_conv_b, rg_w_a, rg_b_a, rg_w_x, rg_b_x, rg_lambda, hy_conv_w, hy_conv_b, hy_w1, hy_b1, hy_w2, hy_b2, hy_w3, hy_decay, hy_skip, hg_lower, hg_norm_g, ffn_w1, ffn_w3, ffn_w2):
    depth = w_in.shape[0]
    nb, seq, d = x_prompt.shape
    db, dseq, _ = x_sample.shape

    hg_lb = jnp.cumsum(jax.nn.softmax(hg_lower.astype(F32), axis=0), axis=0)
    hg_lb = hg_lb - hg_lb[0:1]

    cvec = jnp.concatenate([c_ctx[None, :], c, jnp.zeros((SUBLANE - 1 - db, d), F32)], axis=0)
    mod4 = _modulation(cvec, w_mod, b_mod).reshape(depth, SUBLANE, 1, N_MOD * d)

    gated_consts = _gated_consts()
    consts_p = (gated_consts, _hy_pos_features(seq), _dft_tables(seq), 8)
    consts_s = (gated_consts, _hy_pos_features(dseq), _dft_tables(dseq), 1)

    xp = x_prompt.reshape(1, nb * seq, d)
    xs = _add_pos(x_sample, _grid_position_encoding(max(dseq // GRID_W, GRID_W), d))
    zero_gla = jnp.zeros((nb, 2, N_HEADS, HEAD_D, HEAD_D), F32)
    zero_rg = jnp.zeros((nb, 2, GROUP_W), F32)
    final_g = final_norm_g[None, :]

    gla_states, rg_states, hg_states = [], [], []
    for l in range(depth):
        p = _layer_params(l, hg_lb, norm1_g, norm2_g, w_in, w_out, gla_w_gate, gla_b_gate, gla_norm_g,
                          rg_conv_w, rg_conv_b, rg_w_a, rg_b_a, rg_w_x, rg_b_x, rg_lambda,
                          hy_conv_w, hy_conv_b, hy_w1, hy_b1, hy_w2, hy_b2, hy_w3, hy_decay, hy_skip,
                          hg_norm_g, ffn_w1, ffn_w3, ffn_w2)
        final = l == depth - 1
        xp, (sg, sr, sh) = _trunk_layer(xp, p, mod4, l, 0, (nb, seq), zero_gla, zero_rg, zero_gla,
                                        consts_p, final, final_g)
        xs, _ = _trunk_layer(xs, p, mod4, l, 1, (db, dseq), state_gla[:, l], state_rglru[:, l],
                             state_hgrn[:, l], consts_s, final, final_g)
        gla_states.append(sg)
        rg_states.append(sr)
        hg_states.append(sh)

    return (xp.reshape(nb, seq, d), xs,
            jnp.stack(gla_states, axis=1), jnp.stack(rg_states, axis=1), jnp.stack(hg_states, axis=1))
```

```python
import functools
import math

import jax
import jax.numpy as jnp
from jax import lax
from jax.experimental import pallas as pl
from jax.experimental.pallas import tpu as pltpu

F32 = jnp.float32
BF16 = jnp.bfloat16

D_MODEL = 1024
N_MOD = 6
GROUP_W = 256
N_HEADS = 4
HEAD_D = GROUP_W // N_HEADS
GLA_LOWRANK = 16
GLA_GATE_TEMP = 16.0
RG_C = 8.0
RG_CONV_W = 4
HY_POS_BANDS = 16
HY_FFN_W = 64
D_FF = 2816
EPS = 1e-6
GRID_W = 64
POS_BASE = 10000.0

LANE = 128
SUBLANE = 8
VMEM_LIMIT = 56 * 1024 * 1024

LOG2E = 1.0 / math.log(2.0)
FAST_LOG2_RANGE = 96.0
SUB = 16
LC = 256
N_SUB = LC // SUB
BOUNDED_BLKS = (64, 32)
LR_PAD = LANE
W_GLA = 4 * GROUP_W + LR_PAD
W_RG = 2 * GROUP_W
W_HY = 3 * GROUP_W
W_HG = 5 * GROUP_W
W_PROJ = W_GLA + W_RG + W_HY + W_HG
W_GLA_OUT = 6 * GROUP_W
W_HG_OUT = 7 * GROUP_W
GATED_COLS = {"gla": dict(q=0, k=(1, 1), v=2, gate=3, la=(4, 5)),
              "hg": dict(q=0, k=(1, 2), v=3, gate=4, la=(5, 6))}
TM = 512
TF = D_FF // 2
TN_MOD = 512
RG_HALF = GROUP_W // 2
RG_ROWS = 256
FREQ_TILE = 256
TWID = 64


def _dot(a, b):
    return jnp.dot(a, b, preferred_element_type=F32)


def _split(x):
    hi = x.astype(BF16)
    lo = (x - hi.astype(F32)).astype(BF16)
    return hi, lo


def _dot_x2(x, w):
    hi, lo = _split(x)
    return _dot(hi, w) + _dot(lo, w)


def _dot_w2(w, x):
    hi, lo = _split(x)
    return _dot(w, hi) + _dot(w, lo)


def _dot3(a, b):
    ah, al = _split(a)
    bh, bl = _split(b)
    return _dot(ah, bh) + _dot(al, bh) + _dot(ah, bl)


def _sigmoid(x):
    return 0.5 * jnp.tanh(0.5 * x) + 0.5


def _silu(x):
    return x * _sigmoid(x)


def _log1p_exp_neg_abs(x):
    return jnp.log(1.0 + jnp.exp(-jnp.abs(x)))


def _log_sigmoid(x):
    return jnp.minimum(x, 0.0) - _log1p_exp_neg_abs(x)


def _rms(x):
    return x * lax.rsqrt(jnp.mean(x * x, axis=-1, keepdims=True) + EPS)


def _params(sem, vmem=VMEM_LIMIT):
    return pltpu.CompilerParams(dimension_semantics=sem, vmem_limit_bytes=vmem)


def _mod_kernel(c_ref, w_ref, b_ref, o_ref):
    c = c_ref[...]
    o_ref[0] = _dot3(_silu(c), w_ref[0]) + b_ref[0]


def _modulation(cvec, w_mod, b_mod):
    depth = w_mod.shape[0]
    n = N_MOD * D_MODEL
    return pl.pallas_call(
        _mod_kernel,
        out_shape=jax.ShapeDtypeStruct((depth, SUBLANE, n), F32),
        grid=(depth, n // TN_MOD),
        in_specs=[
            pl.BlockSpec((SUBLANE, D_MODEL), lambda l, j: (0, 0)),
            pl.BlockSpec((1, D_MODEL, TN_MOD), lambda l, j: (l, 0, j)),
            pl.BlockSpec((1, 1, TN_MOD), lambda l, j: (l, 0, j)),
        ],
        out_specs=pl.BlockSpec((1, SUBLANE, TN_MOD), lambda l, j: (l, 0, j)),
        compiler_params=_params(("parallel", "parallel")),
        name="modulation",
    )(cvec, w_mod, b_mod.reshape(depth, 1, n))


def _add_kernel(x_ref, e_ref, o_ref):
    j = pl.program_id(1)
    half = D_MODEL // 2
    enc = e_ref[0:GRID_W, :]
    for r in range(TM // GRID_W):
        rows = slice(r * GRID_W, (r + 1) * GRID_W)
        enc_row = e_ref[pl.ds(j * (TM // GRID_W) + r, 1), :]
        o_ref[0, rows, 0:half] = x_ref[0, rows, 0:half] + enc_row
        o_ref[0, rows, half:] = x_ref[0, rows, half:] + enc


def _add_pos(x, enc):
    b, l, d = x.shape
    assert l // GRID_W <= enc.shape[0]
    return pl.pallas_call(
        _add_kernel,
        out_shape=jax.ShapeDtypeStruct(x.shape, F32),
        grid=(b, l // TM),
        in_specs=[pl.BlockSpec((1, TM, d), lambda i, j: (i, j, 0)),
                  pl.BlockSpec(enc.shape, lambda i, j: (0, 0))],
        out_specs=pl.BlockSpec((1, TM, d), lambda i, j: (i, j, 0)),
        compiler_params=_params(("parallel", "parallel")),
        name="add_pos",
    )(x, enc)


def _gla_features(p, wg, par):
    w = GROUP_W
    lr = p[:, 4 * w:4 * w + LR_PAD].astype(BF16)
    las = [_log_sigmoid(_dot(lr, wg[d]) + par[d:d + 1, :]) * (LOG2E / GLA_GATE_TEMP) for d in range(2)]
    return [p[:, 0:w] * (HEAD_D ** -0.5), p[:, w:2 * w], p[:, 2 * w:3 * w], _silu(p[:, 3 * w:4 * w])] + las


def _hg_features(p, par):
    w = GROUP_W
    ks, las = [], []
    for d in range(2):
        f = p[:, (1 + d) * w:(2 + d) * w]
        e = jnp.exp(-jnp.abs(f))
        ks.append(par[0:1, :] * (jnp.where(f >= 0.0, e, 1.0) / (1.0 + e)))
        y = par[2:3, :] + (jnp.minimum(f, 0.0) - jnp.log(1.0 + e))
        lb = par[1:2, :]
        las.append((jnp.maximum(lb, y) + _log1p_exp_neg_abs(lb - y)) * LOG2E)
    return [_silu(p[:, 0:w]), ks[0], ks[1], p[:, 3 * w:4 * w], _silu(p[:, 4 * w:5 * w]), las[0], las[1]]


def _proj_kernel(x_ref, mod_ref, g_ref, w_ref, wg_ref, pg_ref, ph_ref, oa_ref, ob_ref, oc_ref, od_ref):
    x = x_ref[0]
    m = mod_ref[0, 0]
    sh = m[:, 0:D_MODEL]
    sc = m[:, D_MODEL:2 * D_MODEL]
    u = _rms(x) * g_ref[...] * (1.0 + sc) + sh
    p = _dot(u.astype(BF16), w_ref[...])
    w = GROUP_W
    for i, t in enumerate(_gla_features(p[:, 0:W_GLA], wg_ref, pg_ref)):
        oa_ref[0, :, i * w:(i + 1) * w] = t
    ob_ref[0] = p[:, W_GLA:W_GLA + W_RG]
    oc_ref[0] = p[:, W_GLA + W_RG:W_GLA + W_RG + W_HY]
    for i, t in enumerate(_hg_features(p[:, W_GLA + W_RG + W_HY:W_PROJ], ph_ref)):
        od_ref[0, :, i * w:(i + 1) * w] = t


def _norm_proj(x, mod4, layer, row0, gain, w, wg, par_gla, par_hg):
    bm, lm, d = x.shape
    widths = (W_GLA_OUT, W_RG, W_HY, W_HG_OUT)
    const = lambda a: pl.BlockSpec(a.shape, lambda i, j: (0,) * a.ndim)
    return pl.pallas_call(
        _proj_kernel,
        out_shape=[jax.ShapeDtypeStruct((bm, lm, wd), F32) for wd in widths],
        grid=(bm, lm // TM),
        in_specs=[
            pl.BlockSpec((1, TM, d), lambda i, j: (i, j, 0)),
            pl.BlockSpec((1, 1, 1, N_MOD * d), lambda i, j: (layer, row0 + i, 0, 0)),
            pl.BlockSpec((1, d), lambda i, j: (0, 0)),
            pl.BlockSpec((d, W_PROJ), lambda i, j: (0, 0)),
            const(wg), const(par_gla), const(par_hg),
        ],
        out_specs=[pl.BlockSpec((1, TM, wd), lambda i, j: (i, j, 0)) for wd in widths],
        compiler_params=_params(("parallel", "parallel")),
        name="norm_proj",
    )(x, mod4, gain, w, wg, par_gla, par_hg)


def _ffn_kernel(x_ref, ya_ref, yb_ref, yc_ref, yd_ref, mod_ref, g2_ref, gf_ref,
                wo_ref, w1_ref, w3_ref, w2_ref, o_ref, x1_s, u_s, acc_s, *, nf, final):
    f = pl.program_id(2)
    d = D_MODEL

    @pl.when(f == 0)
    def _():
        m = mod_ref[0, 0]
        g1 = m[:, 2 * d:3 * d]
        sh2 = m[:, 3 * d:4 * d]
        sc2 = m[:, 4 * d:5 * d]
        y = jnp.concatenate([ya_ref[0], yb_ref[0], yc_ref[0], yd_ref[0]], axis=-1)
        x1 = x_ref[0] + g1 * _dot(y.astype(BF16), wo_ref[...])
        x1_s[...] = x1
        u_s[...] = (_rms(x1) * g2_ref[...] * (1.0 + sc2) + sh2).astype(BF16)
        acc_s[...] = jnp.zeros_like(acc_s)

    u = u_s[...]
    h = _silu(_dot(u, w1_ref[...])) * _dot(u, w3_ref[...])
    acc_s[...] += _dot(h.astype(BF16), w2_ref[...])

    @pl.when(f == nf - 1)
    def _():
        g2 = mod_ref[0, 0][:, 5 * d:6 * d]
        xo = x1_s[...] + g2 * acc_s[...]
        if final:
            xo = _rms(xo) * gf_ref[...]
        o_ref[0] = xo


def _out_ffn(x, ys, mod4, layer, row0, g2, gf, wo, w1, w3, w2, final):
    bm, lm, d = x.shape
    nf = D_FF // TF
    tok = lambda wd: pl.BlockSpec((1, TM, wd), lambda i, j, f: (i, j, 0))
    return pl.pallas_call(
        functools.partial(_ffn_kernel, nf=nf, final=final),
        out_shape=jax.ShapeDtypeStruct(x.shape, F32),
        grid=(bm, lm // TM, nf),
        in_specs=[
            tok(d), tok(GROUP_W), tok(GROUP_W), tok(GROUP_W), tok(GROUP_W),
            pl.BlockSpec((1, 1, 1, N_MOD * d), lambda i, j, f: (layer, row0 + i, 0, 0)),
            pl.BlockSpec((1, d), lambda i, j, f: (0, 0)),
            pl.BlockSpec((1, d), lambda i, j, f: (0, 0)),
            pl.BlockSpec((d, d), lambda i, j, f: (0, 0)),
            pl.BlockSpec((d, TF), lambda i, j, f: (0, f)),
            pl.BlockSpec((d, TF), lambda i, j, f: (0, f)),
            pl.BlockSpec((TF, d), lambda i, j, f: (f, 0)),
        ],
        out_specs=tok(d),
        scratch_shapes=[pltpu.VMEM((TM, d), F32), pltpu.VMEM((TM, d), BF16), pltpu.VMEM((TM, d), F32)],
        compiler_params=_params(("parallel", "parallel", "arbitrary")),
        name="out_ffn",
    )(x, *ys, mod4, g2, gf, wo, w1, w3, w2)


def _operand(p_ref, mode, name, d, rows):
    col = GATED_COLS[mode][name]
    col = col[d] if isinstance(col, tuple) else col
    return p_ref[0, rows, col * GROUP_W:(col + 1) * GROUP_W]


def _gated_subblock(d, i, p_ref, mode, ones_ref, mbd_ref, b_s, p_s, o_s, st_s):
    w = GROUP_W
    rev = d == 1
    tio = lax.broadcasted_iota(jnp.int32, (SUB, w), 0)
    edge = 0 if rev else SUB - 1
    rows = pl.ds(pl.multiple_of(i * SUB, SUB), SUB)
    bb = b_s[d, rows, :]
    qb = _operand(p_ref, mode, "q", d, rows)
    kb = _operand(p_ref, mode, "k", d, rows)
    vb = _operand(p_ref, mode, "v", d, rows)

    def row(a, s):
        return jnp.broadcast_to(a[s:s + 1, :], (SUB, w))

    for s in range(SUB):
        valid = (tio <= s) if rev else (tio >= s)
        e = jnp.exp2(jnp.where(valid, bb - row(bb, s), -jnp.inf))
        p_s[d, s * SUB:(s + 1) * SUB, :] = (e * qb * row(kb, s)).astype(BF16)
    r = _dot(p_s[d], ones_ref[...])
    od = jnp.zeros((SUB, w), F32)
    for s in range(SUB):
        od = od + r[s * SUB:(s + 1) * SUB, :] * row(vb, s)
    bend = row(bb, edge)
    st = st_s[d]
    qt = (qb * jnp.exp2(bb)).astype(BF16)
    oi = lax.dot_general(qt, st.astype(BF16), (((1,), (1,)), ((), ())), preferred_element_type=F32)
    kt = (kb * jnp.exp2(bend - bb)).astype(BF16)
    kv = lax.dot_general(vb.astype(BF16), kt, (((0,), (0,)), ((), ())), preferred_element_type=F32)
    st_s[d] = st * jnp.exp2(bend[0:1, :]) + kv * mbd_ref[...]
    o_s[d, rows, :] = od + oi


def _stack_heads(x):
    lane_head = lax.broadcasted_iota(jnp.int32, x.shape, 1) // HEAD_D
    return jnp.concatenate([jnp.where(lane_head == h, x, 0.0) for h in range(N_HEADS)], axis=0).astype(BF16)


def _centred_span(b_s, blk):
    span = None
    for d in range(2):
        for g in range(LC // blk):
            r0, rm, r1 = g * blk, g * blk + blk // 2, (g + 1) * blk - 1
            mid = b_s[d, rm:rm + 1, :]
            m = jnp.maximum(jnp.abs(b_s[d, r0:r0 + 1, :] - mid), jnp.abs(b_s[d, r1:r1 + 1, :] - mid))
            span = m if span is None else jnp.maximum(span, m)
    return jnp.max(span)


def _gated_block_bounded(d, g, blk, p_ref, mode, b_s, o_s, st_s):
    w = GROUP_W
    rev = d == 1
    edge = 0 if rev else blk - 1
    rows = slice(g * blk, (g + 1) * blk)
    bb = b_s[d, rows, :]
    qb = _operand(p_ref, mode, "q", d, rows)
    kb = _operand(p_ref, mode, "k", d, rows)
    cc = bb - jnp.broadcast_to(bb[blk // 2:blk // 2 + 1, :], bb.shape)
    vm = _stack_heads(_operand(p_ref, mode, "v", d, rows))
    km = _stack_heads(kb * jnp.exp2(-cc))
    sc = lax.dot_general((qb * jnp.exp2(cc)).astype(BF16), km, (((1,), (1,)), ((), ())),
                         preferred_element_type=F32)
    t_i = lax.broadcasted_iota(jnp.int32, sc.shape, 0)
    s_i = lax.broadcasted_iota(jnp.int32, sc.shape, 1) % blk
    od = _dot(jnp.where((t_i <= s_i) if rev else (t_i >= s_i), sc, 0.0).astype(BF16), vm)
    bend = jnp.broadcast_to(bb[edge:edge + 1, :], (blk, w))
    st = st_s[d]
    oi = lax.dot_general((qb * jnp.exp2(bb)).astype(BF16), st.astype(BF16), (((1,), (1,)), ((), ())),
                         preferred_element_type=F32)
    ktm = _stack_heads(kb * jnp.exp2(bend - bb))
    kv = lax.dot_general(vm, ktm, (((0,), (0,)), ((), ())), preferred_element_type=F32)
    st_s[d] = st * jnp.exp2(bend[0:1, :]) + kv
    o_s[d, rows, :] = od + oi


def _gated_kernel(pf_ref, pb_ref, gain_ref, s0_ref, tri_ref, ones_ref, mbd_ref,
                  y_ref, st_ref, b_s, p_s, o_s, st_s, *, mode, nc):
    j = pl.program_id(1)
    p_refs = (pf_ref, pb_ref)
    all_rows = slice(0, LC)

    @pl.when(j == 0)
    def _():
        st_s[0] = s0_ref[0, 0]
        st_s[1] = s0_ref[0, 1]

    def cumulate(level):
        for d in range(2):
            b_s[d] = _dot_w2(tri_ref[level, d], _operand(p_refs[d], mode, "la", d, all_rows))

    def run_bounded(blk):
        for g in range(LC // blk):
            _gated_block_bounded(0, g, blk, pf_ref, mode, b_s, o_s, st_s)
            _gated_block_bounded(1, LC // blk - 1 - g, blk, pb_ref, mode, b_s, o_s, st_s)

    def run_exact():
        def body(it, carry):
            _gated_subblock(0, it, pf_ref, mode, ones_ref, mbd_ref, b_s, p_s, o_s, st_s)
            _gated_subblock(1, N_SUB - 1 - it, pb_ref, mode, ones_ref, mbd_ref, b_s, p_s, o_s, st_s)
            return carry

        lax.fori_loop(0, N_SUB, body, 0)

    cumulate(0)
    wide = _centred_span(b_s, BOUNDED_BLKS[0]) < FAST_LOG2_RANGE

    @pl.when(wide)
    def _():
        run_bounded(BOUNDED_BLKS[0])

    @pl.when(jnp.logical_not(wide))
    def _():
        cumulate(1)
        narrow = _centred_span(b_s, BOUNDED_BLKS[1]) < FAST_LOG2_RANGE

        @pl.when(narrow)
        def _():
            run_bounded(BOUNDED_BLKS[1])

        @pl.when(jnp.logical_not(narrow))
        def _():
            cumulate(2)
            run_exact()

    @pl.when(j == nc - 1)
    def _():
        for d in range(2):
            for h in range(N_HEADS):
                blk = pl.ds(h * HEAD_D, HEAD_D)
                st_ref[0, d, h] = st_s[d, blk, blk]

    def finish(o, p_ref):
        ms = _dot_x2(o * o, ones_ref[...]) * (1.0 / HEAD_D)
        return o * lax.rsqrt(ms + EPS) * gain_ref[...] * _operand(p_ref, mode, "gate", 0, all_rows)

    rows_f = pl.ds(pl.multiple_of(j * LC, LC), LC)
    rows_b = pl.ds(pl.multiple_of((nc - 1 - j) * LC, LC), LC)
    if nc == 1:
        y_ref[0] = finish(o_s[0] + o_s[1], pf_ref)
    else:
        @pl.when(j < nc // 2)
        def _():
            y_ref[0, rows_f, :] = o_s[0]
            y_ref[0, rows_b, :] = o_s[1]

        @pl.when(j >= nc // 2)
        def _():
            y_ref[0, rows_f, :] = finish(y_ref[0, rows_f, :] + o_s[0], pf_ref)
            y_ref[0, rows_b, :] = finish(y_ref[0, rows_b, :] + o_s[1], pb_ref)


def _gated_consts():
    r = jnp.arange(LC)

    def tri(block):
        same = (r[:, None] // block) == (r[None, :] // block)
        lower = (same & (r[None, :] <= r[:, None])).astype(BF16)
        upper = (same & (r[None, :] >= r[:, None])).astype(BF16)
        return jnp.stack([lower, upper])

    head = (r[:, None] // HEAD_D) == (r[None, :] // HEAD_D)
    tris = jnp.stack([tri(blk) for blk in BOUNDED_BLKS + (SUB,)])
    return tris, head.astype(BF16), head.astype(F32)


def _gated_mixer(p, gain, s0t, consts, mode):
    b, l, width = p.shape
    nc = l // LC
    assert nc == 1 or nc % 2 == 0
    tri, ones, mbd = consts
    w = GROUP_W
    const2 = lambda shape: pl.BlockSpec(shape, lambda i, j: (0,) * len(shape))
    return pl.pallas_call(
        functools.partial(_gated_kernel, mode=mode, nc=nc),
        out_shape=[jax.ShapeDtypeStruct((b, l, w), F32),
                   jax.ShapeDtypeStruct((b, 2, N_HEADS, HEAD_D, HEAD_D), F32)],
        grid=(b, nc),
        in_specs=[
            pl.BlockSpec((1, LC, width), lambda i, j: (i, j, 0)),
            pl.BlockSpec((1, LC, width), lambda i, j: (i, nc - 1 - j, 0)),
            const2(gain.shape),
            pl.BlockSpec((1, 2, w, w), lambda i, j: (i, 0, 0, 0)),
            const2(tri.shape), const2(ones.shape), const2(mbd.shape),
        ],
        out_specs=[pl.BlockSpec((1, l, w), lambda i, j: (i, 0, 0)),
                   pl.BlockSpec((1, 2, N_HEADS, HEAD_D, HEAD_D), lambda i, j: (i, 0, 0, 0, 0))],
        scratch_shapes=[pltpu.VMEM((2, LC, w), F32), pltpu.VMEM((2, SUB * SUB, w), BF16),
                        pltpu.VMEM((2, LC, w), F32), pltpu.VMEM((2, w, w), F32)],
        compiler_params=_params(("parallel", "arbitrary")),
        name="gated_" + mode,
    )(p, p, gain, s0t, tri, ones, mbd)


def _state_to_blockdiag_t(s):
    b = s.shape[0]
    st = jnp.swapaxes(s, -1, -2)
    eye = jnp.eye(N_HEADS, dtype=s.dtype)
    full = st[:, :, :, :, None, :] * eye[None, None, :, None, :, None]
    return full.reshape(b, 2, GROUP_W, GROUP_W)


def _blockdiag_t_to_state(st):
    return jnp.swapaxes(st, -1, -2)


def _rglru_kernel(x_ref, g_ref, cw_ref, cb_ref, w_ref, bias_ref, nsp_ref, h0_ref, y_ref, hT_ref,
                  xc_s, a_s, u_s, *, l):
    c = RG_HALF
    x = x_ref[0]
    row = lax.broadcasted_iota(jnp.int32, (l, c), 0)
    xc = x * cw_ref[2:3, :] + cb_ref[...]
    xc = xc + jnp.where(row >= 2, pltpu.roll(x, 2, 0), 0.0) * cw_ref[0:1, :]
    xc = xc + jnp.where(row >= 1, pltpu.roll(x, 1, 0), 0.0) * cw_ref[1:2, :]
    xc = xc + jnp.where(row <= l - 2, pltpu.roll(x, l - 1, 0), 0.0) * cw_ref[3:4, :]
    xc_s[...] = xc

    nslab = l // RG_ROWS
    grp = (RG_ROWS // SUBLANE, SUBLANE, c)
    sub = lax.broadcasted_iota(jnp.int32, grp, 1)

    def slab(n, carry):
        rows = pl.ds(pl.multiple_of(n * RG_ROWS, RG_ROWS), RG_ROWS)
        xs = xc_s[rows, :]
        gates = _sigmoid(_dot(xs.astype(BF16), w_ref[...]) + bias_ref[...])
        for d in range(2):
            r = gates[:, (2 * d) * c:(2 * d + 1) * c]
            i = gates[:, (2 * d + 1) * c:(2 * d + 2) * c]
            log_a = r * nsp_ref[d:d + 1, :]
            a_flat = jnp.exp(log_a)
            a = a_flat.reshape(grp)
            u = (jnp.sqrt(1.0 - a_flat * a_flat) * (i * xs)).reshape(grp)
            for sft in (1, 2, 4):
                if d == 0:
                    ok = sub >= sft
                    a_n, u_n = pltpu.roll(a, sft, 1), pltpu.roll(u, sft, 1)
                else:
                    ok = sub <= SUBLANE - 1 - sft
                    a_n, u_n = pltpu.roll(a, SUBLANE - sft, 1), pltpu.roll(u, SUBLANE - sft, 1)
                u = jnp.where(ok, a * u_n + u, u)
                a = jnp.where(ok, a * a_n, a)
            a_s[d, rows, :] = a.reshape(RG_ROWS, c)
            u_s[d, rows, :] = u.reshape(RG_ROWS, c)
        return carry

    lax.fori_loop(0, nslab, slab, 0)

    ngrp = l // SUBLANE

    def carry_step(n, hs):
        hf, hb = hs
        rows_f = pl.ds(pl.multiple_of(n * SUBLANE, SUBLANE), SUBLANE)
        rows_b = pl.ds(pl.multiple_of((ngrp - 1 - n) * SUBLANE, SUBLANE), SUBLANE)
        hh_f = a_s[0, rows_f, :] * hf + u_s[0, rows_f, :]
        hh_b = a_s[1, rows_b, :] * hb + u_s[1, rows_b, :]
        u_s[0, rows_f, :] = hh_f
        u_s[1, rows_b, :] = hh_b
        return (jnp.broadcast_to(hh_f[SUBLANE - 1:SUBLANE, :], (SUBLANE, c)),
                jnp.broadcast_to(hh_b[0:1, :], (SUBLANE, c)))

    h_init = (jnp.broadcast_to(h0_ref[0, 0:1, :], (SUBLANE, c)), jnp.broadcast_to(h0_ref[0, 1:2, :], (SUBLANE, c)))
    hf, hb = lax.fori_loop(0, ngrp, carry_step, h_init, unroll=2)
    hT_ref[0, 0:1, :] = hf[0:1, :]
    hT_ref[0, 1:2, :] = hb[0:1, :]

    g = g_ref[0]
    gelu = 0.5 * g * (1.0 + jnp.tanh(math.sqrt(2.0 / math.pi) * (g + 0.044715 * (g * g * g))))
    y_ref[0] = (u_s[0] + u_s[1]) * gelu


def _rglru(p, cw, cb, wbd, bias, nsp, h0):
    b, l, _ = p.shape
    c = RG_HALF
    half = lambda shape: pl.BlockSpec(shape, lambda i, j: (0,) * (len(shape) - 1) + (j,))
    return pl.pallas_call(
        functools.partial(_rglru_kernel, l=l),
        out_shape=[jax.ShapeDtypeStruct((b, l, GROUP_W), F32), jax.ShapeDtypeStruct((b, 2, GROUP_W), F32)],
        grid=(b, 2),
        in_specs=[
            pl.BlockSpec((1, l, c), lambda i, j: (i, 0, j)),
            pl.BlockSpec((1, l, c), lambda i, j: (i, 0, 2 + j)),
            half((RG_CONV_W, c)), half((1, c)),
            pl.BlockSpec((None, c, 4 * c), lambda i, j: (j, 0, 0)),
            pl.BlockSpec((None, 1, 4 * c), lambda i, j: (j, 0, 0)),
            half((2, c)),
            pl.BlockSpec((1, 2, c), lambda i, j: (i, 0, j)),
        ],
        out_specs=[pl.BlockSpec((1, l, c), lambda i, j: (i, 0, j)),
                   pl.BlockSpec((1, 2, c), lambda i, j: (i, 0, j))],
        scratch_shapes=[pltpu.VMEM((l, c), F32), pltpu.VMEM((2, l, c), F32), pltpu.VMEM((2, l, c), F32)],
        compiler_params=_params(("parallel", "parallel")),
        name="rglru",
    )(p, p, cw, cb, wbd, bias, nsp, h0)


def _short_conv3(x, w_ref, b_ref):
    l = x.shape[0]
    row = lax.broadcasted_iota(jnp.int32, x.shape, 0)
    o = x * w_ref[1:2, :] + b_ref[...]
    o = o + jnp.where(row >= 1, pltpu.roll(x, 1, 0), 0.0) * w_ref[0:1, :]
    return o + jnp.where(row <= l - 2, pltpu.roll(x, l - 1, 0), 0.0) * w_ref[2:3, :]


def _hy_filter_kernel(pe_ref, w1_ref, b1_ref, w2_ref, b2_ref, w3_ref, dec_ref, o_ref):
    pe = pe_ref[...]
    h = jnp.sin(_dot3(pe, w1_ref[...]) + b1_ref[...])
    h = jnp.sin(_dot3(h, w2_ref[...]) + b2_ref[...])
    h = _dot3(h, w3_ref[...])
    dist = pe[:, LANE - 1:LANE]
    h = h * jnp.exp(-dist * dec_ref[...])
    o_ref[...] = h / jnp.sum(jnp.abs(h), axis=0, keepdims=True)


def _hy_filters(pe, w1, b1, w2, b2, w3, decay):
    l = pe.shape[0]
    n = w3.shape[1]
    args = (pe, w1, b1, w2, b2, w3, decay)
    return pl.pallas_call(
        _hy_filter_kernel,
        out_shape=jax.ShapeDtypeStruct((l, n), F32),
        grid=(1,),
        in_specs=[pl.BlockSpec(a.shape, lambda i: (0, 0)) for a in args],
        out_specs=pl.BlockSpec((l, n), lambda i: (0, 0)),
        compiler_params=_params(("arbitrary",)),
        name="hy_filters",
    )(*args)


def _hy_filter_dft_kernel(h_ref, ct_ref, st_ref, hc_ref, hs_ref, ht_s):
    @pl.when(pl.program_id(0) == 0)
    def _():
        ht_s[...] = h_ref[...].T.astype(BF16)

    hc_ref[...] = _dot(ht_s[...], ct_ref[...])
    hs_ref[...] = _dot(ht_s[...], st_ref[...])


def _hy_filter_dft(h, tables):
    l, n = h.shape
    ct, st, _, _ = tables
    nk = ct.shape[0]
    spec = jax.ShapeDtypeStruct((n, nk * FREQ_TILE), F32)
    return pl.pallas_call(
        _hy_filter_dft_kernel,
        out_shape=[spec, spec],
        grid=(nk,),
        in_specs=[pl.BlockSpec((l, n), lambda k: (0, 0)),
                  pl.BlockSpec((None, l, FREQ_TILE), lambda k: (k, 0, 0)),
                  pl.BlockSpec((None, l, FREQ_TILE), lambda k: (k, 0, 0))],
        out_specs=[pl.BlockSpec((n, FREQ_TILE), lambda k: (0, k)), pl.BlockSpec((n, FREQ_TILE), lambda k: (0, k))],
        scratch_shapes=[pltpu.VMEM((n, l), BF16)],
        compiler_params=_params(("arbitrary",)),
        name="hy_filter_dft",
    )(h, ct, st)


def _hy_conv_kernel(z_ref, x_ref, hc_ref, hs_ref, skip_ref, wz_ref, bz_ref, wx_ref, bx_ref,
                    ct_ref, st_ref, wc_ref, ws_ref, o_ref, zt_s, *, bb, nk, z_raw):
    kt = pl.program_id(1)
    c = GROUP_W

    def z_of(b):
        return _short_conv3(z_ref[b], wz_ref, bz_ref) if z_raw else z_ref[b]

    @pl.when(kt == 0)
    def _():
        o_ref[...] = jnp.zeros_like(o_ref)
        for b in range(bb):
            zt_s[b * c:(b + 1) * c, :] = z_of(b).T.astype(BF16)

    zt = zt_s[...]
    xc = _dot(zt, ct_ref[...])
    xs = _dot(zt, st_ref[...])
    hc = hc_ref[...]
    hs = hs_ref[...]
    wc = wc_ref[...]
    ws = ws_ref[...]
    for b in range(bb):
        zc = xc[b * c:(b + 1) * c, :]
        zsn = xs[b * c:(b + 1) * c, :]
        yc = zc * hc - zsn * hs
        ys = zc * hs + zsn * hc
        a = (wc * yc + ws * ys).T.astype(BF16)
        bm = (wc * ys - ws * yc).T.astype(BF16)
        o_ref[b] += _dot(ct_ref[...], a) + _dot(st_ref[...], bm)

    @pl.when(kt == nk - 1)
    def _():
        for b in range(bb):
            gate = _short_conv3(x_ref[b], wx_ref, bx_ref)
            o_ref[b] = gate * (o_ref[b] + skip_ref[...] * z_of(b))


def _hy_long_conv(pc, zprev, z_col, gate_col, cw, cb, hspec, h_col, skip, skip_col, tables, bb):
    b, l, _ = pc.shape
    c = GROUP_W
    ct, st, wc, ws = tables
    nk = ct.shape[0]
    z_raw = zprev is None
    z_arr = pc if z_raw else zprev
    return pl.pallas_call(
        functools.partial(_hy_conv_kernel, bb=bb, nk=nk, z_raw=z_raw),
        out_shape=jax.ShapeDtypeStruct((b, l, c), F32),
        grid=(b // bb, nk),
        in_specs=[
            pl.BlockSpec((bb, l, c), lambda g, k: (g, 0, z_col)),
            pl.BlockSpec((bb, l, c), lambda g, k: (g, 0, gate_col)),
            pl.BlockSpec((c, FREQ_TILE), lambda g, k: (h_col, k)),
            pl.BlockSpec((c, FREQ_TILE), lambda g, k: (h_col, k)),
            pl.BlockSpec((1, c), lambda g, k: (0, skip_col)),
            pl.BlockSpec((3, c), lambda g, k: (0, z_col)),
            pl.BlockSpec((1, c), lambda g, k: (0, z_col)),
            pl.BlockSpec((3, c), lambda g, k: (0, gate_col)),
            pl.BlockSpec((1, c), lambda g, k: (0, gate_col)),
            pl.BlockSpec((None, l, FREQ_TILE), lambda g, k: (k, 0, 0)),
            pl.BlockSpec((None, l, FREQ_TILE), lambda g, k: (k, 0, 0)),
            pl.BlockSpec((1, FREQ_TILE), lambda g, k: (0, k)),
            pl.BlockSpec((1, FREQ_TILE), lambda g, k: (0, k)),
        ],
        out_specs=pl.BlockSpec((bb, l, c), lambda g, k: (g, 0, 0)),
        scratch_shapes=[pltpu.VMEM((bb * c, l), BF16)],
        compiler_params=_params(("parallel", "arbitrary")),
        name="hy_long_conv",
    )(z_arr, pc, hspec[0], hspec[1], skip, cw, cb, cw, cb, ct, st, wc, ws)


def _dft_tables(l):
    n = 3 * l // 2
    nf = n // 2 + 1
    nfp = -(-nf // FREQ_TILE) * FREQ_TILE
    k = jnp.arange(nfp, dtype=jnp.int32)
    ta = jnp.arange(l // TWID, dtype=jnp.int32) * TWID
    tb = jnp.arange(TWID, dtype=jnp.int32)
    live = (k < nf)

    def cos_sin(m):
        ang = (m % n).astype(F32) * (2.0 * math.pi / n)
        return jnp.cos(ang), jnp.sin(ang)

    def tiles(x):
        return x.reshape(x.shape[0], nfp // FREQ_TILE, FREQ_TILE).transpose(1, 0, 2)

    ca, sa = cos_sin(ta[:, None] * k[None, :])
    cb, sb = cos_sin(tb[:, None] * k[None, :])
    ca, sa = (tiles(jnp.where(live[None, :], x, 0.0))[:, :, None, :] for x in (ca, sa))
    cb, sb = tiles(cb)[:, None, :, :], tiles(sb)[:, None, :, :]
    shape = (nfp // FREQ_TILE, l, FREQ_TILE)
    ct = (ca * cb - sa * sb).reshape(shape).astype(BF16)
    st = (sa * cb + ca * sb).reshape(shape).astype(BF16)
    wk = jnp.where((k == 0) | (k == n // 2), 1.0, 2.0) / n
    wk = jnp.where(live, wk, 0.0).astype(F32)
    cp, sp = cos_sin(k * (l // 2))
    return ct, st, (wk * cp)[None, :], (wk * sp)[None, :]


def _hy_pos_features(l):
    pos = jnp.arange(l, dtype=F32)
    t = pos / (l - 1)
    ang = (2.0 * jnp.pi * pos / l)[:, None] * jnp.linspace(1e-4, HY_POS_BANDS - 1, HY_POS_BANDS, dtype=F32)[None, :]
    half = l // 2
    dist = jnp.abs(pos - half) / half
    pe = jnp.concatenate([t[:, None], jnp.cos(ang), -jnp.sin(ang)], axis=-1)
    pad = jnp.zeros((l, LANE - 1 - pe.shape[1]), F32)
    return jnp.concatenate([pe, pad, dist[:, None]], axis=-1)


def _grid_position_encoding(n_pos, dim):
    quarter = dim // 4
    omega = 1.0 / (POS_BASE ** (jnp.arange(quarter, dtype=F32) / quarter))
    ang = jnp.arange(n_pos, dtype=F32)[:, None] * omega[None, :]
    return jnp.concatenate([jnp.sin(ang), jnp.cos(ang)], axis=-1)


def _blockdiag2(a, b):
    z = jnp.zeros_like(a)
    return jnp.concatenate([jnp.concatenate([a, z], axis=1), jnp.concatenate([z, b], axis=1)], axis=0)


def _layer_params(l, hg_lb, norm1_g, norm2_g, w_in, w_out, gla_w_gate, gla_b_gate, gla_norm_g,
                  rg_conv_w, rg_conv_b, rg_w_a, rg_b_a, rg_w_x, rg_b_x, rg_lambda,
                  hy_conv_w, hy_conv_b, hy_w1, hy_b1, hy_w2, hy_b2, hy_w3, hy_decay, hy_skip,
                  hg_norm_g, ffn_w1, ffn_w3, ffn_w2):
    d, w = D_MODEL, GROUP_W
    n_gla = 4 * w + GLA_LOWRANK
    wi = w_in[l]
    w_proj = jnp.concatenate([wi[:, :n_gla], jnp.zeros((d, LR_PAD - GLA_LOWRANK), F32), wi[:, n_gla:]], axis=1)
    wg = jnp.concatenate([gla_w_gate[l], jnp.zeros((2, LR_PAD - GLA_LOWRANK, w), F32)], axis=1)
    zrow = jnp.zeros((w,), F32)
    par_gla = jnp.stack([gla_b_gate[l, 0], gla_b_gate[l, 1], zrow, gla_norm_g[l]] + [zrow] * 4)
    lb = hg_lb[l]
    par_hg = jnp.stack([1.0 - lb, jnp.log(lb), jnp.log1p(-lb), hg_norm_g[l]] + [zrow] * 4)
    wa, wx = rg_w_a[l], rg_w_x[l]
    rg_w = jnp.stack([
        jnp.concatenate([_blockdiag2(m[dd, 2 * j], m[dd, 2 * j + 1]) for dd in range(2) for m in (wa, wx)], axis=1)
        for j in range(2)])
    ba, bx = rg_b_a[l], rg_b_x[l]
    rg_bias = jnp.stack([
        jnp.concatenate([v[dd, j * RG_HALF:(j + 1) * RG_HALF] for dd in range(2) for v in (ba, bx)])[None, :]
        for j in range(2)])
    nsp = -RG_C * jax.nn.softplus(-rg_lambda[l])
    w1p = jnp.concatenate([hy_w1[l], jnp.zeros((LANE - hy_w1.shape[1], HY_FFN_W), F32)], axis=0)
    return dict(
        norm1=norm1_g[l][None, :], norm2=norm2_g[l][None, :],
        w_proj=w_proj.astype(BF16), w_out=w_out[l].astype(BF16),
        wg=wg.astype(BF16), par_gla=par_gla, par_hg=par_hg,
        rg_cw=rg_conv_w[l], rg_cb=rg_conv_b[l][None, :], rg_w=rg_w.astype(BF16), rg_bias=rg_bias, rg_nsp=nsp,
        hy_cw=hy_conv_w[l], hy_cb=hy_conv_b[l][None, :], hy_w1=w1p, hy_b1=hy_b1[l][None, :],
        hy_w2=hy_w2[l], hy_b2=hy_b2[l][None, :], hy_w3=hy_w3[l], hy_decay=hy_decay[l][None, :],
        hy_skip=hy_skip[l][None, :],
        w1=ffn_w1[l].astype(BF16), w3=ffn_w3[l].astype(BF16), w2=ffn_w2[l].astype(BF16))


def _trunk_layer(x, p, mod4, layer, row0, seq_shape, s_gla, s_rg, s_hg, stream_consts, final, final_g):
    bm, lm, d = x.shape
    b, l = seq_shape
    gated_consts, pe, tables, bb = stream_consts
    pa, pb, pc, pd = _norm_proj(x, mod4, layer, row0, p["norm1"], p["w_proj"], p["wg"], p["par_gla"], p["par_hg"])
    pa, pb, pc, pd = (t.reshape(b, l, t.shape[-1]) for t in (pa, pb, pc, pd))

    ya, st_a = _gated_mixer(pa, p["par_gla"][3:4], _state_to_blockdiag_t(s_gla), gated_consts, "gla")
    yd, st_d = _gated_mixer(pd, p["par_hg"][3:4], _state_to_blockdiag_t(s_hg), gated_consts, "hg")
    yb, st_b = _rglru(pb, p["rg_cw"], p["rg_cb"], p["rg_w"], p["rg_bias"], p["rg_nsp"], s_rg)

    filt = _hy_filters(pe, p["hy_w1"], p["hy_b1"], p["hy_w2"], p["hy_b2"], p["hy_w3"], p["hy_decay"])
    hspec = _hy_filter_dft(filt, tables)
    z1 = _hy_long_conv(pc, None, 0, 1, p["hy_cw"], p["hy_cb"], hspec, 0, p["hy_skip"], 0, tables, bb)
    yc = _hy_long_conv(pc, z1, 0, 2, p["hy_cw"], p["hy_cb"], hspec, 1, p["hy_skip"], 1, tables, bb)

    ys = [t.reshape(bm, lm, GROUP_W) for t in (ya, yb, yc, yd)]
    x = _out_ffn(x, ys, mod4, layer, row0, p["norm2"], final_g, p["w_out"], p["w1"], p["w3"], p["w2"], final)
    return x, (_blockdiag_t_to_state(st_a), st_b, _blockdiag_t_to_state(st_d))


def kernel(x_prompt, x_sample, state_gla, state_rglru, state_hgrn, c, c_ctx, norm1_g, norm2_g, final_norm_g, w_mod, b_mod, w_in, w_out, gla_w_gate, gla_b_gate, gla_norm_g, rg_conv_w, rg_conv_b, rg_w_a, rg_b_a, rg_w_x, rg_b_x, rg_lambda, hy_conv_w, hy_conv_b, hy_w1, hy_b1, hy_w2, hy_b2, hy_w3, hy_decay, hy_skip, hg_lower, hg_norm_g, ffn_w1, ffn_w3, ffn_w2):
    depth = w_in.shape[0]
    nb, seq, d = x_prompt.shape
    db, dseq, _ = x_sample.shape

    hg_lb = jnp.cumsum(jax.nn.softmax(hg_lower.astype(F32), axis=0), axis=0)
    hg_lb = hg_lb - hg_lb[0:1]

    cvec = jnp.concatenate([c_ctx[None, :], c, jnp.zeros((SUBLANE - 1 - db, d), F32)], axis=0)
    mod4 = _modulation(cvec, w_mod, b_mod).reshape(depth, SUBLANE, 1, N_MOD * d)

    gated_consts = _gated_consts()
    consts_p = (gated_consts, _hy_pos_features(seq), _dft_tables(seq), 8)
    consts_s = (gated_consts, _hy_pos_features(dseq), _dft_tables(dseq), 1)

    xp = x_prompt.reshape(1, nb * seq, d)
    xs = _add_pos(x_sample, _grid_position_encoding(max(dseq // GRID_W, GRID_W), d))
    zero_gla = jnp.zeros((nb, 2, N_HEADS, HEAD_D, HEAD_D), F32)
    zero_rg = jnp.zeros((nb, 2, GROUP_W), F32)
    final_g = final_norm_g[None, :]

    gla_states, rg_states, hg_states = [], [], []
    for l in range(depth):
        p = _layer_params(l, hg_lb, norm1_g, norm2_g, w_in, w_out, gla_w_gate, gla_b_gate, gla_norm_g,
                          rg_conv_w, rg_conv_b, rg_w_a, rg_b_a, rg_w_x, rg_b_x, rg_lambda,
                          hy_conv_w, hy_conv_b, hy_w1, hy_b1, hy_w2, hy_b2, hy_w3, hy_decay, hy_skip,
                          hg_norm_g, ffn_w1, ffn_w3, ffn_w2)
        final = l == depth - 1
        xp, (sg, sr, sh) = _trunk_layer(xp, p, mod4, l, 0, (nb, seq), zero_gla, zero_rg, zero_gla,
                                        consts_p, final, final_g)
        xs, _ = _trunk_layer(xs, p, mod4, l, 1, (db, dseq), state_gla[:, l], state_rglru[:, l],
                             state_hgrn[:, l], consts_s, final, final_g)
        gla_states.append(sg)
        rg_states.append(sr)
        hg_states.append(sh)

    return (xp.reshape(nb, seq, d), xs,
            jnp.stack(gla_states, axis=1), jnp.stack(rg_states, axis=1), jnp.stack(hg_states, axis=1))
```

```python
import functools
import math

import jax
import jax.numpy as jnp
from jax import lax
from jax.experimental import pallas as pl
from jax.experimental.pallas import tpu as pltpu

F32 = jnp.float32
BF16 = jnp.bfloat16

D_MODEL = 1024
N_MOD = 6
GROUP_W = 256
N_HEADS = 4
HEAD_D = GROUP_W // N_HEADS
GLA_LOWRANK = 16
GLA_GATE_TEMP = 16.0
RG_C = 8.0
RG_CONV_W = 4
HY_POS_BANDS = 16
HY_FFN_W = 64
D_FF = 2816
EPS = 1e-6
GRID_W = 64
POS_BASE = 10000.0

LANE = 128
SUBLANE = 8
VMEM_LIMIT = 56 * 1024 * 1024

LOG2E = 1.0 / math.log(2.0)
FAST_LOG2_RANGE = 96.0
SUB = 16
LC = 256
N_SUB = LC // SUB
BOUNDED_BLKS = (64, 32)
LR_PAD = LANE
W_GLA = 4 * GROUP_W + LR_PAD
W_RG = 2 * GROUP_W
W_HY = 3 * GROUP_W
W_HG = 5 * GROUP_W
W_PROJ = W_GLA + W_RG + W_HY + W_HG
W_GLA_MM = 4 * GROUP_W
W_HG_MM = 5 * GROUP_W
W_DECAY = 2 * GROUP_W
GATED_COLS = {"gla": dict(q=0, k=(1, 1), v=2, gate=3, la=(0, 1)),
              "hg": dict(q=0, k=(1, 2), v=3, gate=4, la=(0, 1))}
TM = 512
TF = D_FF // 2
TN_MOD = 512
RG_HALF = GROUP_W // 2
RG_ROWS = 256
FREQ_TILE = 256
TWID = 64


def _dot(a, b):
    return jnp.dot(a, b, preferred_element_type=F32)


def _split(x):
    hi = x.astype(BF16)
    lo = (x - hi.astype(F32)).astype(BF16)
    return hi, lo


def _dot_x2(x, w):
    hi, lo = _split(x)
    return _dot(hi, w) + _dot(lo, w)


def _dot_w2(w, x):
    hi, lo = _split(x)
    return _dot(w, hi) + _dot(w, lo)


def _dot3(a, b):
    ah, al = _split(a)
    bh, bl = _split(b)
    return _dot(ah, bh) + _dot(al, bh) + _dot(ah, bl)


def _sigmoid(x):
    return 0.5 * jnp.tanh(0.5 * x) + 0.5


def _silu(x):
    return x * _sigmoid(x)


def _log1p_exp_neg_abs(x):
    return jnp.log(1.0 + jnp.exp(-jnp.abs(x)))


def _log_sigmoid(x):
    return jnp.minimum(x, 0.0) - _log1p_exp_neg_abs(x)


def _rms(x):
    return x * lax.rsqrt(jnp.mean(x * x, axis=-1, keepdims=True) + EPS)


def _params(sem, vmem=VMEM_LIMIT):
    return pltpu.CompilerParams(dimension_semantics=sem, vmem_limit_bytes=vmem)


def _mod_kernel(c_ref, w_ref, b_ref, o_ref):
    c = c_ref[...]
    o_ref[0] = _dot3(_silu(c), w_ref[0]) + b_ref[0]


def _modulation(cvec, w_mod, b_mod):
    depth = w_mod.shape[0]
    n = N_MOD * D_MODEL
    return pl.pallas_call(
        _mod_kernel,
        out_shape=jax.ShapeDtypeStruct((depth, SUBLANE, n), F32),
        grid=(depth, n // TN_MOD),
        in_specs=[
            pl.BlockSpec((SUBLANE, D_MODEL), lambda l, j: (0, 0)),
            pl.BlockSpec((1, D_MODEL, TN_MOD), lambda l, j: (l, 0, j)),
            pl.BlockSpec((1, 1, TN_MOD), lambda l, j: (l, 0, j)),
        ],
        out_specs=pl.BlockSpec((1, SUBLANE, TN_MOD), lambda l, j: (l, 0, j)),
        compiler_params=_params(("parallel", "parallel")),
        name="modulation",
    )(cvec, w_mod, b_mod.reshape(depth, 1, n))


def _add_kernel(x_ref, e_ref, o_ref):
    j = pl.program_id(1)
    half = D_MODEL // 2
    enc = e_ref[0:GRID_W, :]
    for r in range(TM // GRID_W):
        rows = slice(r * GRID_W, (r + 1) * GRID_W)
        enc_row = e_ref[pl.ds(j * (TM // GRID_W) + r, 1), :]
        o_ref[0, rows, 0:half] = x_ref[0, rows, 0:half] + enc_row
        o_ref[0, rows, half:] = x_ref[0, rows, half:] + enc


def _add_pos(x, enc):
    b, l, d = x.shape
    assert l // GRID_W <= enc.shape[0]
    return pl.pallas_call(
        _add_kernel,
        out_shape=jax.ShapeDtypeStruct(x.shape, F32),
        grid=(b, l // TM),
        in_specs=[pl.BlockSpec((1, TM, d), lambda i, j: (i, j, 0)),
                  pl.BlockSpec(enc.shape, lambda i, j: (0, 0))],
        out_specs=pl.BlockSpec((1, TM, d), lambda i, j: (i, j, 0)),
        compiler_params=_params(("parallel", "parallel")),
        name="add_pos",
    )(x, enc)


def _gla_features(p, wg, par):
    w = GROUP_W
    lr = p[:, 4 * w:4 * w + LR_PAD].astype(BF16)
    las = [_log_sigmoid(_dot(lr, wg[d]) + par[d:d + 1, :]) * (LOG2E / GLA_GATE_TEMP) for d in range(2)]
    return [p[:, 0:w] * (HEAD_D ** -0.5), p[:, w:2 * w], p[:, 2 * w:3 * w], _silu(p[:, 3 * w:4 * w])] + las


def _hg_features(p, par):
    w = GROUP_W
    ks, las = [], []
    for d in range(2):
        f = p[:, (1 + d) * w:(2 + d) * w]
        e = jnp.exp(-jnp.abs(f))
        ks.append(par[0:1, :] * (jnp.where(f >= 0.0, e, 1.0) / (1.0 + e)))
        y = par[2:3, :] + (jnp.minimum(f, 0.0) - jnp.log(1.0 + e))
        lb = par[1:2, :]
        las.append((jnp.maximum(lb, y) + _log1p_exp_neg_abs(lb - y)) * LOG2E)
    return [_silu(p[:, 0:w]), ks[0], ks[1], p[:, 3 * w:4 * w], _silu(p[:, 4 * w:5 * w]), las[0], las[1]]


def _proj_kernel(x_ref, mod_ref, g_ref, w_ref, wg_ref, pg_ref, ph_ref,
                 oa_ref, oal_ref, ob_ref, oc_ref, od_ref, odl_ref):
    x = x_ref[0]
    m = mod_ref[0, 0]
    sh = m[:, 0:D_MODEL]
    sc = m[:, D_MODEL:2 * D_MODEL]
    u = _rms(x) * g_ref[...] * (1.0 + sc) + sh
    p = _dot(u.astype(BF16), w_ref[...])
    w = GROUP_W

    def put(feats, mm_ref, decay_ref):
        n_mm = len(feats) - 2
        for i, t in enumerate(feats[:n_mm]):
            mm_ref[0, :, i * w:(i + 1) * w] = t.astype(BF16)
        for i, t in enumerate(feats[n_mm:]):
            decay_ref[0, :, i * w:(i + 1) * w] = t

    put(_gla_features(p[:, 0:W_GLA], wg_ref, pg_ref), oa_ref, oal_ref)
    ob_ref[0] = p[:, W_GLA:W_GLA + W_RG]
    oc_ref[0] = p[:, W_GLA + W_RG:W_GLA + W_RG + W_HY]
    put(_hg_features(p[:, W_GLA + W_RG + W_HY:W_PROJ], ph_ref), od_ref, odl_ref)


def _norm_proj(x, mod4, layer, row0, gain, w, wg, par_gla, par_hg):
    bm, lm, d = x.shape
    widths = (W_GLA_MM, W_DECAY, W_RG, W_HY, W_HG_MM, W_DECAY)
    dtypes = (BF16, F32, F32, F32, BF16, F32)
    const = lambda a: pl.BlockSpec(a.shape, lambda i, j: (0,) * a.ndim)
    return pl.pallas_call(
        _proj_kernel,
        out_shape=[jax.ShapeDtypeStruct((bm, lm, wd), dt) for wd, dt in zip(widths, dtypes)],
        grid=(bm, lm // TM),
        in_specs=[
            pl.BlockSpec((1, TM, d), lambda i, j: (i, j, 0)),
            pl.BlockSpec((1, 1, 1, N_MOD * d), lambda i, j: (layer, row0 + i, 0, 0)),
            pl.BlockSpec((1, d), lambda i, j: (0, 0)),
            pl.BlockSpec((d, W_PROJ), lambda i, j: (0, 0)),
            const(wg), const(par_gla), const(par_hg),
        ],
        out_specs=[pl.BlockSpec((1, TM, wd), lambda i, j: (i, j, 0)) for wd in widths],
        compiler_params=_params(("parallel", "parallel")),
        name="norm_proj",
    )(x, mod4, gain, w, wg, par_gla, par_hg)


def _ffn_kernel(x_ref, ya_ref, yb_ref, yc_ref, yd_ref, mod_ref, g2_ref, gf_ref,
                wo_ref, w1_ref, w3_ref, w2_ref, o_ref, x1_s, u_s, acc_s, *, nf, final):
    f = pl.program_id(2)
    d = D_MODEL

    @pl.when(f == 0)
    def _():
        m = mod_ref[0, 0]
        g1 = m[:, 2 * d:3 * d]
        sh2 = m[:, 3 * d:4 * d]
        sc2 = m[:, 4 * d:5 * d]
        y = jnp.concatenate([ya_ref[0], yb_ref[0], yc_ref[0], yd_ref[0]], axis=-1)
        x1 = x_ref[0] + g1 * _dot(y.astype(BF16), wo_ref[...])
        x1_s[...] = x1
        u_s[...] = (_rms(x1) * g2_ref[...] * (1.0 + sc2) + sh2).astype(BF16)
        acc_s[...] = jnp.zeros_like(acc_s)

    u = u_s[...]
    h = _silu(_dot(u, w1_ref[...])) * _dot(u, w3_ref[...])
    acc_s[...] += _dot(h.astype(BF16), w2_ref[...])

    @pl.when(f == nf - 1)
    def _():
        g2 = mod_ref[0, 0][:, 5 * d:6 * d]
        xo = x1_s[...] + g2 * acc_s[...]
        if final:
            xo = _rms(xo) * gf_ref[...]
        o_ref[0] = xo


def _out_ffn(x, ys, mod4, layer, row0, g2, gf, wo, w1, w3, w2, final):
    bm, lm, d = x.shape
    nf = D_FF // TF
    tok = lambda wd: pl.BlockSpec((1, TM, wd), lambda i, j, f: (i, j, 0))
    return pl.pallas_call(
        functools.partial(_ffn_kernel, nf=nf, final=final),
        out_shape=jax.ShapeDtypeStruct(x.shape, F32),
        grid=(bm, lm // TM, nf),
        in_specs=[
            tok(d), tok(GROUP_W), tok(GROUP_W), tok(GROUP_W), tok(GROUP_W),
            pl.BlockSpec((1, 1, 1, N_MOD * d), lambda i, j, f: (layer, row0 + i, 0, 0)),
            pl.BlockSpec((1, d), lambda i, j, f: (0, 0)),
            pl.BlockSpec((1, d), lambda i, j, f: (0, 0)),
            pl.BlockSpec((d, d), lambda i, j, f: (0, 0)),
            pl.BlockSpec((d, TF), lambda i, j, f: (0, f)),
            pl.BlockSpec((d, TF), lambda i, j, f: (0, f)),
            pl.BlockSpec((TF, d), lambda i, j, f: (f, 0)),
        ],
        out_specs=tok(d),
        scratch_shapes=[pltpu.VMEM((TM, d), F32), pltpu.VMEM((TM, d), BF16), pltpu.VMEM((TM, d), F32)],
        compiler_params=_params(("parallel", "parallel", "arbitrary")),
        name="out_ffn",
    )(x, *ys, mod4, g2, gf, wo, w1, w3, w2)


def _operand(p_ref, mode, name, d, rows):
    col = GATED_COLS[mode][name]
    col = col[d] if isinstance(col, tuple) else col
    ref = p_ref[1] if name == "la" else p_ref[0]
    return ref[0, rows, col * GROUP_W:(col + 1) * GROUP_W].astype(F32)


def _gated_subblock(d, i, p_ref, mode, ones_ref, mbd_ref, b_s, p_s, o_s, st_s):
    w = GROUP_W
    rev = d == 1
    tio = lax.broadcasted_iota(jnp.int32, (SUB, w), 0)
    edge = 0 if rev else SUB - 1
    rows = pl.ds(pl.multiple_of(i * SUB, SUB), SUB)
    bb = b_s[d, rows, :]
    qb = _operand(p_ref, mode, "q", d, rows)
    kb = _operand(p_ref, mode, "k", d, rows)
    vb = _operand(p_ref, mode, "v", d, rows)

    def row(a, s):
        return jnp.broadcast_to(a[s:s + 1, :], (SUB, w))

    for s in range(SUB):
        valid = (tio <= s) if rev else (tio >= s)
        e = jnp.exp2(jnp.where(valid, bb - row(bb, s), -jnp.inf))
        p_s[d, s * SUB:(s + 1) * SUB, :] = (e * qb * row(kb, s)).astype(BF16)
    r = _dot(p_s[d], ones_ref[...])
    od = jnp.zeros((SUB, w), F32)
    for s in range(SUB):
        od = od + r[s * SUB:(s + 1) * SUB, :] * row(vb, s)
    bend = row(bb, edge)
    st = st_s[d]
    qt = (qb * jnp.exp2(bb)).astype(BF16)
    oi = lax.dot_general(qt, st.astype(BF16), (((1,), (1,)), ((), ())), preferred_element_type=F32)
    kt = (kb * jnp.exp2(bend - bb)).astype(BF16)
    kv = lax.dot_general(vb.astype(BF16), kt, (((0,), (0,)), ((), ())), preferred_element_type=F32)
    st_s[d] = st * jnp.exp2(bend[0:1, :]) + kv * mbd_ref[...]
    o_s[d, rows, :] = od + oi


def _stack_heads(x):
    lane_head = lax.broadcasted_iota(jnp.int32, x.shape, 1) // HEAD_D
    return jnp.concatenate([jnp.where(lane_head == h, x, 0.0) for h in range(N_HEADS)], axis=0).astype(BF16)


def _centred_span(b_s, blk):
    span = None
    for d in range(2):
        for g in range(LC // blk):
            r0, rm, r1 = g * blk, g * blk + blk // 2, (g + 1) * blk - 1
            mid = b_s[d, rm:rm + 1, :]
            m = jnp.maximum(jnp.abs(b_s[d, r0:r0 + 1, :] - mid), jnp.abs(b_s[d, r1:r1 + 1, :] - mid))
            span = m if span is None else jnp.maximum(span, m)
    return jnp.max(span)


def _gated_block_bounded(d, g, blk, p_ref, mode, b_s, o_s, st_s):
    w = GROUP_W
    rev = d == 1
    edge = 0 if rev else blk - 1
    rows = slice(g * blk, (g + 1) * blk)
    bb = b_s[d, rows, :]
    qb = _operand(p_ref, mode, "q", d, rows)
    kb = _operand(p_ref, mode, "k", d, rows)
    cc = bb - jnp.broadcast_to(bb[blk // 2:blk // 2 + 1, :], bb.shape)
    vm = _stack_heads(_operand(p_ref, mode, "v", d, rows))
    km = _stack_heads(kb * jnp.exp2(-cc))
    sc = lax.dot_general((qb * jnp.exp2(cc)).astype(BF16), km, (((1,), (1,)), ((), ())),
                         preferred_element_type=F32)
    t_i = lax.broadcasted_iota(jnp.int32, sc.shape, 0)
    s_i = lax.broadcasted_iota(jnp.int32, sc.shape, 1) % blk
    od = _dot(jnp.where((t_i <= s_i) if rev else (t_i >= s_i), sc, 0.0).astype(BF16), vm)
    bend = jnp.broadcast_to(bb[edge:edge + 1, :], (blk, w))
    st = st_s[d]
    oi = lax.dot_general((qb * jnp.exp2(bb)).astype(BF16), st.astype(BF16), (((1,), (1,)), ((), ())),
                         preferred_element_type=F32)
    ktm = _stack_heads(kb * jnp.exp2(bend - bb))
    kv = lax.dot_general(vm, ktm, (((0,), (0,)), ((), ())), preferred_element_type=F32)
    st_s[d] = st * jnp.exp2(bend[0:1, :]) + kv
    o_s[d, rows, :] = od + oi


def _gated_kernel(pf_mm_ref, pf_la_ref, pb_mm_ref, pb_la_ref, gain_ref, s0_ref, tri_ref, ones_ref, mbd_ref,
                  y_ref, st_ref, b_s, p_s, o_s, st_s, *, mode, nc):
    j = pl.program_id(1)
    pf_ref = (pf_mm_ref, pf_la_ref)
    pb_ref = (pb_mm_ref, pb_la_ref)
    p_refs = (pf_ref, pb_ref)
    all_rows = slice(0, LC)

    @pl.when(j == 0)
    def _():
        st_s[0] = s0_ref[0, 0]
        st_s[1] = s0_ref[0, 1]

    def cumulate(level):
        for d in range(2):
            b_s[d] = _dot_w2(tri_ref[level, d], _operand(p_refs[d], mode, "la", d, all_rows))

    def run_bounded(blk):
        for g in range(LC // blk):
            _gated_block_bounded(0, g, blk, pf_ref, mode, b_s, o_s, st_s)
            _gated_block_bounded(1, LC // blk - 1 - g, blk, pb_ref, mode, b_s, o_s, st_s)

    def run_exact():
        def body(it, carry):
            _gated_subblock(0, it, pf_ref, mode, ones_ref, mbd_ref, b_s, p_s, o_s, st_s)
            _gated_subblock(1, N_SUB - 1 - it, pb_ref, mode, ones_ref, mbd_ref, b_s, p_s, o_s, st_s)
            return carry

        lax.fori_loop(0, N_SUB, body, 0)

    cumulate(0)
    wide = _centred_span(b_s, BOUNDED_BLKS[0]) < FAST_LOG2_RANGE

    @pl.when(wide)
    def _():
        run_bounded(BOUNDED_BLKS[0])

    @pl.when(jnp.logical_not(wide))
    def _():
        cumulate(1)
        narrow = _centred_span(b_s, BOUNDED_BLKS[1]) < FAST_LOG2_RANGE

        @pl.when(narrow)
        def _():
            run_bounded(BOUNDED_BLKS[1])

        @pl.when(jnp.logical_not(narrow))
        def _():
            cumulate(2)
            run_exact()

    @pl.when(j == nc - 1)
    def _():
        for d in range(2):
            for h in range(N_HEADS):
                blk = pl.ds(h * HEAD_D, HEAD_D)
                st_ref[0, d, h] = st_s[d, blk, blk]

    def finish(o, p_ref):
        ms = _dot_x2(o * o, ones_ref[...]) * (1.0 / HEAD_D)
        return o * lax.rsqrt(ms + EPS) * gain_ref[...] * _operand(p_ref, mode, "gate", 0, all_rows)

    rows_f = pl.ds(pl.multiple_of(j * LC, LC), LC)
    rows_b = pl.ds(pl.multiple_of((nc - 1 - j) * LC, LC), LC)
    if nc == 1:
        y_ref[0] = finish(o_s[0] + o_s[1], pf_ref)
    else:
        @pl.when(j < nc // 2)
        def _():
            y_ref[0, rows_f, :] = o_s[0]
            y_ref[0, rows_b, :] = o_s[1]

        @pl.when(j >= nc // 2)
        def _():
            y_ref[0, rows_f, :] = finish(y_ref[0, rows_f, :] + o_s[0], pf_ref)
            y_ref[0, rows_b, :] = finish(y_ref[0, rows_b, :] + o_s[1], pb_ref)


def _gated_consts():
    r = jnp.arange(LC)

    def tri(block):
        same = (r[:, None] // block) == (r[None, :] // block)
        lower = (same & (r[None, :] <= r[:, None])).astype(BF16)
        upper = (same & (r[None, :] >= r[:, None])).astype(BF16)
        return jnp.stack([lower, upper])

    head = (r[:, None] // HEAD_D) == (r[None, :] // HEAD_D)
    tris = jnp.stack([tri(blk) for blk in BOUNDED_BLKS + (SUB,)])
    return tris, head.astype(BF16), head.astype(F32)


def _gated_mixer(p_mm, p_la, gain, s0t, consts, mode):
    b, l, width = p_mm.shape
    nc = l // LC
    assert nc == 1 or nc % 2 == 0
    tri, ones, mbd = consts
    w = GROUP_W
    const2 = lambda shape: pl.BlockSpec(shape, lambda i, j: (0,) * len(shape))
    return pl.pallas_call(
        functools.partial(_gated_kernel, mode=mode, nc=nc),
        out_shape=[jax.ShapeDtypeStruct((b, l, w), F32),
                   jax.ShapeDtypeStruct((b, 2, N_HEADS, HEAD_D, HEAD_D), F32)],
        grid=(b, nc),
        in_specs=[
            pl.BlockSpec((1, LC, width), lambda i, j: (i, j, 0)),
            pl.BlockSpec((1, LC, W_DECAY), lambda i, j: (i, j, 0)),
            pl.BlockSpec((1, LC, width), lambda i, j: (i, nc - 1 - j, 0)),
            pl.BlockSpec((1, LC, W_DECAY), lambda i, j: (i, nc - 1 - j, 0)),
            const2(gain.shape),
            pl.BlockSpec((1, 2, w, w), lambda i, j: (i, 0, 0, 0)),
            const2(tri.shape), const2(ones.shape), const2(mbd.shape),
        ],
        out_specs=[pl.BlockSpec((1, l, w), lambda i, j: (i, 0, 0)),
                   pl.BlockSpec((1, 2, N_HEADS, HEAD_D, HEAD_D), lambda i, j: (i, 0, 0, 0, 0))],
        scratch_shapes=[pltpu.VMEM((2, LC, w), F32), pltpu.VMEM((2, SUB * SUB, w), BF16),
                        pltpu.VMEM((2, LC, w), F32), pltpu.VMEM((2, w, w), F32)],
        compiler_params=_params(("parallel", "arbitrary")),
        name="gated_" + mode,
    )(p_mm, p_la, p_mm, p_la, gain, s0t, tri, ones, mbd)


def _state_to_blockdiag_t(s):
    b = s.shape[0]
    st = jnp.swapaxes(s, -1, -2)
    eye = jnp.eye(N_HEADS, dtype=s.dtype)
    full = st[:, :, :, :, None, :] * eye[None, None, :, None, :, None]
    return full.reshape(b, 2, GROUP_W, GROUP_W)


def _blockdiag_t_to_state(st):
    return jnp.swapaxes(st, -1, -2)


def _rglru_kernel(x_ref, g_ref, cw_ref, cb_ref, w_ref, bias_ref, nsp_ref, h0_ref, y_ref, hT_ref,
                  xc_s, a_s, u_s, *, l):
    c = RG_HALF
    x = x_ref[0]
    row = lax.broadcasted_iota(jnp.int32, (l, c), 0)
    xc = x * cw_ref[2:3, :] + cb_ref[...]
    xc = xc + jnp.where(row >= 2, pltpu.roll(x, 2, 0), 0.0) * cw_ref[0:1, :]
    xc = xc + jnp.where(row >= 1, pltpu.roll(x, 1, 0), 0.0) * cw_ref[1:2, :]
    xc = xc + jnp.where(row <= l - 2, pltpu.roll(x, l - 1, 0), 0.0) * cw_ref[3:4, :]
    xc_s[...] = xc

    nslab = l // RG_ROWS
    grp = (RG_ROWS // SUBLANE, SUBLANE, c)
    sub = lax.broadcasted_iota(jnp.int32, grp, 1)

    def slab(n, carry):
        rows = pl.ds(pl.multiple_of(n * RG_ROWS, RG_ROWS), RG_ROWS)
        xs = xc_s[rows, :]
        gates = _sigmoid(_dot(xs.astype(BF16), w_ref[...]) + bias_ref[...])
        for d in range(2):
            r = gates[:, (2 * d) * c:(2 * d + 1) * c]
            i = gates[:, (2 * d + 1) * c:(2 * d + 2) * c]
            log_a = r * nsp_ref[d:d + 1, :]
            a_flat = jnp.exp(log_a)
            a = a_flat.reshape(grp)
            u = (jnp.sqrt(1.0 - a_flat * a_flat) * (i * xs)).reshape(grp)
            for sft in (1, 2, 4):
                if d == 0:
                    ok = sub >= sft
                    a_n, u_n = pltpu.roll(a, sft, 1), pltpu.roll(u, sft, 1)
                else:
                    ok = sub <= SUBLANE - 1 - sft
                    a_n, u_n = pltpu.roll(a, SUBLANE - sft, 1), pltpu.roll(u, SUBLANE - sft, 1)
                u = jnp.where(ok, a * u_n + u, u)
                a = jnp.where(ok, a * a_n, a)
            a_s[d, rows, :] = a.reshape(RG_ROWS, c)
            u_s[d, rows, :] = u.reshape(RG_ROWS, c)
        return carry

    lax.fori_loop(0, nslab, slab, 0)

    ngrp = l // SUBLANE

    def carry_step(n, hs):
        hf, hb = hs
        rows_f = pl.ds(pl.multiple_of(n * SUBLANE, SUBLANE), SUBLANE)
        rows_b = pl.ds(pl.multiple_of((ngrp - 1 - n) * SUBLANE, SUBLANE), SUBLANE)
        hh_f = a_s[0, rows_f, :] * hf + u_s[0, rows_f, :]
        hh_b = a_s[1, rows_b, :] * hb + u_s[1, rows_b, :]
        u_s[0, rows_f, :] = hh_f
        u_s[1, rows_b, :] = hh_b
        return (jnp.broadcast_to(hh_f[SUBLANE - 1:SUBLANE, :], (SUBLANE, c)),
                jnp.broadcast_to(hh_b[0:1, :], (SUBLANE, c)))

    h_init = (jnp.broadcast_to(h0_ref[0, 0:1, :], (SUBLANE, c)), jnp.broadcast_to(h0_ref[0, 1:2, :], (SUBLANE, c)))
    hf, hb = lax.fori_loop(0, ngrp, carry_step, h_init, unroll=2)
    hT_ref[0, 0:1, :] = hf[0:1, :]
    hT_ref[0, 1:2, :] = hb[0:1, :]

    g = g_ref[0]
    gelu = 0.5 * g * (1.0 + jnp.tanh(math.sqrt(2.0 / math.pi) * (g + 0.044715 * (g * g * g))))
    y_ref[0] = (u_s[0] + u_s[1]) * gelu


def _rglru(p, cw, cb, wbd, bias, nsp, h0):
    b, l, _ = p.shape
    c = RG_HALF
    half = lambda shape: pl.BlockSpec(shape, lambda i, j: (0,) * (len(shape) - 1) + (j,))
    return pl.pallas_call(
        functools.partial(_rglru_kernel, l=l),
        out_shape=[jax.ShapeDtypeStruct((b, l, GROUP_W), F32), jax.ShapeDtypeStruct((b, 2, GROUP_W), F32)],
        grid=(b, 2),
        in_specs=[
            pl.BlockSpec((1, l, c), lambda i, j: (i, 0, j)),
            pl.BlockSpec((1, l, c), lambda i, j: (i, 0, 2 + j)),
            half((RG_CONV_W, c)), half((1, c)),
            pl.BlockSpec((None, c, 4 * c), lambda i, j: (j, 0, 0)),
            pl.BlockSpec((None, 1, 4 * c), lambda i, j: (j, 0, 0)),
            half((2, c)),
            pl.BlockSpec((1, 2, c), lambda i, j: (i, 0, j)),
        ],
        out_specs=[pl.BlockSpec((1, l, c), lambda i, j: (i, 0, j)),
                   pl.BlockSpec((1, 2, c), lambda i, j: (i, 0, j))],
        scratch_shapes=[pltpu.VMEM((l, c), F32), pltpu.VMEM((2, l, c), F32), pltpu.VMEM((2, l, c), F32)],
        compiler_params=_params(("parallel", "parallel")),
        name="rglru",
    )(p, p, cw, cb, wbd, bias, nsp, h0)


def _short_conv3(x, w_ref, b_ref):
    l = x.shape[0]
    row = lax.broadcasted_iota(jnp.int32, x.shape, 0)
    o = x * w_ref[1:2, :] + b_ref[...]
    o = o + jnp.where(row >= 1, pltpu.roll(x, 1, 0), 0.0) * w_ref[0:1, :]
    return o + jnp.where(row <= l - 2, pltpu.roll(x, l - 1, 0), 0.0) * w_ref[2:3, :]


def _hy_filter_kernel(pe_ref, w1_ref, b1_ref, w2_ref, b2_ref, w3_ref, dec_ref, o_ref):
    pe = pe_ref[...]
    h = jnp.sin(_dot3(pe, w1_ref[...]) + b1_ref[...])
    h = jnp.sin(_dot3(h, w2_ref[...]) + b2_ref[...])
    h = _dot3(h, w3_ref[...])
    dist = pe[:, LANE - 1:LANE]
    h = h * jnp.exp(-dist * dec_ref[...])
    o_ref[...] = h / jnp.sum(jnp.abs(h), axis=0, keepdims=True)


def _hy_filters(pe, w1, b1, w2, b2, w3, decay):
    l = pe.shape[0]
    n = w3.shape[1]
    args = (pe, w1, b1, w2, b2, w3, decay)
    return pl.pallas_call(
        _hy_filter_kernel,
        out_shape=jax.ShapeDtypeStruct((l, n), F32),
        grid=(1,),
        in_specs=[pl.BlockSpec(a.shape, lambda i: (0, 0)) for a in args],
        out_specs=pl.BlockSpec((l, n), lambda i: (0, 0)),
        compiler_params=_params(("arbitrary",)),
        name="hy_filters",
    )(*args)


def _hy_filter_dft_kernel(h_ref, ct_ref, st_ref, hc_ref, hs_ref, ht_s):
    @pl.when(pl.program_id(0) == 0)
    def _():
        ht_s[...] = h_ref[...].T.astype(BF16)

    hc_ref[...] = _dot(ht_s[...], ct_ref[...])
    hs_ref[...] = _dot(ht_s[...], st_ref[...])


def _hy_filter_dft(h, tables):
    l, n = h.shape
    ct, st, _, _ = tables
    nk = ct.shape[0]
    spec = jax.ShapeDtypeStruct((n, nk * FREQ_TILE), F32)
    return pl.pallas_call(
        _hy_filter_dft_kernel,
        out_shape=[spec, spec],
        grid=(nk,),
        in_specs=[pl.BlockSpec((l, n), lambda k: (0, 0)),
                  pl.BlockSpec((None, l, FREQ_TILE), lambda k: (k, 0, 0)),
                  pl.BlockSpec((None, l, FREQ_TILE), lambda k: (k, 0, 0))],
        out_specs=[pl.BlockSpec((n, FREQ_TILE), lambda k: (0, k)), pl.BlockSpec((n, FREQ_TILE), lambda k: (0, k))],
        scratch_shapes=[pltpu.VMEM((n, l), BF16)],
        compiler_params=_params(("arbitrary",)),
        name="hy_filter_dft",
    )(h, ct, st)


def _hy_conv_kernel(z_ref, x_ref, hc_ref, hs_ref, skip_ref, wz_ref, bz_ref, wx_ref, bx_ref,
                    ct_ref, st_ref, wc_ref, ws_ref, o_ref, zt_s, *, bb, nk, z_raw):
    kt = pl.program_id(1)
    c = GROUP_W

    def z_of(b):
        return _short_conv3(z_ref[b], wz_ref, bz_ref) if z_raw else z_ref[b]

    @pl.when(kt == 0)
    def _():
        o_ref[...] = jnp.zeros_like(o_ref)
        for b in range(bb):
            zt_s[b * c:(b + 1) * c, :] = z_of(b).T.astype(BF16)

    zt = zt_s[...]
    xc = _dot(zt, ct_ref[...])
    xs = _dot(zt, st_ref[...])
    hc = hc_ref[...]
    hs = hs_ref[...]
    wc = wc_ref[...]
    ws = ws_ref[...]
    for b in range(bb):
        zc = xc[b * c:(b + 1) * c, :]
        zsn = xs[b * c:(b + 1) * c, :]
        yc = zc * hc - zsn * hs
        ys = zc * hs + zsn * hc
        a = (wc * yc + ws * ys).T.astype(BF16)
        bm = (wc * ys - ws * yc).T.astype(BF16)
        o_ref[b] += _dot(ct_ref[...], a) + _dot(st_ref[...], bm)

    @pl.when(kt == nk - 1)
    def _():
        for b in range(bb):
            gate = _short_conv3(x_ref[b], wx_ref, bx_ref)
            o_ref[b] = gate * (o_ref[b] + skip_ref[...] * z_of(b))


def _hy_long_conv(pc, zprev, z_col, gate_col, cw, cb, hspec, h_col, skip, skip_col, tables, bb):
    b, l, _ = pc.shape
    c = GROUP_W
    ct, st, wc, ws = tables
    nk = ct.shape[0]
    z_raw = zprev is None
    z_arr = pc if z_raw else zprev
    return pl.pallas_call(
        functools.partial(_hy_conv_kernel, bb=bb, nk=nk, z_raw=z_raw),
        out_shape=jax.ShapeDtypeStruct((b, l, c), F32),
        grid=(b // bb, nk),
        in_specs=[
            pl.BlockSpec((bb, l, c), lambda g, k: (g, 0, z_col)),
            pl.BlockSpec((bb, l, c), lambda g, k: (g, 0, gate_col)),
            pl.BlockSpec((c, FREQ_TILE), lambda g, k: (h_col, k)),
            pl.BlockSpec((c, FREQ_TILE), lambda g, k: (h_col, k)),
            pl.BlockSpec((1, c), lambda g, k: (0, skip_col)),
            pl.BlockSpec((3, c), lambda g, k: (0, z_col)),
            pl.BlockSpec((1, c), lambda g, k: (0, z_col)),
            pl.BlockSpec((3, c), lambda g, k: (0, gate_col)),
            pl.BlockSpec((1, c), lambda g, k: (0, gate_col)),
            pl.BlockSpec((None, l, FREQ_TILE), lambda g, k: (k, 0, 0)),
            pl.BlockSpec((None, l, FREQ_TILE), lambda g, k: (k, 0, 0)),
            pl.BlockSpec((1, FREQ_TILE), lambda g, k: (0, k)),
            pl.BlockSpec((1, FREQ_TILE), lambda g, k: (0, k)),
        ],
        out_specs=pl.BlockSpec((bb, l, c), lambda g, k: (g, 0, 0)),
        scratch_shapes=[pltpu.VMEM((bb * c, l), BF16)],
        compiler_params=_params(("parallel", "arbitrary")),
        name="hy_long_conv",
    )(z_arr, pc, hspec[0], hspec[1], skip, cw, cb, cw, cb, ct, st, wc, ws)


def _dft_tables(l):
    n = 3 * l // 2
    nf = n // 2 + 1
    nfp = -(-nf // FREQ_TILE) * FREQ_TILE
    k = jnp.arange(nfp, dtype=jnp.int32)
    ta = jnp.arange(l // TWID, dtype=jnp.int32) * TWID
    tb = jnp.arange(TWID, dtype=jnp.int32)
    live = (k < nf)

    def cos_sin(m):
        ang = (m % n).astype(F32) * (2.0 * math.pi / n)
        return jnp.cos(ang), jnp.sin(ang)

    def tiles(x):
        return x.reshape(x.shape[0], nfp // FREQ_TILE, FREQ_TILE).transpose(1, 0, 2)

    ca, sa = cos_sin(ta[:, None] * k[None, :])
    cb, sb = cos_sin(tb[:, None] * k[None, :])
    ca, sa = (tiles(jnp.where(live[None, :], x, 0.0))[:, :, None, :] for x in (ca, sa))
    cb, sb = tiles(cb)[:, None, :, :], tiles(sb)[:, None, :, :]
    shape = (nfp // FREQ_TILE, l, FREQ_TILE)
    ct = (ca * cb - sa * sb).reshape(shape).astype(BF16)
    st = (sa * cb + ca * sb).reshape(shape).astype(BF16)
    wk = jnp.where((k == 0) | (k == n // 2), 1.0, 2.0) / n
    wk = jnp.where(live, wk, 0.0).astype(F32)
    cp, sp = cos_sin(k * (l // 2))
    return ct, st, (wk * cp)[None, :], (wk * sp)[None, :]


def _hy_pos_features(l):
    pos = jnp.arange(l, dtype=F32)
    t = pos / (l - 1)
    ang = (2.0 * jnp.pi * pos / l)[:, None] * jnp.linspace(1e-4, HY_POS_BANDS - 1, HY_POS_BANDS, dtype=F32)[None, :]
    half = l // 2
    dist = jnp.abs(pos - half) / half
    pe = jnp.concatenate([t[:, None], jnp.cos(ang), -jnp.sin(ang)], axis=-1)
    pad = jnp.zeros((l, LANE - 1 - pe.shape[1]), F32)
    return jnp.concatenate([pe, pad, dist[:, None]], axis=-1)


def _grid_position_encoding(n_pos, dim):
    quarter = dim // 4
    omega = 1.0 / (POS_BASE ** (jnp.arange(quarter, dtype=F32) / quarter))
    ang = jnp.arange(n_pos, dtype=F32)[:, None] * omega[None, :]
    return jnp.concatenate([jnp.sin(ang), jnp.cos(ang)], axis=-1)


def _blockdiag2(a, b):
    z = jnp.zeros_like(a)
    return jnp.concatenate([jnp.concatenate([a, z], axis=1), jnp.concatenate([z, b], axis=1)], axis=0)


def _layer_params(l, hg_lb, norm1_g, norm2_g, w_in, w_out, gla_w_gate, gla_b_gate, gla_norm_g,
                  rg_conv_w, rg_conv_b, rg_w_a, rg_b_a, rg_w_x, rg_b_x, rg_lambda,
                  hy_conv_w, hy_conv_b, hy_w1, hy_b1, hy_w2, hy_b2, hy_w3, hy_decay, hy_skip,
                  hg_norm_g, ffn_w1, ffn_w3, ffn_w2):
    d, w = D_MODEL, GROUP_W
    n_gla = 4 * w + GLA_LOWRANK
    wi = w_in[l]
    w_proj = jnp.concatenate([wi[:, :n_gla], jnp.zeros((d, LR_PAD - GLA_LOWRANK), F32), wi[:, n_gla:]], axis=1)
    wg = jnp.concatenate([gla_w_gate[l], jnp.zeros((2, LR_PAD - GLA_LOWRANK, w), F32)], axis=1)
    zrow = jnp.zeros((w,), F32)
    par_gla = jnp.stack([gla_b_gate[l, 0], gla_b_gate[l, 1], zrow, gla_norm_g[l]] + [zrow] * 4)
    lb = hg_lb[l]
    par_hg = jnp.stack([1.0 - lb, jnp.log(lb), jnp.log1p(-lb), hg_norm_g[l]] + [zrow] * 4)
    wa, wx = rg_w_a[l], rg_w_x[l]
    rg_w = jnp.stack([
        jnp.concatenate([_blockdiag2(m[dd, 2 * j], m[dd, 2 * j + 1]) for dd in range(2) for m in (wa, wx)], axis=1)
        for j in range(2)])
    ba, bx = rg_b_a[l], rg_b_x[l]
    rg_bias = jnp.stack([
        jnp.concatenate([v[dd, j * RG_HALF:(j + 1) * RG_HALF] for dd in range(2) for v in (ba, bx)])[None, :]
        for j in range(2)])
    nsp = -RG_C * jax.nn.softplus(-rg_lambda[l])
    w1p = jnp.concatenate([hy_w1[l], jnp.zeros((LANE - hy_w1.shape[1], HY_FFN_W), F32)], axis=0)
    return dict(
        norm1=norm1_g[l][None, :], norm2=norm2_g[l][None, :],
        w_proj=w_proj.astype(BF16), w_out=w_out[l].astype(BF16),
        wg=wg.astype(BF16), par_gla=par_gla, par_hg=par_hg,
        rg_cw=rg_conv_w[l], rg_cb=rg_conv_b[l][None, :], rg_w=rg_w.astype(BF16), rg_bias=rg_bias, rg_nsp=nsp,
        hy_cw=hy_conv_w[l], hy_cb=hy_conv_b[l][None, :], hy_w1=w1p, hy_b1=hy_b1[l][None, :],
        hy_w2=hy_w2[l], hy_b2=hy_b2[l][None, :], hy_w3=hy_w3[l], hy_decay=hy_decay[l][None, :],
        hy_skip=hy_skip[l][None, :],
        w1=ffn_w1[l].astype(BF16), w3=ffn_w3[l].astype(BF16), w2=ffn_w2[l].astype(BF16))


def _trunk_layer(x, p, mod4, layer, row0, seq_shape, s_gla, s_rg, s_hg, stream_consts, final, final_g):
    bm, lm, d = x.shape
    b, l = seq_shape
    gated_consts, pe, tables, bb = stream_consts
    outs = _norm_proj(x, mod4, layer, row0, p["norm1"], p["w_proj"], p["wg"], p["par_gla"], p["par_hg"])
    pa, pa_la, pb, pc, pd, pd_la = (t.reshape(b, l, t.shape[-1]) for t in outs)

    ya, st_a = _gated_mixer(pa, pa_la, p["par_gla"][3:4], _state_to_blockdiag_t(s_gla), gated_consts, "gla")
    yd, st_d = _gated_mixer(pd, pd_la, p["par_hg"][3:4], _state_to_blockdiag_t(s_hg), gated_consts, "hg")
    yb, st_b = _rglru(pb, p["rg_cw"], p["rg_cb"], p["rg_w"], p["rg_bias"], p["rg_nsp"], s_rg)

    filt = _hy_filters(pe, p["hy_w1"], p["hy_b1"], p["hy_w2"], p["hy_b2"], p["hy_w3"], p["hy_decay"])
    hspec = _hy_filter_dft(filt, tables)
    z1 = _hy_long_conv(pc, None, 0, 1, p["hy_cw"], p["hy_cb"], hspec, 0, p["hy_skip"], 0, tables, bb)
    yc = _hy_long_conv(pc, z1, 0, 2, p["hy_cw"], p["hy_cb"], hspec, 1, p["hy_skip"], 1, tables, bb)

    ys = [t.reshape(bm, lm, GROUP_W) for t in (ya, yb, yc, yd)]
    x = _out_ffn(x, ys, mod4, layer, row0, p["norm2"], final_g, p["w_out"], p["w1"], p["w3"], p["w2"], final)
    return x, (_blockdiag_t_to_state(st_a), st_b, _blockdiag_t_to_state(st_d))


def kernel(x_prompt, x_sample, state_gla, state_rglru, state_hgrn, c, c_ctx, norm1_g, norm2_g, final_norm_g, w_mod, b_mod, w_in, w_out, gla_w_gate, gla_b_gate, gla_norm_g, rg_conv_w, rg_conv_b, rg_w_a, rg_b_a, rg_w_x, rg_b_x, rg_lambda, hy_conv_w, hy_conv_b, hy_w1, hy_b1, hy_w2, hy_b2, hy_w3, hy_decay, hy_skip, hg_lower, hg_norm_g, ffn_w1, ffn_w3, ffn_w2):
    depth = w_in.shape[0]
    nb, seq, d = x_prompt.shape
    db, dseq, _ = x_sample.shape

    hg_lb = jnp.cumsum(jax.nn.softmax(hg_lower.astype(F32), axis=0), axis=0)
    hg_lb = hg_lb - hg_lb[0:1]

    cvec = jnp.concatenate([c_ctx[None, :], c, jnp.zeros((SUBLANE - 1 - db, d), F32)], axis=0)
    mod4 = _modulation(cvec, w_mod, b_mod).reshape(depth, SUBLANE, 1, N_MOD * d)

    gated_consts = _gated_consts()
    consts_p = (gated_consts, _hy_pos_features(seq), _dft_tables(seq), 8)
    consts_s = (gated_consts, _hy_pos_features(dseq), _dft_tables(dseq), 1)

    xp = x_prompt.reshape(1, nb * seq, d)
    xs = _add_pos(x_sample, _grid_position_encoding(max(dseq // GRID_W, GRID_W), d))
    zero_gla = jnp.zeros((nb, 2, N_HEADS, HEAD_D, HEAD_D), F32)
    zero_rg = jnp.zeros((nb, 2, GROUP_W), F32)
    final_g = final_norm_g[None, :]

    gla_states, rg_states, hg_states = [], [], []
    for l in range(depth):
        p = _layer_params(l, hg_lb, norm1_g, norm2_g, w_in, w_out, gla_w_gate, gla_b_gate, gla_norm_g,
                          rg_conv_w, rg_conv_b, rg_w_a, rg_b_a, rg_w_x, rg_b_x, rg_lambda,
                          hy_conv_w, hy_conv_b, hy_w1, hy_b1, hy_w2, hy_b2, hy_w3, hy_decay, hy_skip,
                          hg_norm_g, ffn_w1, ffn_w3, ffn_w2)
        final = l == depth - 1
        xp, (sg, sr, sh) = _trunk_layer(xp, p, mod4, l, 0, (nb, seq), zero_gla, zero_rg, zero_gla,
                                        consts_p, final, final_g)
        xs, _ = _trunk_layer(xs, p, mod4, l, 1, (db, dseq), state_gla[:, l], state_rglru[:, l],
                             state_hgrn[:, l], consts_s, final, final_g)
        gla_states.append(sg)
        rg_states.append(sr)
        hg_states.append(sh)

    return (xp.reshape(nb, seq, d), xs,
            jnp.stack(gla_states, axis=1), jnp.stack(rg_states, axis=1), jnp.stack(hg_states, axis=1))
```

```python
import functools
import math

import jax
import jax.numpy as jnp
from jax import lax
from jax.experimental import pallas as pl
from jax.experimental.pallas import tpu as pltpu

F32 = jnp.float32
BF16 = jnp.bfloat16

D_MODEL = 1024
N_MOD = 6
GROUP_W = 256
N_HEADS = 4
HEAD_D = GROUP_W // N_HEADS
GLA_LOWRANK = 16
GLA_GATE_TEMP = 16.0
RG_C = 8.0
RG_CONV_W = 4
HY_POS_BANDS = 16
HY_FFN_W = 64
D_FF = 2816
EPS = 1e-6
GRID_W = 64
POS_BASE = 10000.0

LANE = 128
SUBLANE = 8
VMEM_LIMIT = 56 * 1024 * 1024

LOG2E = 1.0 / math.log(2.0)
FAST_LOG2_RANGE = 96.0
SUB = 16
LC = 256
N_SUB = LC // SUB
BOUNDED_BLKS = (64, 32)
LR_PAD = LANE
W_GLA = 4 * GROUP_W + LR_PAD
W_RG = 2 * GROUP_W
W_HY = 3 * GROUP_W
W_HG = 5 * GROUP_W
W_PROJ = W_GLA + W_RG + W_HY + W_HG
W_GLA_MM = 4 * GROUP_W
W_HG_MM = 5 * GROUP_W
W_DECAY = 2 * GROUP_W
GATED_COLS = {"gla": dict(q=0, k=(1, 1), v=2, gate=3, la=(0, 1)),
              "hg": dict(q=0, k=(1, 2), v=3, gate=4, la=(0, 1))}
TM = 512
TF = D_FF // 2
TN_MOD = 512
RG_HALF = GROUP_W // 2
RG_ROWS = 256
FREQ_TILE = 256
TWID = 64


def _dot(a, b):
    return jnp.dot(a, b, preferred_element_type=F32)


def _split(x):
    hi = x.astype(BF16)
    lo = (x - hi.astype(F32)).astype(BF16)
    return hi, lo


def _dot_x2(x, w):
    hi, lo = _split(x)
    return _dot(hi, w) + _dot(lo, w)


def _dot_w2(w, x):
    hi, lo = _split(x)
    return _dot(w, hi) + _dot(w, lo)


def _dot3(a, b):
    ah, al = _split(a)
    bh, bl = _split(b)
    return _dot(ah, bh) + _dot(al, bh) + _dot(ah, bl)


def _sigmoid(x):
    return 0.5 * jnp.tanh(0.5 * x) + 0.5


def _silu(x):
    return x * _sigmoid(x)


def _log1p_exp_neg_abs(x):
    return jnp.log(1.0 + jnp.exp(-jnp.abs(x)))


def _log_sigmoid(x):
    return jnp.minimum(x, 0.0) - _log1p_exp_neg_abs(x)


def _rms(x):
    return x * lax.rsqrt(jnp.mean(x * x, axis=-1, keepdims=True) + EPS)


def _params(sem, vmem=VMEM_LIMIT):
    return pltpu.CompilerParams(dimension_semantics=sem, vmem_limit_bytes=vmem)


def _mod_kernel(c_ref, w_ref, b_ref, o_ref):
    c = c_ref[...]
    o_ref[0] = _dot3(_silu(c), w_ref[0]) + b_ref[0]


def _modulation(cvec, w_mod, b_mod):
    depth = w_mod.shape[0]
    n = N_MOD * D_MODEL
    return pl.pallas_call(
        _mod_kernel,
        out_shape=jax.ShapeDtypeStruct((depth, SUBLANE, n), F32),
        grid=(depth, n // TN_MOD),
        in_specs=[
            pl.BlockSpec((SUBLANE, D_MODEL), lambda l, j: (0, 0)),
            pl.BlockSpec((1, D_MODEL, TN_MOD), lambda l, j: (l, 0, j)),
            pl.BlockSpec((1, 1, TN_MOD), lambda l, j: (l, 0, j)),
        ],
        out_specs=pl.BlockSpec((1, SUBLANE, TN_MOD), lambda l, j: (l, 0, j)),
        compiler_params=_params(("parallel", "parallel")),
        name="modulation",
    )(cvec, w_mod, b_mod.reshape(depth, 1, n))


def _add_kernel(x_ref, e_ref, o_ref):
    j = pl.program_id(1)
    half = D_MODEL // 2
    enc = e_ref[0:GRID_W, :]
    for r in range(TM // GRID_W):
        rows = slice(r * GRID_W, (r + 1) * GRID_W)
        enc_row = e_ref[pl.ds(j * (TM // GRID_W) + r, 1), :]
        o_ref[0, rows, 0:half] = x_ref[0, rows, 0:half] + enc_row
        o_ref[0, rows, half:] = x_ref[0, rows, half:] + enc


def _add_pos(x, enc):
    b, l, d = x.shape
    assert l // GRID_W <= enc.shape[0]
    return pl.pallas_call(
        _add_kernel,
        out_shape=jax.ShapeDtypeStruct(x.shape, F32),
        grid=(b, l // TM),
        in_specs=[pl.BlockSpec((1, TM, d), lambda i, j: (i, j, 0)),
                  pl.BlockSpec(enc.shape, lambda i, j: (0, 0))],
        out_specs=pl.BlockSpec((1, TM, d), lambda i, j: (i, j, 0)),
        compiler_params=_params(("parallel", "parallel")),
        name="add_pos",
    )(x, enc)


def _gla_features(p, wg, par):
    w = GROUP_W
    lr = p[:, 4 * w:4 * w + LR_PAD].astype(BF16)
    las = [_log_sigmoid(_dot(lr, wg[d]) + par[d:d + 1, :]) * (LOG2E / GLA_GATE_TEMP) for d in range(2)]
    return [p[:, 0:w] * (HEAD_D ** -0.5), p[:, w:2 * w], p[:, 2 * w:3 * w], _silu(p[:, 3 * w:4 * w])] + las


def _hg_features(p, par):
    w = GROUP_W
    ks, las = [], []
    for d in range(2):
        f = p[:, (1 + d) * w:(2 + d) * w]
        e = jnp.exp(-jnp.abs(f))
        ks.append(par[0:1, :] * (jnp.where(f >= 0.0, e, 1.0) / (1.0 + e)))
        y = par[2:3, :] + (jnp.minimum(f, 0.0) - jnp.log(1.0 + e))
        lb = par[1:2, :]
        las.append((jnp.maximum(lb, y) + _log1p_exp_neg_abs(lb - y)) * LOG2E)
    return [_silu(p[:, 0:w]), ks[0], ks[1], p[:, 3 * w:4 * w], _silu(p[:, 4 * w:5 * w]), las[0], las[1]]


def _proj_kernel(x_ref, mod_ref, g_ref, w_ref, wg_ref, pg_ref, ph_ref,
                 oa_ref, oal_ref, ob_ref, oc_ref, od_ref, odl_ref):
    x = x_ref[0]
    m = mod_ref[0, 0]
    sh = m[:, 0:D_MODEL]
    sc = m[:, D_MODEL:2 * D_MODEL]
    u = (_rms(x) * g_ref[...] * (1.0 + sc) + sh).astype(BF16)
    w = GROUP_W
    bounds = (0, W_GLA, W_GLA + W_RG, W_GLA + W_RG + W_HY, W_PROJ)
    p = _dot(u, w_ref[0])
    p_gla, p_rg, p_hy, p_hg = (p[:, lo:hi] for lo, hi in zip(bounds[:-1], bounds[1:]))

    def put(feats, mm_ref, decay_ref):
        n_mm = len(feats) - 2
        for i, t in enumerate(feats[:n_mm]):
            mm_ref[0, :, i * w:(i + 1) * w] = t.astype(BF16)
        for i, t in enumerate(feats[n_mm:]):
            decay_ref[0, :, i * w:(i + 1) * w] = t

    put(_gla_features(p_gla, wg_ref[0], pg_ref[0]), oa_ref, oal_ref)
    ob_ref[0] = p_rg
    oc_ref[0] = p_hy
    put(_hg_features(p_hg, ph_ref[0]), od_ref, odl_ref)


def _norm_proj(x, mod4, layer, row0, gain, w, wg, par_gla, par_hg):
    bm, lm, d = x.shape
    widths = (W_GLA_MM, W_DECAY, W_RG, W_HY, W_HG_MM, W_DECAY)
    dtypes = (BF16, F32, F32, F32, BF16, F32)
    per_layer = lambda a: pl.BlockSpec((1,) + a.shape[1:], lambda i, j: (layer,) + (0,) * (a.ndim - 1))
    return pl.pallas_call(
        _proj_kernel,
        out_shape=[jax.ShapeDtypeStruct((bm, lm, wd), dt) for wd, dt in zip(widths, dtypes)],
        grid=(bm, lm // TM),
        in_specs=[
            pl.BlockSpec((1, TM, d), lambda i, j: (i, j, 0)),
            pl.BlockSpec((1, 1, 1, N_MOD * d), lambda i, j: (layer, row0 + i, 0, 0)),
            pl.BlockSpec((1, d), lambda i, j: (0, 0)),
            per_layer(w), per_layer(wg), per_layer(par_gla), per_layer(par_hg),
        ],
        out_specs=[pl.BlockSpec((1, TM, wd), lambda i, j: (i, j, 0)) for wd in widths],
        compiler_params=_params(("parallel", "parallel")),
        name="norm_proj",
    )(x, mod4, gain, w, wg, par_gla, par_hg)


def _ffn_kernel(x_ref, ya_ref, yb_ref, yc_ref, yd_ref, mod_ref, g2_ref, gf_ref,
                wo_ref, w1_ref, w3_ref, w2_ref, o_ref, x1_s, u_s, acc_s, *, nf, final):
    f = pl.program_id(2)
    d = D_MODEL

    @pl.when(f == 0)
    def _():
        m = mod_ref[0, 0]
        g1 = m[:, 2 * d:3 * d]
        sh2 = m[:, 3 * d:4 * d]
        sc2 = m[:, 4 * d:5 * d]
        y = jnp.concatenate([ya_ref[0], yb_ref[0], yc_ref[0], yd_ref[0]], axis=-1)
        x1 = x_ref[0] + g1 * _dot(y.astype(BF16), wo_ref[...])
        x1_s[...] = x1
        u_s[...] = (_rms(x1) * g2_ref[...] * (1.0 + sc2) + sh2).astype(BF16)
        acc_s[...] = jnp.zeros_like(acc_s)

    u = u_s[...]
    h = _silu(_dot(u, w1_ref[...])) * _dot(u, w3_ref[...])
    acc_s[...] += _dot(h.astype(BF16), w2_ref[...])

    @pl.when(f == nf - 1)
    def _():
        g2 = mod_ref[0, 0][:, 5 * d:6 * d]
        xo = x1_s[...] + g2 * acc_s[...]
        if final:
            xo = _rms(xo) * gf_ref[...]
        o_ref[0] = xo


def _out_ffn(x, ys, mod4, layer, row0, g2, gf, wo, w1, w3, w2, final):
    bm, lm, d = x.shape
    nf = D_FF // TF
    tok = lambda wd: pl.BlockSpec((1, TM, wd), lambda i, j, f: (i, j, 0))
    return pl.pallas_call(
        functools.partial(_ffn_kernel, nf=nf, final=final),
        out_shape=jax.ShapeDtypeStruct(x.shape, F32),
        grid=(bm, lm // TM, nf),
        in_specs=[
            tok(d), tok(GROUP_W), tok(GROUP_W), tok(GROUP_W), tok(GROUP_W),
            pl.BlockSpec((1, 1, 1, N_MOD * d), lambda i, j, f: (layer, row0 + i, 0, 0)),
            pl.BlockSpec((1, d), lambda i, j, f: (0, 0)),
            pl.BlockSpec((1, d), lambda i, j, f: (0, 0)),
            pl.BlockSpec((None, d, d), lambda i, j, f: (layer, 0, 0)),
            pl.BlockSpec((None, d, TF), lambda i, j, f: (layer, 0, f)),
            pl.BlockSpec((None, d, TF), lambda i, j, f: (layer, 0, f)),
            pl.BlockSpec((None, TF, d), lambda i, j, f: (layer, f, 0)),
        ],
        out_specs=tok(d),
        scratch_shapes=[pltpu.VMEM((TM, d), F32), pltpu.VMEM((TM, d), BF16), pltpu.VMEM((TM, d), F32)],
        compiler_params=_params(("parallel", "parallel", "arbitrary")),
        name="out_ffn",
    )(x, *ys, mod4, g2, gf, wo, w1, w3, w2)


def _operand(p_ref, mode, name, d, rows):
    col = GATED_COLS[mode][name]
    col = col[d] if isinstance(col, tuple) else col
    ref = p_ref[1] if name == "la" else p_ref[0]
    return ref[0, rows, col * GROUP_W:(col + 1) * GROUP_W].astype(F32)


def _gated_subblock(d, i, p_ref, mode, ones_ref, mbd_ref, b_s, p_s, o_s, st_s):
    w = GROUP_W
    rev = d == 1
    tio = lax.broadcasted_iota(jnp.int32, (SUB, w), 0)
    edge = 0 if rev else SUB - 1
    rows = pl.ds(pl.multiple_of(i * SUB, SUB), SUB)
    bb = b_s[d, rows, :]
    qb = _operand(p_ref, mode, "q", d, rows)
    kb = _operand(p_ref, mode, "k", d, rows)
    vb = _operand(p_ref, mode, "v", d, rows)

    def row(a, s):
        return jnp.broadcast_to(a[s:s + 1, :], (SUB, w))

    for s in range(SUB):
        valid = (tio <= s) if rev else (tio >= s)
        e = jnp.exp2(jnp.where(valid, bb - row(bb, s), -jnp.inf))
        p_s[d, s * SUB:(s + 1) * SUB, :] = (e * qb * row(kb, s)).astype(BF16)
    r = _dot(p_s[d], ones_ref[...])
    od = jnp.zeros((SUB, w), F32)
    for s in range(SUB):
        od = od + r[s * SUB:(s + 1) * SUB, :] * row(vb, s)
    bend = row(bb, edge)
    st = st_s[d]
    qt = (qb * jnp.exp2(bb)).astype(BF16)
    oi = lax.dot_general(qt, st.astype(BF16), (((1,), (1,)), ((), ())), preferred_element_type=F32)
    kt = (kb * jnp.exp2(bend - bb)).astype(BF16)
    kv = lax.dot_general(vb.astype(BF16), kt, (((0,), (0,)), ((), ())), preferred_element_type=F32)
    st_s[d] = st * jnp.exp2(bend[0:1, :]) + kv * mbd_ref[...]
    o_s[d, rows, :] = od + oi


def _stack_heads(x):
    lane_head = lax.broadcasted_iota(jnp.int32, x.shape, 1) // HEAD_D
    return jnp.concatenate([jnp.where(lane_head == h, x, 0.0) for h in range(N_HEADS)], axis=0).astype(BF16)


def _centred_span(b_s, blk):
    span = None
    for d in range(2):
        for g in range(LC // blk):
            r0, rm, r1 = g * blk, g * blk + blk // 2, (g + 1) * blk - 1
            mid = b_s[d, rm:rm + 1, :]
            m = jnp.maximum(jnp.abs(b_s[d, r0:r0 + 1, :] - mid), jnp.abs(b_s[d, r1:r1 + 1, :] - mid))
            span = m if span is None else jnp.maximum(span, m)
    return jnp.max(span)


def _gated_block_bounded(d, g, blk, p_ref, mode, b_s, o_s, st_s):
    w = GROUP_W
    rev = d == 1
    edge = 0 if rev else blk - 1
    rows = slice(g * blk, (g + 1) * blk)
    bb = b_s[d, rows, :]
    qb = _operand(p_ref, mode, "q", d, rows)
    kb = _operand(p_ref, mode, "k", d, rows)
    cc = bb - jnp.broadcast_to(bb[blk // 2:blk // 2 + 1, :], bb.shape)
    vm = _stack_heads(_operand(p_ref, mode, "v", d, rows))
    km = _stack_heads(kb * jnp.exp2(-cc))
    sc = lax.dot_general((qb * jnp.exp2(cc)).astype(BF16), km, (((1,), (1,)), ((), ())),
                         preferred_element_type=F32)
    t_i = lax.broadcasted_iota(jnp.int32, sc.shape, 0)
    s_i = lax.broadcasted_iota(jnp.int32, sc.shape, 1) % blk
    od = _dot(jnp.where((t_i <= s_i) if rev else (t_i >= s_i), sc, 0.0).astype(BF16), vm)
    bend = jnp.broadcast_to(bb[edge:edge + 1, :], (blk, w))
    st = st_s[d]
    oi = lax.dot_general((qb * jnp.exp2(bb)).astype(BF16), st.astype(BF16), (((1,), (1,)), ((), ())),
                         preferred_element_type=F32)
    ktm = _stack_heads(kb * jnp.exp2(bend - bb))
    kv = lax.dot_general(vm, ktm, (((0,), (0,)), ((), ())), preferred_element_type=F32)
    st_s[d] = st * jnp.exp2(bend[0:1, :]) + kv
    o_s[d, rows, :] = od + oi


def _gated_kernel(pf_mm_ref, pf_la_ref, pb_mm_ref, pb_la_ref, gain_ref, s0_ref, tri_ref, ones_ref, mbd_ref,
                  y_ref, st_ref, b_s, p_s, o_s, st_s, *, mode, nc):
    j = pl.program_id(1)
    pf_ref = (pf_mm_ref, pf_la_ref)
    pb_ref = (pb_mm_ref, pb_la_ref)
    p_refs = (pf_ref, pb_ref)
    all_rows = slice(0, LC)

    @pl.when(j == 0)
    def _():
        st_s[0] = s0_ref[0, 0]
        st_s[1] = s0_ref[0, 1]

    def cumulate(level):
        for d in range(2):
            b_s[d] = _dot_w2(tri_ref[level, d], _operand(p_refs[d], mode, "la", d, all_rows))

    def run_bounded(blk):
        for g in range(LC // blk):
            _gated_block_bounded(0, g, blk, pf_ref, mode, b_s, o_s, st_s)
            _gated_block_bounded(1, LC // blk - 1 - g, blk, pb_ref, mode, b_s, o_s, st_s)

    def run_exact():
        def body(it, carry):
            _gated_subblock(0, it, pf_ref, mode, ones_ref, mbd_ref, b_s, p_s, o_s, st_s)
            _gated_subblock(1, N_SUB - 1 - it, pb_ref, mode, ones_ref, mbd_ref, b_s, p_s, o_s, st_s)
            return carry

        lax.fori_loop(0, N_SUB, body, 0)

    cumulate(0)
    wide = _centred_span(b_s, BOUNDED_BLKS[0]) < FAST_LOG2_RANGE

    @pl.when(wide)
    def _():
        run_bounded(BOUNDED_BLKS[0])

    @pl.when(jnp.logical_not(wide))
    def _():
        cumulate(1)
        narrow = _centred_span(b_s, BOUNDED_BLKS[1]) < FAST_LOG2_RANGE

        @pl.when(narrow)
        def _():
            run_bounded(BOUNDED_BLKS[1])

        @pl.when(jnp.logical_not(narrow))
        def _():
            cumulate(2)
            run_exact()

    @pl.when(j == nc - 1)
    def _():
        for d in range(2):
            for h in range(N_HEADS):
                blk = pl.ds(h * HEAD_D, HEAD_D)
                st_ref[0, d, h] = st_s[d, blk, blk]

    def finish(o, p_ref):
        ms = _dot_x2(o * o, ones_ref[...]) * (1.0 / HEAD_D)
        return o * lax.rsqrt(ms + EPS) * gain_ref[...] * _operand(p_ref, mode, "gate", 0, all_rows)

    rows_f = pl.ds(pl.multiple_of(j * LC, LC), LC)
    rows_b = pl.ds(pl.multiple_of((nc - 1 - j) * LC, LC), LC)
    if nc == 1:
        y_ref[0] = finish(o_s[0] + o_s[1], pf_ref)
    else:
        @pl.when(j < nc // 2)
        def _():
            y_ref[0, rows_f, :] = o_s[0]
            y_ref[0, rows_b, :] = o_s[1]

        @pl.when(j >= nc // 2)
        def _():
            y_ref[0, rows_f, :] = finish(y_ref[0, rows_f, :] + o_s[0], pf_ref)
            y_ref[0, rows_b, :] = finish(y_ref[0, rows_b, :] + o_s[1], pb_ref)


def _gated_consts():
    r = jnp.arange(LC)

    def tri(block):
        same = (r[:, None] // block) == (r[None, :] // block)
        lower = (same & (r[None, :] <= r[:, None])).astype(BF16)
        upper = (same & (r[None, :] >= r[:, None])).astype(BF16)
        return jnp.stack([lower, upper])

    head = (r[:, None] // HEAD_D) == (r[None, :] // HEAD_D)
    tris = jnp.stack([tri(blk) for blk in BOUNDED_BLKS + (SUB,)])
    return tris, head.astype(BF16), head.astype(F32)


def _gated_mixer(p_mm, p_la, gain, s0t, consts, mode):
    b, l, width = p_mm.shape
    nc = l // LC
    assert nc == 1 or nc % 2 == 0
    tri, ones, mbd = consts
    w = GROUP_W
    const2 = lambda shape: pl.BlockSpec(shape, lambda i, j: (0,) * len(shape))
    return pl.pallas_call(
        functools.partial(_gated_kernel, mode=mode, nc=nc),
        out_shape=[jax.ShapeDtypeStruct((b, l, w), F32),
                   jax.ShapeDtypeStruct((b, 2, N_HEADS, HEAD_D, HEAD_D), F32)],
        grid=(b, nc),
        in_specs=[
            pl.BlockSpec((1, LC, width), lambda i, j: (i, j, 0)),
            pl.BlockSpec((1, LC, W_DECAY), lambda i, j: (i, j, 0)),
            pl.BlockSpec((1, LC, width), lambda i, j: (i, nc - 1 - j, 0)),
            pl.BlockSpec((1, LC, W_DECAY), lambda i, j: (i, nc - 1 - j, 0)),
            const2(gain.shape),
            pl.BlockSpec((1, 2, w, w), lambda i, j: (i, 0, 0, 0)),
            const2(tri.shape), const2(ones.shape), const2(mbd.shape),
        ],
        out_specs=[pl.BlockSpec((1, l, w), lambda i, j: (i, 0, 0)),
                   pl.BlockSpec((1, 2, N_HEADS, HEAD_D, HEAD_D), lambda i, j: (i, 0, 0, 0, 0))],
        scratch_shapes=[pltpu.VMEM((2, LC, w), F32), pltpu.VMEM((2, SUB * SUB, w), BF16),
                        pltpu.VMEM((2, LC, w), F32), pltpu.VMEM((2, w, w), F32)],
        compiler_params=_params(("parallel", "arbitrary")),
        name="gated_" + mode,
    )(p_mm, p_la, p_mm, p_la, gain, s0t, tri, ones, mbd)


def _state_to_blockdiag_t(s):
    b = s.shape[0]
    st = jnp.swapaxes(s, -1, -2)
    eye = jnp.eye(N_HEADS, dtype=s.dtype)
    full = st[:, :, :, :, None, :] * eye[None, None, :, None, :, None]
    return full.reshape(b, 2, GROUP_W, GROUP_W)


def _blockdiag_t_to_state(st):
    return jnp.swapaxes(st, -1, -2)


def _rglru_kernel(x_ref, g_ref, cw_ref, cb_ref, w_ref, bias_ref, nsp_ref, h0_ref, y_ref, hT_ref,
                  xc_s, a_s, u_s, *, l):
    c = RG_HALF
    x = x_ref[0]
    row = lax.broadcasted_iota(jnp.int32, (l, c), 0)
    xc = x * cw_ref[2:3, :] + cb_ref[...]
    xc = xc + jnp.where(row >= 2, pltpu.roll(x, 2, 0), 0.0) * cw_ref[0:1, :]
    xc = xc + jnp.where(row >= 1, pltpu.roll(x, 1, 0), 0.0) * cw_ref[1:2, :]
    xc = xc + jnp.where(row <= l - 2, pltpu.roll(x, l - 1, 0), 0.0) * cw_ref[3:4, :]
    xc_s[...] = xc

    nslab = l // RG_ROWS
    grp = (RG_ROWS // SUBLANE, SUBLANE, c)
    sub = lax.broadcasted_iota(jnp.int32, grp, 1)

    def slab(n, carry):
        rows = pl.ds(pl.multiple_of(n * RG_ROWS, RG_ROWS), RG_ROWS)
        xs = xc_s[rows, :]
        gates = _sigmoid(_dot(xs.astype(BF16), w_ref[...]) + bias_ref[...])
        for d in range(2):
            r = gates[:, (2 * d) * c:(2 * d + 1) * c]
            i = gates[:, (2 * d + 1) * c:(2 * d + 2) * c]
            log_a = r * nsp_ref[d:d + 1, :]
            a_flat = jnp.exp(log_a)
            a = a_flat.reshape(grp)
            u = (jnp.sqrt(1.0 - a_flat * a_flat) * (i * xs)).reshape(grp)
            for sft in (1, 2, 4):
                if d == 0:
                    ok = sub >= sft
                    a_n, u_n = pltpu.roll(a, sft, 1), pltpu.roll(u, sft, 1)
                else:
                    ok = sub <= SUBLANE - 1 - sft
                    a_n, u_n = pltpu.roll(a, SUBLANE - sft, 1), pltpu.roll(u, SUBLANE - sft, 1)
                u = jnp.where(ok, a * u_n + u, u)
                a = jnp.where(ok, a * a_n, a)
            a_s[d, rows, :] = a.reshape(RG_ROWS, c)
            u_s[d, rows, :] = u.reshape(RG_ROWS, c)
        return carry

    lax.fori_loop(0, nslab, slab, 0)

    ngrp = l // SUBLANE

    def carry_step(n, hs):
        hf, hb = hs
        rows_f = pl.ds(pl.multiple_of(n * SUBLANE, SUBLANE), SUBLANE)
        rows_b = pl.ds(pl.multiple_of((ngrp - 1 - n) * SUBLANE, SUBLANE), SUBLANE)
        hh_f = a_s[0, rows_f, :] * hf + u_s[0, rows_f, :]
        hh_b = a_s[1, rows_b, :] * hb + u_s[1, rows_b, :]
        u_s[0, rows_f, :] = hh_f
        u_s[1, rows_b, :] = hh_b
        return (jnp.broadcast_to(hh_f[SUBLANE - 1:SUBLANE, :], (SUBLANE, c)),
                jnp.broadcast_to(hh_b[0:1, :], (SUBLANE, c)))

    h_init = (jnp.broadcast_to(h0_ref[0, 0:1, :], (SUBLANE, c)), jnp.broadcast_to(h0_ref[0, 1:2, :], (SUBLANE, c)))
    hf, hb = lax.fori_loop(0, ngrp, carry_step, h_init, unroll=2)
    hT_ref[0, 0:1, :] = hf[0:1, :]
    hT_ref[0, 1:2, :] = hb[0:1, :]

    g = g_ref[0]
    gelu = 0.5 * g * (1.0 + jnp.tanh(math.sqrt(2.0 / math.pi) * (g + 0.044715 * (g * g * g))))
    y_ref[0] = (u_s[0] + u_s[1]) * gelu


def _rglru(p, cw, cb, wbd, bias, nsp, h0):
    b, l, _ = p.shape
    c = RG_HALF
    half = lambda shape: pl.BlockSpec(shape, lambda i, j: (0,) * (len(shape) - 1) + (j,))
    return pl.pallas_call(
        functools.partial(_rglru_kernel, l=l),
        out_shape=[jax.ShapeDtypeStruct((b, l, GROUP_W), F32), jax.ShapeDtypeStruct((b, 2, GROUP_W), F32)],
        grid=(b, 2),
        in_specs=[
            pl.BlockSpec((1, l, c), lambda i, j: (i, 0, j)),
            pl.BlockSpec((1, l, c), lambda i, j: (i, 0, 2 + j)),
            half((RG_CONV_W, c)), half((1, c)),
            pl.BlockSpec((None, c, 4 * c), lambda i, j: (j, 0, 0)),
            pl.BlockSpec((None, 1, 4 * c), lambda i, j: (j, 0, 0)),
            half((2, c)),
            pl.BlockSpec((1, 2, c), lambda i, j: (i, 0, j)),
        ],
        out_specs=[pl.BlockSpec((1, l, c), lambda i, j: (i, 0, j)),
                   pl.BlockSpec((1, 2, c), lambda i, j: (i, 0, j))],
        scratch_shapes=[pltpu.VMEM((l, c), F32), pltpu.VMEM((2, l, c), F32), pltpu.VMEM((2, l, c), F32)],
        compiler_params=_params(("parallel", "parallel")),
        name="rglru",
    )(p, p, cw, cb, wbd, bias, nsp, h0)


def _short_conv3(x, w_ref, b_ref):
    l = x.shape[0]
    row = lax.broadcasted_iota(jnp.int32, x.shape, 0)
    o = x * w_ref[1:2, :] + b_ref[...]
    o = o + jnp.where(row >= 1, pltpu.roll(x, 1, 0), 0.0) * w_ref[0:1, :]
    return o + jnp.where(row <= l - 2, pltpu.roll(x, l - 1, 0), 0.0) * w_ref[2:3, :]


def _hy_filter_kernel(pe_ref, w1_ref, b1_ref, w2_ref, b2_ref, w3_ref, dec_ref, o_ref):
    pe = pe_ref[...]
    h = jnp.sin(_dot3(pe, w1_ref[...]) + b1_ref[...])
    h = jnp.sin(_dot3(h, w2_ref[...]) + b2_ref[...])
    h = _dot3(h, w3_ref[...])
    dist = pe[:, LANE - 1:LANE]
    h = h * jnp.exp(-dist * dec_ref[...])
    o_ref[...] = h / jnp.sum(jnp.abs(h), axis=0, keepdims=True)


def _hy_filters(pe, w1, b1, w2, b2, w3, decay):
    l = pe.shape[0]
    n = w3.shape[1]
    args = (pe, w1, b1, w2, b2, w3, decay)
    return pl.pallas_call(
        _hy_filter_kernel,
        out_shape=jax.ShapeDtypeStruct((l, n), F32),
        grid=(1,),
        in_specs=[pl.BlockSpec(a.shape, lambda i: (0, 0)) for a in args],
        out_specs=pl.BlockSpec((l, n), lambda i: (0, 0)),
        compiler_params=_params(("arbitrary",)),
        name="hy_filters",
    )(*args)


def _hy_filter_dft_kernel(h_ref, ct_ref, st_ref, hc_ref, hs_ref, ht_s):
    @pl.when(pl.program_id(0) == 0)
    def _():
        ht_s[...] = h_ref[...].T.astype(BF16)

    hc_ref[...] = _dot(ht_s[...], ct_ref[...])
    hs_ref[...] = _dot(ht_s[...], st_ref[...])


def _hy_filter_dft(h, tables):
    l, n = h.shape
    ct, st, _, _ = tables
    nk = ct.shape[0]
    spec = jax.ShapeDtypeStruct((n, nk * FREQ_TILE), F32)
    return pl.pallas_call(
        _hy_filter_dft_kernel,
        out_shape=[spec, spec],
        grid=(nk,),
        in_specs=[pl.BlockSpec((l, n), lambda k: (0, 0)),
                  pl.BlockSpec((None, l, FREQ_TILE), lambda k: (k, 0, 0)),
                  pl.BlockSpec((None, l, FREQ_TILE), lambda k: (k, 0, 0))],
        out_specs=[pl.BlockSpec((n, FREQ_TILE), lambda k: (0, k)), pl.BlockSpec((n, FREQ_TILE), lambda k: (0, k))],
        scratch_shapes=[pltpu.VMEM((n, l), BF16)],
        compiler_params=_params(("arbitrary",)),
        name="hy_filter_dft",
    )(h, ct, st)


def _hy_conv_kernel(z_ref, x_ref, hc_ref, hs_ref, skip_ref, wz_ref, bz_ref, wx_ref, bx_ref,
                    ct_ref, st_ref, wc_ref, ws_ref, o_ref, zt_s, *, bb, nk, z_raw):
    kt = pl.program_id(1)
    c = GROUP_W

    def z_of(b):
        return _short_conv3(z_ref[b], wz_ref, bz_ref) if z_raw else z_ref[b]

    @pl.when(kt == 0)
    def _():
        o_ref[...] = jnp.zeros_like(o_ref)
        for b in range(bb):
            zt_s[b * c:(b + 1) * c, :] = z_of(b).T.astype(BF16)

    zt = zt_s[...]
    xc = _dot(zt, ct_ref[...])
    xs = _dot(zt, st_ref[...])
    hc = hc_ref[...]
    hs = hs_ref[...]
    wc = wc_ref[...]
    ws = ws_ref[...]
    for b in range(bb):
        zc = xc[b * c:(b + 1) * c, :]
        zsn = xs[b * c:(b + 1) * c, :]
        yc = zc * hc - zsn * hs
        ys = zc * hs + zsn * hc
        a = (wc * yc + ws * ys).T.astype(BF16)
        bm = (wc * ys - ws * yc).T.astype(BF16)
        o_ref[b] += _dot(ct_ref[...], a) + _dot(st_ref[...], bm)

    @pl.when(kt == nk - 1)
    def _():
        for b in range(bb):
            gate = _short_conv3(x_ref[b], wx_ref, bx_ref)
            o_ref[b] = gate * (o_ref[b] + skip_ref[...] * z_of(b))


def _hy_long_conv(pc, zprev, z_col, gate_col, cw, cb, hspec, h_col, skip, skip_col, tables, bb):
    b, l, _ = pc.shape
    c = GROUP_W
    ct, st, wc, ws = tables
    nk = ct.shape[0]
    z_raw = zprev is None
    z_arr = pc if z_raw else zprev
    return pl.pallas_call(
        functools.partial(_hy_conv_kernel, bb=bb, nk=nk, z_raw=z_raw),
        out_shape=jax.ShapeDtypeStruct((b, l, c), F32),
        grid=(b // bb, nk),
        in_specs=[
            pl.BlockSpec((bb, l, c), lambda g, k: (g, 0, z_col)),
            pl.BlockSpec((bb, l, c), lambda g, k: (g, 0, gate_col)),
            pl.BlockSpec((c, FREQ_TILE), lambda g, k: (h_col, k)),
            pl.BlockSpec((c, FREQ_TILE), lambda g, k: (h_col, k)),
            pl.BlockSpec((1, c), lambda g, k: (0, skip_col)),
            pl.BlockSpec((3, c), lambda g, k: (0, z_col)),
            pl.BlockSpec((1, c), lambda g, k: (0, z_col)),
            pl.BlockSpec((3, c), lambda g, k: (0, gate_col)),
            pl.BlockSpec((1, c), lambda g, k: (0, gate_col)),
            pl.BlockSpec((None, l, FREQ_TILE), lambda g, k: (k, 0, 0)),
            pl.BlockSpec((None, l, FREQ_TILE), lambda g, k: (k, 0, 0)),
            pl.BlockSpec((1, FREQ_TILE), lambda g, k: (0, k)),
            pl.BlockSpec((1, FREQ_TILE), lambda g, k: (0, k)),
        ],
        out_specs=pl.BlockSpec((bb, l, c), lambda g, k: (g, 0, 0)),
        scratch_shapes=[pltpu.VMEM((bb * c, l), BF16)],
        compiler_params=_params(("parallel", "arbitrary")),
        name="hy_long_conv",
    )(z_arr, pc, hspec[0], hspec[1], skip, cw, cb, cw, cb, ct, st, wc, ws)


def _dft_tables(l):
    n = 3 * l // 2
    nf = n // 2 + 1
    nfp = -(-nf // FREQ_TILE) * FREQ_TILE
    k = jnp.arange(nfp, dtype=jnp.int32)
    ta = jnp.arange(l // TWID, dtype=jnp.int32) * TWID
    tb = jnp.arange(TWID, dtype=jnp.int32)
    live = (k < nf)

    def cos_sin(m):
        ang = (m % n).astype(F32) * (2.0 * math.pi / n)
        return jnp.cos(ang), jnp.sin(ang)

    def tiles(x):
        return x.reshape(x.shape[0], nfp // FREQ_TILE, FREQ_TILE).transpose(1, 0, 2)

    ca, sa = cos_sin(ta[:, None] * k[None, :])
    cb, sb = cos_sin(tb[:, None] * k[None, :])
    ca, sa = (tiles(jnp.where(live[None, :], x, 0.0))[:, :, None, :] for x in (ca, sa))
    cb, sb = tiles(cb)[:, None, :, :], tiles(sb)[:, None, :, :]
    shape = (nfp // FREQ_TILE, l, FREQ_TILE)
    ct = (ca * cb - sa * sb).reshape(shape).astype(BF16)
    st = (sa * cb + ca * sb).reshape(shape).astype(BF16)
    wk = jnp.where((k == 0) | (k == n // 2), 1.0, 2.0) / n
    wk = jnp.where(live, wk, 0.0).astype(F32)
    cp, sp = cos_sin(k * (l // 2))
    return ct, st, (wk * cp)[None, :], (wk * sp)[None, :]


def _hy_pos_features(l):
    pos = jnp.arange(l, dtype=F32)
    t = pos / (l - 1)
    ang = (2.0 * jnp.pi * pos / l)[:, None] * jnp.linspace(1e-4, HY_POS_BANDS - 1, HY_POS_BANDS, dtype=F32)[None, :]
    half = l // 2
    dist = jnp.abs(pos - half) / half
    pe = jnp.concatenate([t[:, None], jnp.cos(ang), -jnp.sin(ang)], axis=-1)
    pad = jnp.zeros((l, LANE - 1 - pe.shape[1]), F32)
    return jnp.concatenate([pe, pad, dist[:, None]], axis=-1)


def _grid_position_encoding(n_pos, dim):
    quarter = dim // 4
    omega = 1.0 / (POS_BASE ** (jnp.arange(quarter, dtype=F32) / quarter))
    ang = jnp.arange(n_pos, dtype=F32)[:, None] * omega[None, :]
    return jnp.concatenate([jnp.sin(ang), jnp.cos(ang)], axis=-1)


def _blockdiag2(a, b):
    z = jnp.zeros_like(a)
    return jnp.concatenate([jnp.concatenate([a, z], axis=1), jnp.concatenate([z, b], axis=1)], axis=0)


def _layer_params(l, norm1_g, norm2_g, gla_norm_g,
                  rg_conv_w, rg_conv_b, rg_w_a, rg_b_a, rg_w_x, rg_b_x, rg_lambda,
                  hy_conv_w, hy_conv_b, hy_w1, hy_b1, hy_w2, hy_b2, hy_w3, hy_decay, hy_skip, hg_norm_g):
    wa, wx = rg_w_a[l], rg_w_x[l]
    rg_w = jnp.stack([
        jnp.concatenate([_blockdiag2(m[dd, 2 * j], m[dd, 2 * j + 1]) for dd in range(2) for m in (wa, wx)], axis=1)
        for j in range(2)])
    ba, bx = rg_b_a[l], rg_b_x[l]
    rg_bias = jnp.stack([
        jnp.concatenate([v[dd, j * RG_HALF:(j + 1) * RG_HALF] for dd in range(2) for v in (ba, bx)])[None, :]
        for j in range(2)])
    nsp = -RG_C * jax.nn.softplus(-rg_lambda[l])
    w1p = jnp.concatenate([hy_w1[l], jnp.zeros((LANE - hy_w1.shape[1], HY_FFN_W), F32)], axis=0)
    return dict(
        norm1=norm1_g[l][None, :], norm2=norm2_g[l][None, :],
        gla_gain=gla_norm_g[l][None, :], hg_gain=hg_norm_g[l][None, :],
        rg_cw=rg_conv_w[l], rg_cb=rg_conv_b[l][None, :], rg_w=rg_w.astype(BF16), rg_bias=rg_bias, rg_nsp=nsp,
        hy_cw=hy_conv_w[l], hy_cb=hy_conv_b[l][None, :], hy_w1=w1p, hy_b1=hy_b1[l][None, :],
        hy_w2=hy_w2[l], hy_b2=hy_b2[l][None, :], hy_w3=hy_w3[l], hy_decay=hy_decay[l][None, :],
        hy_skip=hy_skip[l][None, :])


def _stacked_weights(hg_lb, w_in, w_out, gla_w_gate, gla_b_gate, ffn_w1, ffn_w3, ffn_w2):
    depth, d, _ = w_in.shape
    w = GROUP_W
    n_gla = 4 * w + GLA_LOWRANK
    w_proj = jnp.concatenate([w_in[:, :, :n_gla], jnp.zeros((depth, d, LR_PAD - GLA_LOWRANK), F32),
                              w_in[:, :, n_gla:]], axis=2)
    wg = jnp.concatenate([gla_w_gate, jnp.zeros((depth, 2, LR_PAD - GLA_LOWRANK, w), F32)], axis=2)
    zrows = jnp.zeros((depth, SUBLANE - 2, w), F32)
    par_gla = jnp.concatenate([gla_b_gate, zrows], axis=1)
    par_hg = jnp.concatenate([jnp.stack([1.0 - hg_lb, jnp.log(hg_lb), jnp.log1p(-hg_lb)], axis=1),
                              jnp.zeros((depth, SUBLANE - 3, w), F32)], axis=1)
    return dict(w_proj=w_proj.astype(BF16), wg=wg.astype(BF16), par_gla=par_gla, par_hg=par_hg,
                w_out=w_out.astype(BF16), w1=ffn_w1.astype(BF16), w3=ffn_w3.astype(BF16), w2=ffn_w2.astype(BF16))


def _trunk_layer(x, p, sw, mod4, layer, row0, seq_shape, s_gla, s_rg, s_hg, stream_consts, final, final_g):
    bm, lm, d = x.shape
    b, l = seq_shape
    gated_consts, pe, tables, bb = stream_consts
    outs = _norm_proj(x, mod4, layer, row0, p["norm1"], sw["w_proj"], sw["wg"], sw["par_gla"], sw["par_hg"])
    pa, pa_la, pb, pc, pd, pd_la = (t.reshape(b, l, t.shape[-1]) for t in outs)

    ya, st_a = _gated_mixer(pa, pa_la, p["gla_gain"], _state_to_blockdiag_t(s_gla), gated_consts, "gla")
    yd, st_d = _gated_mixer(pd, pd_la, p["hg_gain"], _state_to_blockdiag_t(s_hg), gated_consts, "hg")
    yb, st_b = _rglru(pb, p["rg_cw"], p["rg_cb"], p["rg_w"], p["rg_bias"], p["rg_nsp"], s_rg)

    filt = _hy_filters(pe, p["hy_w1"], p["hy_b1"], p["hy_w2"], p["hy_b2"], p["hy_w3"], p["hy_decay"])
    hspec = _hy_filter_dft(filt, tables)
    z1 = _hy_long_conv(pc, None, 0, 1, p["hy_cw"], p["hy_cb"], hspec, 0, p["hy_skip"], 0, tables, bb)
    yc = _hy_long_conv(pc, z1, 0, 2, p["hy_cw"], p["hy_cb"], hspec, 1, p["hy_skip"], 1, tables, bb)

    ys = [t.reshape(bm, lm, GROUP_W) for t in (ya, yb, yc, yd)]
    x = _out_ffn(x, ys, mod4, layer, row0, p["norm2"], final_g, sw["w_out"], sw["w1"], sw["w3"], sw["w2"], final)
    return x, (_blockdiag_t_to_state(st_a), st_b, _blockdiag_t_to_state(st_d))


def kernel(x_prompt, x_sample, state_gla, state_rglru, state_hgrn, c, c_ctx, norm1_g, norm2_g, final_norm_g, w_mod, b_mod, w_in, w_out, gla_w_gate, gla_b_gate, gla_norm_g, rg_conv_w, rg_conv_b, rg_w_a, rg_b_a, rg_w_x, rg_b_x, rg_lambda, hy_conv_w, hy_conv_b, hy_w1, hy_b1, hy_w2, hy_b2, hy_w3, hy_decay, hy_skip, hg_lower, hg_norm_g, ffn_w1, ffn_w3, ffn_w2):
    depth = w_in.shape[0]
    nb, seq, d = x_prompt.shape
    db, dseq, _ = x_sample.shape

    hg_lb = jnp.cumsum(jax.nn.softmax(hg_lower.astype(F32), axis=0), axis=0)
    hg_lb = hg_lb - hg_lb[0:1]

    cvec = jnp.concatenate([c_ctx[None, :], c, jnp.zeros((SUBLANE - 1 - db, d), F32)], axis=0)
    mod4 = _modulation(cvec, w_mod, b_mod).reshape(depth, SUBLANE, 1, N_MOD * d)

    gated_consts = _gated_consts()
    consts_p = (gated_consts, _hy_pos_features(seq), _dft_tables(seq), 16)
    consts_s = (gated_consts, _hy_pos_features(dseq), _dft_tables(dseq), 1)

    xp = x_prompt.reshape(1, nb * seq, d)
    xs = _add_pos(x_sample, _grid_position_encoding(max(dseq // GRID_W, GRID_W), d))
    zero_gla = jnp.zeros((nb, 2, N_HEADS, HEAD_D, HEAD_D), F32)
    zero_rg = jnp.zeros((nb, 2, GROUP_W), F32)
    final_g = final_norm_g[None, :]

    sw = _stacked_weights(hg_lb, w_in, w_out, gla_w_gate, gla_b_gate, ffn_w1, ffn_w3, ffn_w2)
    gla_states, rg_states, hg_states = [], [], []
    for l in range(depth):
        p = _layer_params(l, norm1_g, norm2_g, gla_norm_g,
                          rg_conv_w, rg_conv_b, rg_w_a, rg_b_a, rg_w_x, rg_b_x, rg_lambda,
                          hy_conv_w, hy_conv_b, hy_w1, hy_b1, hy_w2, hy_b2, hy_w3, hy_decay, hy_skip, hg_norm_g)
        final = l == depth - 1
        xp, (sg, sr, sh) = _trunk_layer(xp, p, sw, mod4, l, 0, (nb, seq), zero_gla, zero_rg, zero_gla,
                                        consts_p, final, final_g)
        xs, _ = _trunk_layer(xs, p, sw, mod4, l, 1, (db, dseq), state_gla[:, l], state_rglru[:, l],
                             state_hgrn[:, l], consts_s, final, final_g)
        gla_states.append(sg)
        rg_states.append(sr)
        hg_states.append(sh)

    return (xp.reshape(nb, seq, d), xs,
            jnp.stack(gla_states, axis=1), jnp.stack(rg_states, axis=1), jnp.stack(hg_states, axis=1))
```

```python
import functools
import math

import jax
import jax.numpy as jnp
from jax import lax
from jax.experimental import pallas as pl
from jax.experimental.pallas import tpu as pltpu

F32 = jnp.float32
BF16 = jnp.bfloat16

D_MODEL = 1024
N_MOD = 6
GROUP_W = 256
N_HEADS = 4
HEAD_D = GROUP_W // N_HEADS
GLA_LOWRANK = 16
GLA_GATE_TEMP = 16.0
RG_C = 8.0
RG_CONV_W = 4
HY_POS_BANDS = 16
HY_FFN_W = 64
D_FF = 2816
EPS = 1e-6
GRID_W = 64
POS_BASE = 10000.0

LANE = 128
SUBLANE = 8
VMEM_LIMIT = 56 * 1024 * 1024

LOG2E = 1.0 / math.log(2.0)
FAST_LOG2_RANGE = 96.0
SUB = 16
LC = 256
N_SUB = LC // SUB
BOUNDED_BLKS = (64, 32)
LR_PAD = LANE
W_GLA = 4 * GROUP_W + LR_PAD
W_RG = 2 * GROUP_W
W_HY = 3 * GROUP_W
W_HG = 5 * GROUP_W
W_PROJ = W_GLA + W_RG + W_HY + W_HG
W_GLA_MM = 4 * GROUP_W
W_HG_MM = 5 * GROUP_W
W_DECAY = 2 * GROUP_W
GATED_COLS = {"gla": dict(q=0, k=(1, 1), v=2, gate=3, la=(0, 1)),
              "hg": dict(q=0, k=(1, 2), v=3, gate=4, la=(0, 1))}
TM = 512
TF = D_FF // 2
TN_MOD = 512
RG_HALF = GROUP_W // 2
RG_ROWS = 256
FREQ_TILE = 256
TWID = 64


def _dot(a, b):
    return jnp.dot(a, b, preferred_element_type=F32)


def _split(x):
    hi = x.astype(BF16)
    lo = (x - hi.astype(F32)).astype(BF16)
    return hi, lo


def _dot_x2(x, w):
    hi, lo = _split(x)
    return _dot(hi, w) + _dot(lo, w)


def _dot_w2(w, x):
    hi, lo = _split(x)
    return _dot(w, hi) + _dot(w, lo)


def _dot3(a, b):
    ah, al = _split(a)
    bh, bl = _split(b)
    return _dot(ah, bh) + _dot(al, bh) + _dot(ah, bl)


def _sigmoid(x):
    return 0.5 * jnp.tanh(0.5 * x) + 0.5


def _silu(x):
    return x * _sigmoid(x)


def _log1p_exp_neg_abs(x):
    return jnp.log(1.0 + jnp.exp(-jnp.abs(x)))


def _log_sigmoid(x):
    return jnp.minimum(x, 0.0) - _log1p_exp_neg_abs(x)


def _rms(x):
    return x * lax.rsqrt(jnp.mean(x * x, axis=-1, keepdims=True) + EPS)


def _params(sem, vmem=VMEM_LIMIT):
    return pltpu.CompilerParams(dimension_semantics=sem, vmem_limit_bytes=vmem)


def _mod_kernel(c_ref, w_ref, b_ref, o_ref):
    c = c_ref[...]
    o_ref[0] = _dot3(_silu(c), w_ref[0]) + b_ref[0]


def _modulation(cvec, w_mod, b_mod):
    depth = w_mod.shape[0]
    n = N_MOD * D_MODEL
    return pl.pallas_call(
        _mod_kernel,
        out_shape=jax.ShapeDtypeStruct((depth, SUBLANE, n), F32),
        grid=(depth, n // TN_MOD),
        in_specs=[
            pl.BlockSpec((SUBLANE, D_MODEL), lambda l, j: (0, 0)),
            pl.BlockSpec((1, D_MODEL, TN_MOD), lambda l, j: (l, 0, j)),
            pl.BlockSpec((1, 1, TN_MOD), lambda l, j: (l, 0, j)),
        ],
        out_specs=pl.BlockSpec((1, SUBLANE, TN_MOD), lambda l, j: (l, 0, j)),
        compiler_params=_params(("parallel", "parallel")),
        name="modulation",
    )(cvec, w_mod, b_mod.reshape(depth, 1, n))


def _add_kernel(x_ref, e_ref, o_ref):
    j = pl.program_id(1)
    half = D_MODEL // 2
    enc = e_ref[0:GRID_W, :]
    for r in range(TM // GRID_W):
        rows = slice(r * GRID_W, (r + 1) * GRID_W)
        enc_row = e_ref[pl.ds(j * (TM // GRID_W) + r, 1), :]
        o_ref[0, rows, 0:half] = x_ref[0, rows, 0:half] + enc_row
        o_ref[0, rows, half:] = x_ref[0, rows, half:] + enc


def _add_pos(x, enc):
    b, l, d = x.shape
    assert l // GRID_W <= enc.shape[0]
    return pl.pallas_call(
        _add_kernel,
        out_shape=jax.ShapeDtypeStruct(x.shape, F32),
        grid=(b, l // TM),
        in_specs=[pl.BlockSpec((1, TM, d), lambda i, j: (i, j, 0)),
                  pl.BlockSpec(enc.shape, lambda i, j: (0, 0))],
        out_specs=pl.BlockSpec((1, TM, d), lambda i, j: (i, j, 0)),
        compiler_params=_params(("parallel", "parallel")),
        name="add_pos",
    )(x, enc)


def _gla_features(p, wg, par):
    w = GROUP_W
    lr = p[:, 4 * w:4 * w + LR_PAD].astype(BF16)
    las = [_log_sigmoid(_dot(lr, wg[d]) + par[d:d + 1, :]) * (LOG2E / GLA_GATE_TEMP) for d in range(2)]
    return [p[:, 0:w] * (HEAD_D ** -0.5), p[:, w:2 * w], p[:, 2 * w:3 * w], _silu(p[:, 3 * w:4 * w])] + las


def _hg_features(p, par):
    w = GROUP_W
    ks, las = [], []
    for d in range(2):
        f = p[:, (1 + d) * w:(2 + d) * w]
        e = jnp.exp(-jnp.abs(f))
        ks.append(par[0:1, :] * (jnp.where(f >= 0.0, e, 1.0) / (1.0 + e)))
        y = par[2:3, :] + (jnp.minimum(f, 0.0) - jnp.log(1.0 + e))
        lb = par[1:2, :]
        las.append((jnp.maximum(lb, y) + _log1p_exp_neg_abs(lb - y)) * LOG2E)
    return [_silu(p[:, 0:w]), ks[0], ks[1], p[:, 3 * w:4 * w], _silu(p[:, 4 * w:5 * w]), las[0], las[1]]


def _proj_kernel(x_ref, mod_ref, g_ref, w_ref, wg_ref, pg_ref, ph_ref,
                 oa_ref, oal_ref, ob_ref, oc_ref, od_ref, odl_ref):
    x = x_ref[0]
    m = mod_ref[0, 0]
    sh = m[:, 0:D_MODEL]
    sc = m[:, D_MODEL:2 * D_MODEL]
    u = (_rms(x) * g_ref[...] * (1.0 + sc) + sh).astype(BF16)
    w = GROUP_W
    bounds = (0, W_GLA, W_GLA + W_RG, W_GLA + W_RG + W_HY, W_PROJ)
    p = _dot(u, w_ref[0])
    p_gla, p_rg, p_hy, p_hg = (p[:, lo:hi] for lo, hi in zip(bounds[:-1], bounds[1:]))

    def put(feats, mm_ref, decay_ref):
        n_mm = len(feats) - 2
        for i, t in enumerate(feats[:n_mm]):
            mm_ref[0, :, i * w:(i + 1) * w] = t.astype(BF16)
        for i, t in enumerate(feats[n_mm:]):
            decay_ref[0, :, i * w:(i + 1) * w] = t

    put(_gla_features(p_gla, wg_ref[0], pg_ref[0]), oa_ref, oal_ref)
    ob_ref[0] = p_rg
    oc_ref[0] = p_hy
    put(_hg_features(p_hg, ph_ref[0]), od_ref, odl_ref)


def _norm_proj(x, mod4, layer, row0, gain, w, wg, par_gla, par_hg):
    bm, lm, d = x.shape
    widths = (W_GLA_MM, W_DECAY, W_RG, W_HY, W_HG_MM, W_DECAY)
    dtypes = (BF16, F32, F32, F32, BF16, F32)
    per_layer = lambda a: pl.BlockSpec((1,) + a.shape[1:], lambda i, j: (layer,) + (0,) * (a.ndim - 1))
    return pl.pallas_call(
        _proj_kernel,
        out_shape=[jax.ShapeDtypeStruct((bm, lm, wd), dt) for wd, dt in zip(widths, dtypes)],
        grid=(bm, lm // TM),
        in_specs=[
            pl.BlockSpec((1, TM, d), lambda i, j: (i, j, 0)),
            pl.BlockSpec((1, 1, 1, N_MOD * d), lambda i, j: (layer, row0 + i, 0, 0)),
            pl.BlockSpec((1, d), lambda i, j: (0, 0)),
            per_layer(w), per_layer(wg), per_layer(par_gla), per_layer(par_hg),
        ],
        out_specs=[pl.BlockSpec((1, TM, wd), lambda i, j: (i, j, 0)) for wd in widths],
        compiler_params=_params(("parallel", "parallel")),
        name="norm_proj",
    )(x, mod4, gain, w, wg, par_gla, par_hg)


def _ffn_kernel(x_ref, ya_ref, yb_ref, yc_ref, yd_ref, mod_ref, g2_ref, gf_ref,
                wo_ref, w1_ref, w3_ref, w2_ref, o_ref, x1_s, u_s, acc_s, *, nf, final):
    f = pl.program_id(2)
    d = D_MODEL

    @pl.when(f == 0)
    def _():
        m = mod_ref[0, 0]
        g1 = m[:, 2 * d:3 * d]
        sh2 = m[:, 3 * d:4 * d]
        sc2 = m[:, 4 * d:5 * d]
        y = jnp.concatenate([ya_ref[0], yb_ref[0], yc_ref[0], yd_ref[0]], axis=-1)
        x1 = x_ref[0] + g1 * _dot(y.astype(BF16), wo_ref[...])
        x1_s[...] = x1
        u_s[...] = (_rms(x1) * g2_ref[...] * (1.0 + sc2) + sh2).astype(BF16)
        acc_s[...] = jnp.zeros_like(acc_s)

    u = u_s[...]
    h = _silu(_dot(u, w1_ref[...])) * _dot(u, w3_ref[...])
    acc_s[...] += _dot(h.astype(BF16), w2_ref[...])

    @pl.when(f == nf - 1)
    def _():
        g2 = mod_ref[0, 0][:, 5 * d:6 * d]
        xo = x1_s[...] + g2 * acc_s[...]
        if final:
            xo = _rms(xo) * gf_ref[...]
        o_ref[0] = xo


def _out_ffn(x, ys, mod4, layer, row0, g2, gf, wo, w1, w3, w2, final):
    bm, lm, d = x.shape
    nf = D_FF // TF
    tok = lambda wd: pl.BlockSpec((1, TM, wd), lambda i, j, f: (i, j, 0))
    return pl.pallas_call(
        functools.partial(_ffn_kernel, nf=nf, final=final),
        out_shape=jax.ShapeDtypeStruct(x.shape, F32),
        grid=(bm, lm // TM, nf),
        in_specs=[
            tok(d), tok(GROUP_W), tok(GROUP_W), tok(GROUP_W), tok(GROUP_W),
            pl.BlockSpec((1, 1, 1, N_MOD * d), lambda i, j, f: (layer, row0 + i, 0, 0)),
            pl.BlockSpec((1, d), lambda i, j, f: (0, 0)),
            pl.BlockSpec((1, d), lambda i, j, f: (0, 0)),
            pl.BlockSpec((None, d, d), lambda i, j, f: (layer, 0, 0)),
            pl.BlockSpec((None, d, TF), lambda i, j, f: (layer, 0, f)),
            pl.BlockSpec((None, d, TF), lambda i, j, f: (layer, 0, f)),
            pl.BlockSpec((None, TF, d), lambda i, j, f: (layer, f, 0)),
        ],
        out_specs=tok(d),
        scratch_shapes=[pltpu.VMEM((TM, d), F32), pltpu.VMEM((TM, d), BF16), pltpu.VMEM((TM, d), F32)],
        compiler_params=_params(("parallel", "parallel", "arbitrary")),
        name="out_ffn",
    )(x, *ys, mod4, g2, gf, wo, w1, w3, w2)


def _operand(p_ref, mode, name, d, rows):
    col = GATED_COLS[mode][name]
    col = col[d] if isinstance(col, tuple) else col
    ref = p_ref[1] if name == "la" else p_ref[0]
    return ref[0, rows, col * GROUP_W:(col + 1) * GROUP_W].astype(F32)


def _gated_subblock(d, i, p_ref, mode, ones_ref, mbd_ref, b_s, p_s, o_s, st_s):
    w = GROUP_W
    rev = d == 1
    tio = lax.broadcasted_iota(jnp.int32, (SUB, w), 0)
    edge = 0 if rev else SUB - 1
    rows = pl.ds(pl.multiple_of(i * SUB, SUB), SUB)
    bb = b_s[d, rows, :]
    qb = _operand(p_ref, mode, "q", d, rows)
    kb = _operand(p_ref, mode, "k", d, rows)
    vb = _operand(p_ref, mode, "v", d, rows)

    def row(a, s):
        return jnp.broadcast_to(a[s:s + 1, :], (SUB, w))

    for s in range(SUB):
        valid = (tio <= s) if rev else (tio >= s)
        e = jnp.exp2(jnp.where(valid, bb - row(bb, s), -jnp.inf))
        p_s[d, s * SUB:(s + 1) * SUB, :] = (e * qb * row(kb, s)).astype(BF16)
    r = _dot(p_s[d], ones_ref[...])
    od = jnp.zeros((SUB, w), F32)
    for s in range(SUB):
        od = od + r[s * SUB:(s + 1) * SUB, :] * row(vb, s)
    bend = row(bb, edge)
    st = st_s[d]
    qt = (qb * jnp.exp2(bb)).astype(BF16)
    oi = lax.dot_general(qt, st.astype(BF16), (((1,), (1,)), ((), ())), preferred_element_type=F32)
    kt = (kb * jnp.exp2(bend - bb)).astype(BF16)
    kv = lax.dot_general(vb.astype(BF16), kt, (((0,), (0,)), ((), ())), preferred_element_type=F32)
    st_s[d] = st * jnp.exp2(bend[0:1, :]) + kv * mbd_ref[...]
    o_s[d, rows, :] = od + oi


def _stack_heads(x):
    lane_head = lax.broadcasted_iota(jnp.int32, x.shape, 1) // HEAD_D
    return jnp.concatenate([jnp.where(lane_head == h, x, 0.0) for h in range(N_HEADS)], axis=0).astype(BF16)


def _centred_span(b_s, blk):
    span = None
    for d in range(2):
        for g in range(LC // blk):
            r0, rm, r1 = g * blk, g * blk + blk // 2, (g + 1) * blk - 1
            mid = b_s[d, rm:rm + 1, :]
            m = jnp.maximum(jnp.abs(b_s[d, r0:r0 + 1, :] - mid), jnp.abs(b_s[d, r1:r1 + 1, :] - mid))
            span = m if span is None else jnp.maximum(span, m)
    return jnp.max(span)


def _gated_block_bounded(d, g, blk, p_ref, mode, b_s, o_s, st_s):
    w = GROUP_W
    rev = d == 1
    edge = 0 if rev else blk - 1
    rows = slice(g * blk, (g + 1) * blk)
    bb = b_s[d, rows, :]
    qb = _operand(p_ref, mode, "q", d, rows)
    kb = _operand(p_ref, mode, "k", d, rows)
    cc = bb - jnp.broadcast_to(bb[blk // 2:blk // 2 + 1, :], bb.shape)
    vm = _stack_heads(_operand(p_ref, mode, "v", d, rows))
    km = _stack_heads(kb * jnp.exp2(-cc))
    sc = lax.dot_general((qb * jnp.exp2(cc)).astype(BF16), km, (((1,), (1,)), ((), ())),
                         preferred_element_type=F32)
    t_i = lax.broadcasted_iota(jnp.int32, sc.shape, 0)
    s_i = lax.broadcasted_iota(jnp.int32, sc.shape, 1) % blk
    od = _dot(jnp.where((t_i <= s_i) if rev else (t_i >= s_i), sc, 0.0).astype(BF16), vm)
    bend = jnp.broadcast_to(bb[edge:edge + 1, :], (blk, w))
    st = st_s[d]
    oi = lax.dot_general((qb * jnp.exp2(bb)).astype(BF16), st.astype(BF16), (((1,), (1,)), ((), ())),
                         preferred_element_type=F32)
    ktm = _stack_heads(kb * jnp.exp2(bend - bb))
    kv = lax.dot_general(vm, ktm, (((0,), (0,)), ((), ())), preferred_element_type=F32)
    st_s[d] = st * jnp.exp2(bend[0:1, :]) + kv
    o_s[d, rows, :] = od + oi


def _gated_kernel(pf_mm_ref, pf_la_ref, pb_mm_ref, pb_la_ref, gain_ref, s0_ref, tri_ref, ones_ref, mbd_ref,
                  y_ref, st_ref, b_s, p_s, o_s, st_s, *, mode, nc):
    j = pl.program_id(1)
    pf_ref = (pf_mm_ref, pf_la_ref)
    pb_ref = (pb_mm_ref, pb_la_ref)
    p_refs = (pf_ref, pb_ref)
    all_rows = slice(0, LC)

    @pl.when(j == 0)
    def _():
        st_s[0] = s0_ref[0, 0]
        st_s[1] = s0_ref[0, 1]

    def cumulate(level):
        for d in range(2):
            b_s[d] = _dot_w2(tri_ref[level, d], _operand(p_refs[d], mode, "la", d, all_rows))

    def run_bounded(blk):
        for g in range(LC // blk):
            _gated_block_bounded(0, g, blk, pf_ref, mode, b_s, o_s, st_s)
            _gated_block_bounded(1, LC // blk - 1 - g, blk, pb_ref, mode, b_s, o_s, st_s)

    def run_exact():
        def body(it, carry):
            _gated_subblock(0, it, pf_ref, mode, ones_ref, mbd_ref, b_s, p_s, o_s, st_s)
            _gated_subblock(1, N_SUB - 1 - it, pb_ref, mode, ones_ref, mbd_ref, b_s, p_s, o_s, st_s)
            return carry

        lax.fori_loop(0, N_SUB, body, 0)

    cumulate(0)
    wide = _centred_span(b_s, BOUNDED_BLKS[0]) < FAST_LOG2_RANGE

    @pl.when(wide)
    def _():
        run_bounded(BOUNDED_BLKS[0])

    @pl.when(jnp.logical_not(wide))
    def _():
        cumulate(1)
        narrow = _centred_span(b_s, BOUNDED_BLKS[1]) < FAST_LOG2_RANGE

        @pl.when(narrow)
        def _():
            run_bounded(BOUNDED_BLKS[1])

        @pl.when(jnp.logical_not(narrow))
        def _():
            cumulate(2)
            run_exact()

    @pl.when(j == nc - 1)
    def _():
        for d in range(2):
            for h in range(N_HEADS):
                blk = pl.ds(h * HEAD_D, HEAD_D)
                st_ref[0, d, h] = st_s[d, blk, blk]

    def finish(o, p_ref):
        ms = _dot_x2(o * o, ones_ref[...]) * (1.0 / HEAD_D)
        return o * lax.rsqrt(ms + EPS) * gain_ref[...] * _operand(p_ref, mode, "gate", 0, all_rows)

    rows_f = pl.ds(pl.multiple_of(j * LC, LC), LC)
    rows_b = pl.ds(pl.multiple_of((nc - 1 - j) * LC, LC), LC)
    if nc == 1:
        y_ref[0] = finish(o_s[0] + o_s[1], pf_ref)
    else:
        @pl.when(j < nc // 2)
        def _():
            y_ref[0, rows_f, :] = o_s[0]
            y_ref[0, rows_b, :] = o_s[1]

        @pl.when(j >= nc // 2)
        def _():
            y_ref[0, rows_f, :] = finish(y_ref[0, rows_f, :] + o_s[0], pf_ref)
            y_ref[0, rows_b, :] = finish(y_ref[0, rows_b, :] + o_s[1], pb_ref)


def _gated_consts():
    r = jnp.arange(LC)

    def tri(block):
        same = (r[:, None] // block) == (r[None, :] // block)
        lower = (same & (r[None, :] <= r[:, None])).astype(BF16)
        upper = (same & (r[None, :] >= r[:, None])).astype(BF16)
        return jnp.stack([lower, upper])

    head = (r[:, None] // HEAD_D) == (r[None, :] // HEAD_D)
    tris = jnp.stack([tri(blk) for blk in BOUNDED_BLKS + (SUB,)])
    return tris, head.astype(BF16), head.astype(F32)


def _gated_mixer(p_mm, p_la, gain, s0t, consts, mode):
    b, l, width = p_mm.shape
    nc = l // LC
    assert nc == 1 or nc % 2 == 0
    tri, ones, mbd = consts
    w = GROUP_W
    const2 = lambda shape: pl.BlockSpec(shape, lambda i, j: (0,) * len(shape))
    return pl.pallas_call(
        functools.partial(_gated_kernel, mode=mode, nc=nc),
        out_shape=[jax.ShapeDtypeStruct((b, l, w), F32),
                   jax.ShapeDtypeStruct((b, 2, N_HEADS, HEAD_D, HEAD_D), F32)],
        grid=(b, nc),
        in_specs=[
            pl.BlockSpec((1, LC, width), lambda i, j: (i, j, 0)),
            pl.BlockSpec((1, LC, W_DECAY), lambda i, j: (i, j, 0)),
            pl.BlockSpec((1, LC, width), lambda i, j: (i, nc - 1 - j, 0)),
            pl.BlockSpec((1, LC, W_DECAY), lambda i, j: (i, nc - 1 - j, 0)),
            const2(gain.shape),
            pl.BlockSpec((1, 2, w, w), lambda i, j: (i, 0, 0, 0)),
            const2(tri.shape), const2(ones.shape), const2(mbd.shape),
        ],
        out_specs=[pl.BlockSpec((1, l, w), lambda i, j: (i, 0, 0)),
                   pl.BlockSpec((1, 2, N_HEADS, HEAD_D, HEAD_D), lambda i, j: (i, 0, 0, 0, 0))],
        scratch_shapes=[pltpu.VMEM((2, LC, w), F32), pltpu.VMEM((2, SUB * SUB, w), BF16),
                        pltpu.VMEM((2, LC, w), F32), pltpu.VMEM((2, w, w), F32)],
        compiler_params=_params(("parallel", "arbitrary")),
        name="gated_" + mode,
    )(p_mm, p_la, p_mm, p_la, gain, s0t, tri, ones, mbd)


def _state_to_blockdiag_t(s):
    b = s.shape[0]
    st = jnp.swapaxes(s, -1, -2)
    eye = jnp.eye(N_HEADS, dtype=s.dtype)
    full = st[:, :, :, :, None, :] * eye[None, None, :, None, :, None]
    return full.reshape(b, 2, GROUP_W, GROUP_W)


def _blockdiag_t_to_state(st):
    return jnp.swapaxes(st, -1, -2)


def _rglru_kernel(x_ref, g_ref, cw_ref, cb_ref, w_ref, bias_ref, nsp_ref, h0_ref, y_ref, hT_ref,
                  xc_s, a_s, u_s, *, l):
    c = RG_HALF
    x = x_ref[0]
    row = lax.broadcasted_iota(jnp.int32, (l, c), 0)
    xc = x * cw_ref[2:3, :] + cb_ref[...]
    xc = xc + jnp.where(row >= 2, pltpu.roll(x, 2, 0), 0.0) * cw_ref[0:1, :]
    xc = xc + jnp.where(row >= 1, pltpu.roll(x, 1, 0), 0.0) * cw_ref[1:2, :]
    xc = xc + jnp.where(row <= l - 2, pltpu.roll(x, l - 1, 0), 0.0) * cw_ref[3:4, :]
    xc_s[...] = xc

    nslab = l // RG_ROWS
    grp = (RG_ROWS // SUBLANE, SUBLANE, c)
    sub = lax.broadcasted_iota(jnp.int32, grp, 1)

    def slab(n, carry):
        rows = pl.ds(pl.multiple_of(n * RG_ROWS, RG_ROWS), RG_ROWS)
        xs = xc_s[rows, :]
        gates = _sigmoid(_dot(xs.astype(BF16), w_ref[...]) + bias_ref[...])
        for d in range(2):
            r = gates[:, (2 * d) * c:(2 * d + 1) * c]
            i = gates[:, (2 * d + 1) * c:(2 * d + 2) * c]
            log_a = r * nsp_ref[d:d + 1, :]
            a_flat = jnp.exp(log_a)
            a = a_flat.reshape(grp)
            u = (jnp.sqrt(1.0 - a_flat * a_flat) * (i * xs)).reshape(grp)
            for sft in (1, 2, 4):
                if d == 0:
                    ok = sub >= sft
                    a_n, u_n = pltpu.roll(a, sft, 1), pltpu.roll(u, sft, 1)
                else:
                    ok = sub <= SUBLANE - 1 - sft
                    a_n, u_n = pltpu.roll(a, SUBLANE - sft, 1), pltpu.roll(u, SUBLANE - sft, 1)
                u = jnp.where(ok, a * u_n + u, u)
                a = jnp.where(ok, a * a_n, a)
            a_s[d, rows, :] = a.reshape(RG_ROWS, c)
            u_s[d, rows, :] = u.reshape(RG_ROWS, c)
        return carry

    lax.fori_loop(0, nslab, slab, 0)

    ngrp = l // SUBLANE

    def carry_step(n, hs):
        hf, hb = hs
        rows_f = pl.ds(pl.multiple_of(n * SUBLANE, SUBLANE), SUBLANE)
        rows_b = pl.ds(pl.multiple_of((ngrp - 1 - n) * SUBLANE, SUBLANE), SUBLANE)
        hh_f = a_s[0, rows_f, :] * hf + u_s[0, rows_f, :]
        hh_b = a_s[1, rows_b, :] * hb + u_s[1, rows_b, :]
        u_s[0, rows_f, :] = hh_f
        u_s[1, rows_b, :] = hh_b
        return (jnp.broadcast_to(hh_f[SUBLANE - 1:SUBLANE, :], (SUBLANE, c)),
                jnp.broadcast_to(hh_b[0:1, :], (SUBLANE, c)))

    h_init = (jnp.broadcast_to(h0_ref[0, 0:1, :], (SUBLANE, c)), jnp.broadcast_to(h0_ref[0, 1:2, :], (SUBLANE, c)))
    hf, hb = lax.fori_loop(0, ngrp, carry_step, h_init, unroll=2)
    hT_ref[0, 0:1, :] = hf[0:1, :]
    hT_ref[0, 1:2, :] = hb[0:1, :]

    g = g_ref[0]
    gelu = 0.5 * g * (1.0 + jnp.tanh(math.sqrt(2.0 / math.pi) * (g + 0.044715 * (g * g * g))))
    y_ref[0] = (u_s[0] + u_s[1]) * gelu


def _rglru(p, cw, cb, wbd, bias, nsp, h0):
    b, l, _ = p.shape
    c = RG_HALF
    half = lambda shape: pl.BlockSpec(shape, lambda i, j: (0,) * (len(shape) - 1) + (j,))
    return pl.pallas_call(
        functools.partial(_rglru_kernel, l=l),
        out_shape=[jax.ShapeDtypeStruct((b, l, GROUP_W), F32), jax.ShapeDtypeStruct((b, 2, GROUP_W), F32)],
        grid=(b, 2),
        in_specs=[
            pl.BlockSpec((1, l, c), lambda i, j: (i, 0, j)),
            pl.BlockSpec((1, l, c), lambda i, j: (i, 0, 2 + j)),
            half((RG_CONV_W, c)), half((1, c)),
            pl.BlockSpec((None, c, 4 * c), lambda i, j: (j, 0, 0)),
            pl.BlockSpec((None, 1, 4 * c), lambda i, j: (j, 0, 0)),
            half((2, c)),
            pl.BlockSpec((1, 2, c), lambda i, j: (i, 0, j)),
        ],
        out_specs=[pl.BlockSpec((1, l, c), lambda i, j: (i, 0, j)),
                   pl.BlockSpec((1, 2, c), lambda i, j: (i, 0, j))],
        scratch_shapes=[pltpu.VMEM((l, c), F32), pltpu.VMEM((2, l, c), F32), pltpu.VMEM((2, l, c), F32)],
        compiler_params=_params(("parallel", "parallel")),
        name="rglru",
    )(p, p, cw, cb, wbd, bias, nsp, h0)


def _short_conv3(x, w_ref, b_ref):
    l = x.shape[0]
    row = lax.broadcasted_iota(jnp.int32, x.shape, 0)
    o = x * w_ref[1:2, :] + b_ref[...]
    o = o + jnp.where(row >= 1, pltpu.roll(x, 1, 0), 0.0) * w_ref[0:1, :]
    return o + jnp.where(row <= l - 2, pltpu.roll(x, l - 1, 0), 0.0) * w_ref[2:3, :]


def _hy_filter_kernel(pe_ref, w1_ref, b1_ref, w2_ref, b2_ref, w3_ref, dec_ref, o_ref):
    pe = pe_ref[...]
    h = jnp.sin(_dot3(pe, w1_ref[...]) + b1_ref[...])
    h = jnp.sin(_dot3(h, w2_ref[...]) + b2_ref[...])
    h = _dot3(h, w3_ref[...])
    dist = pe[:, LANE - 1:LANE]
    h = h * jnp.exp(-dist * dec_ref[...])
    o_ref[...] = h / jnp.sum(jnp.abs(h), axis=0, keepdims=True)


def _hy_filters(pe, w1, b1, w2, b2, w3, decay):
    l = pe.shape[0]
    n = w3.shape[1]
    args = (pe, w1, b1, w2, b2, w3, decay)
    return pl.pallas_call(
        _hy_filter_kernel,
        out_shape=jax.ShapeDtypeStruct((l, n), F32),
        grid=(1,),
        in_specs=[pl.BlockSpec(a.shape, lambda i: (0, 0)) for a in args],
        out_specs=pl.BlockSpec((l, n), lambda i: (0, 0)),
        compiler_params=_params(("arbitrary",)),
        name="hy_filters",
    )(*args)


def _hy_filter_dft_kernel(h_ref, ct_ref, st_ref, hc_ref, hs_ref, ht_s):
    @pl.when(pl.program_id(0) == 0)
    def _():
        ht_s[...] = h_ref[...].T.astype(BF16)

    hc_ref[...] = _dot(ht_s[...], ct_ref[...])
    hs_ref[...] = _dot(ht_s[...], st_ref[...])


def _hy_filter_dft(h, tables):
    l, n = h.shape
    ct, st, _, _ = tables
    nk = ct.shape[0]
    spec = jax.ShapeDtypeStruct((n, nk * FREQ_TILE), F32)
    return pl.pallas_call(
        _hy_filter_dft_kernel,
        out_shape=[spec, spec],
        grid=(nk,),
        in_specs=[pl.BlockSpec((l, n), lambda k: (0, 0)),
                  pl.BlockSpec((None, l, FREQ_TILE), lambda k: (k, 0, 0)),
                  pl.BlockSpec((None, l, FREQ_TILE), lambda k: (k, 0, 0))],
        out_specs=[pl.BlockSpec((n, FREQ_TILE), lambda k: (0, k)), pl.BlockSpec((n, FREQ_TILE), lambda k: (0, k))],
        scratch_shapes=[pltpu.VMEM((n, l), BF16)],
        compiler_params=_params(("arbitrary",)),
        name="hy_filter_dft",
    )(h, ct, st)


def _hy_conv_kernel(z_ref, x_ref, hc_ref, hs_ref, skip_ref, wz_ref, bz_ref, wx_ref, bx_ref,
                    ct_ref, st_ref, wc_ref, ws_ref, o_ref, zt_s, *, bb, nk, z_raw):
    kt = pl.program_id(1)
    c = GROUP_W

    def z_of(b):
        return _short_conv3(z_ref[b], wz_ref, bz_ref) if z_raw else z_ref[b]

    @pl.when(kt == 0)
    def _():
        o_ref[...] = jnp.zeros_like(o_ref)
        for b in range(bb):
            zt_s[b * c:(b + 1) * c, :] = z_of(b).T.astype(BF16)

    zt = zt_s[...]
    xc = _dot(zt, ct_ref[...])
    xs = _dot(zt, st_ref[...])
    hc = hc_ref[...]
    hs = hs_ref[...]
    wc = wc_ref[...]
    ws = ws_ref[...]
    for b in range(bb):
        zc = xc[b * c:(b + 1) * c, :]
        zsn = xs[b * c:(b + 1) * c, :]
        yc = zc * hc - zsn * hs
        ys = zc * hs + zsn * hc
        a = (wc * yc + ws * ys).T.astype(BF16)
        bm = (wc * ys - ws * yc).T.astype(BF16)
        o_ref[b] += _dot(ct_ref[...], a) + _dot(st_ref[...], bm)

    @pl.when(kt == nk - 1)
    def _():
        for b in range(bb):
            gate = _short_conv3(x_ref[b], wx_ref, bx_ref)
            o_ref[b] = gate * (o_ref[b] + skip_ref[...] * z_of(b))


def _hy_long_conv(pc, zprev, z_col, gate_col, cw, cb, hspec, h_col, skip, skip_col, tables, bb):
    b, l, _ = pc.shape
    c = GROUP_W
    ct, st, wc, ws = tables
    nk = ct.shape[0]
    z_raw = zprev is None
    z_arr = pc if z_raw else zprev
    return pl.pallas_call(
        functools.partial(_hy_conv_kernel, bb=bb, nk=nk, z_raw=z_raw),
        out_shape=jax.ShapeDtypeStruct((b, l, c), F32),
        grid=(b // bb, nk),
        in_specs=[
            pl.BlockSpec((bb, l, c), lambda g, k: (g, 0, z_col), pipeline_mode=pl.Buffered(1)),
            pl.BlockSpec((bb, l, c), lambda g, k: (g, 0, gate_col), pipeline_mode=pl.Buffered(1)),
            pl.BlockSpec((c, FREQ_TILE), lambda g, k: (h_col, k)),
            pl.BlockSpec((c, FREQ_TILE), lambda g, k: (h_col, k)),
            pl.BlockSpec((1, c), lambda g, k: (0, skip_col)),
            pl.BlockSpec((3, c), lambda g, k: (0, z_col)),
            pl.BlockSpec((1, c), lambda g, k: (0, z_col)),
            pl.BlockSpec((3, c), lambda g, k: (0, gate_col)),
            pl.BlockSpec((1, c), lambda g, k: (0, gate_col)),
            pl.BlockSpec((None, l, FREQ_TILE), lambda g, k: (k, 0, 0)),
            pl.BlockSpec((None, l, FREQ_TILE), lambda g, k: (k, 0, 0)),
            pl.BlockSpec((1, FREQ_TILE), lambda g, k: (0, k)),
            pl.BlockSpec((1, FREQ_TILE), lambda g, k: (0, k)),
        ],
        out_specs=pl.BlockSpec((bb, l, c), lambda g, k: (g, 0, 0)),
        scratch_shapes=[pltpu.VMEM((bb * c, l), BF16)],
        compiler_params=_params(("parallel", "arbitrary")),
        name="hy_long_conv",
    )(z_arr, pc, hspec[0], hspec[1], skip, cw, cb, cw, cb, ct, st, wc, ws)


def _dft_tables(l):
    n = 3 * l // 2
    nf = n // 2 + 1
    nfp = -(-nf // FREQ_TILE) * FREQ_TILE
    k = jnp.arange(nfp, dtype=jnp.int32)
    ta = jnp.arange(l // TWID, dtype=jnp.int32) * TWID
    tb = jnp.arange(TWID, dtype=jnp.int32)
    live = (k < nf)

    def cos_sin(m):
        ang = (m % n).astype(F32) * (2.0 * math.pi / n)
        return jnp.cos(ang), jnp.sin(ang)

    def tiles(x):
        return x.reshape(x.shape[0], nfp // FREQ_TILE, FREQ_TILE).transpose(1, 0, 2)

    ca, sa = cos_sin(ta[:, None] * k[None, :])
    cb, sb = cos_sin(tb[:, None] * k[None, :])
    ca, sa = (tiles(jnp.where(live[None, :], x, 0.0))[:, :, None, :] for x in (ca, sa))
    cb, sb = tiles(cb)[:, None, :, :], tiles(sb)[:, None, :, :]
    shape = (nfp // FREQ_TILE, l, FREQ_TILE)
    ct = (ca * cb - sa * sb).reshape(shape).astype(BF16)
    st = (sa * cb + ca * sb).reshape(shape).astype(BF16)
    wk = jnp.where((k == 0) | (k == n // 2), 1.0, 2.0) / n
    wk = jnp.where(live, wk, 0.0).astype(F32)
    cp, sp = cos_sin(k * (l // 2))
    return ct, st, (wk * cp)[None, :], (wk * sp)[None, :]


def _hy_pos_features(l):
    pos = jnp.arange(l, dtype=F32)
    t = pos / (l - 1)
    ang = (2.0 * jnp.pi * pos / l)[:, None] * jnp.linspace(1e-4, HY_POS_BANDS - 1, HY_POS_BANDS, dtype=F32)[None, :]
    half = l // 2
    dist = jnp.abs(pos - half) / half
    pe = jnp.concatenate([t[:, None], jnp.cos(ang), -jnp.sin(ang)], axis=-1)
    pad = jnp.zeros((l, LANE - 1 - pe.shape[1]), F32)
    return jnp.concatenate([pe, pad, dist[:, None]], axis=-1)


def _grid_position_encoding(n_pos, dim):
    quarter = dim // 4
    omega = 1.0 / (POS_BASE ** (jnp.arange(quarter, dtype=F32) / quarter))
    ang = jnp.arange(n_pos, dtype=F32)[:, None] * omega[None, :]
    return jnp.concatenate([jnp.sin(ang), jnp.cos(ang)], axis=-1)


def _blockdiag2(a, b):
    z = jnp.zeros_like(a)
    return jnp.concatenate([jnp.concatenate([a, z], axis=1), jnp.concatenate([z, b], axis=1)], axis=0)


def _layer_params(l, norm1_g, norm2_g, gla_norm_g,
                  rg_conv_w, rg_conv_b, rg_w_a, rg_b_a, rg_w_x, rg_b_x, rg_lambda,
                  hy_conv_w, hy_conv_b, hy_w1, hy_b1, hy_w2, hy_b2, hy_w3, hy_decay, hy_skip, hg_norm_g):
    wa, wx = rg_w_a[l], rg_w_x[l]
    rg_w = jnp.stack([
        jnp.concatenate([_blockdiag2(m[dd, 2 * j], m[dd, 2 * j + 1]) for dd in range(2) for m in (wa, wx)], axis=1)
        for j in range(2)])
    ba, bx = rg_b_a[l], rg_b_x[l]
    rg_bias = jnp.stack([
        jnp.concatenate([v[dd, j * RG_HALF:(j + 1) * RG_HALF] for dd in range(2) for v in (ba, bx)])[None, :]
        for j in range(2)])
    nsp = -RG_C * jax.nn.softplus(-rg_lambda[l])
    w1p = jnp.concatenate([hy_w1[l], jnp.zeros((LANE - hy_w1.shape[1], HY_FFN_W), F32)], axis=0)
    return dict(
        norm1=norm1_g[l][None, :], norm2=norm2_g[l][None, :],
        gla_gain=gla_norm_g[l][None, :], hg_gain=hg_norm_g[l][None, :],
        rg_cw=rg_conv_w[l], rg_cb=rg_conv_b[l][None, :], rg_w=rg_w.astype(BF16), rg_bias=rg_bias, rg_nsp=nsp,
        hy_cw=hy_conv_w[l], hy_cb=hy_conv_b[l][None, :], hy_w1=w1p, hy_b1=hy_b1[l][None, :],
        hy_w2=hy_w2[l], hy_b2=hy_b2[l][None, :], hy_w3=hy_w3[l], hy_decay=hy_decay[l][None, :],
        hy_skip=hy_skip[l][None, :])


def _stacked_weights(hg_lb, w_in, w_out, gla_w_gate, gla_b_gate, ffn_w1, ffn_w3, ffn_w2):
    depth, d, _ = w_in.shape
    w = GROUP_W
    n_gla = 4 * w + GLA_LOWRANK
    w_proj = jnp.concatenate([w_in[:, :, :n_gla], jnp.zeros((depth, d, LR_PAD - GLA_LOWRANK), F32),
                              w_in[:, :, n_gla:]], axis=2)
    wg = jnp.concatenate([gla_w_gate, jnp.zeros((depth, 2, LR_PAD - GLA_LOWRANK, w), F32)], axis=2)
    zrows = jnp.zeros((depth, SUBLANE - 2, w), F32)
    par_gla = jnp.concatenate([gla_b_gate, zrows], axis=1)
    par_hg = jnp.concatenate([jnp.stack([1.0 - hg_lb, jnp.log(hg_lb), jnp.log1p(-hg_lb)], axis=1),
                              jnp.zeros((depth, SUBLANE - 3, w), F32)], axis=1)
    return dict(w_proj=w_proj.astype(BF16), wg=wg.astype(BF16), par_gla=par_gla, par_hg=par_hg,
                w_out=w_out.astype(BF16), w1=ffn_w1.astype(BF16), w3=ffn_w3.astype(BF16), w2=ffn_w2.astype(BF16))


def _trunk_layer(x, p, sw, mod4, layer, row0, seq_shape, s_gla, s_rg, s_hg, stream_consts, final, final_g):
    bm, lm, d = x.shape
    b, l = seq_shape
    gated_consts, pe, tables, bb = stream_consts
    outs = _norm_proj(x, mod4, layer, row0, p["norm1"], sw["w_proj"], sw["wg"], sw["par_gla"], sw["par_hg"])
    pa, pa_la, pb, pc, pd, pd_la = (t.reshape(b, l, t.shape[-1]) for t in outs)

    ya, st_a = _gated_mixer(pa, pa_la, p["gla_gain"], _state_to_blockdiag_t(s_gla), gated_consts, "gla")
    yd, st_d = _gated_mixer(pd, pd_la, p["hg_gain"], _state_to_blockdiag_t(s_hg), gated_consts, "hg")
    yb, st_b = _rglru(pb, p["rg_cw"], p["rg_cb"], p["rg_w"], p["rg_bias"], p["rg_nsp"], s_rg)

    filt = _hy_filters(pe, p["hy_w1"], p["hy_b1"], p["hy_w2"], p["hy_b2"], p["hy_w3"], p["hy_decay"])
    hspec = _hy_filter_dft(filt, tables)
    z1 = _hy_long_conv(pc, None, 0, 1, p["hy_cw"], p["hy_cb"], hspec, 0, p["hy_skip"], 0, tables, bb)
    yc = _hy_long_conv(pc, z1, 0, 2, p["hy_cw"], p["hy_cb"], hspec, 1, p["hy_skip"], 1, tables, bb)

    ys = [t.reshape(bm, lm, GROUP_W) for t in (ya, yb, yc, yd)]
    x = _out_ffn(x, ys, mod4, layer, row0, p["norm2"], final_g, sw["w_out"], sw["w1"], sw["w3"], sw["w2"], final)
    return x, (_blockdiag_t_to_state(st_a), st_b, _blockdiag_t_to_state(st_d))


def kernel(x_prompt, x_sample, state_gla, state_rglru, state_hgrn, c, c_ctx, norm1_g, norm2_g, final_norm_g, w_mod, b_mod, w_in, w_out, gla_w_gate, gla_b_gate, gla_norm_g, rg_conv_w, rg_conv_b, rg_w_a, rg_b_a, rg_w_x, rg_b_x, rg_lambda, hy_conv_w, hy_conv_b, hy_w1, hy_b1, hy_w2, hy_b2, hy_w3, hy_decay, hy_skip, hg_lower, hg_norm_g, ffn_w1, ffn_w3, ffn_w2):
    depth = w_in.shape[0]
    nb, seq, d = x_prompt.shape
    db, dseq, _ = x_sample.shape

    hg_lb = jnp.cumsum(jax.nn.softmax(hg_lower.astype(F32), axis=0), axis=0)
    hg_lb = hg_lb - hg_lb[0:1]

    cvec = jnp.concatenate([c_ctx[None, :], c, jnp.zeros((SUBLANE - 1 - db, d), F32)], axis=0)
    mod4 = _modulation(cvec, w_mod, b_mod).reshape(depth, SUBLANE, 1, N_MOD * d)

    gated_consts = _gated_consts()
    consts_p = (gated_consts, _hy_pos_features(seq), _dft_tables(seq), 16)
    consts_s = (gated_consts, _hy_pos_features(dseq), _dft_tables(dseq), 2)

    xp = x_prompt.reshape(1, nb * seq, d)
    xs = _add_pos(x_sample, _grid_position_encoding(max(dseq // GRID_W, GRID_W), d))
    zero_gla = jnp.zeros((nb, 2, N_HEADS, HEAD_D, HEAD_D), F32)
    zero_rg = jnp.zeros((nb, 2, GROUP_W), F32)
    final_g = final_norm_g[None, :]

    sw = _stacked_weights(hg_lb, w_in, w_out, gla_w_gate, gla_b_gate, ffn_w1, ffn_w3, ffn_w2)
    gla_states, rg_states, hg_states = [], [], []
    for l in range(depth):
        p = _layer_params(l, norm1_g, norm2_g, gla_norm_g,
                          rg_conv_w, rg_conv_b, rg_w_a, rg_b_a, rg_w_x, rg_b_x, rg_lambda,
                          hy_conv_w, hy_conv_b, hy_w1, hy_b1, hy_w2, hy_b2, hy_w3, hy_decay, hy_skip, hg_norm_g)
        final = l == depth - 1
        xp, (sg, sr, sh) = _trunk_layer(xp, p, sw, mod4, l, 0, (nb, seq), zero_gla, zero_rg, zero_gla,
                                        consts_p, final, final_g)
        xs, _ = _trunk_layer(xs, p, sw, mod4, l, 1, (db, dseq), state_gla[:, l], state_rglru[:, l],
                             state_hgrn[:, l], consts_s, final, final_g)
        gla_states.append(sg)
        rg_states.append(sr)
        hg_states.append(sh)

    return (xp.reshape(nb, seq, d), xs,
            jnp.stack(gla_states, axis=1), jnp.stack(rg_states, axis=1), jnp.stack(hg_states, axis=1))
```

```python
import functools
import math

import jax
import jax.numpy as jnp
from jax import lax
from jax.experimental import pallas as pl
from jax.experimental.pallas import tpu as pltpu

F32 = jnp.float32
BF16 = jnp.bfloat16

D_MODEL = 1024
N_MOD = 6
GROUP_W = 256
N_HEADS = 4
HEAD_D = GROUP_W // N_HEADS
GLA_LOWRANK = 16
GLA_GATE_TEMP = 16.0
RG_C = 8.0
RG_CONV_W = 4
HY_POS_BANDS = 16
HY_FFN_W = 64
D_FF = 2816
EPS = 1e-6
GRID_W = 64
POS_BASE = 10000.0

LANE = 128
SUBLANE = 8
VMEM_LIMIT = 56 * 1024 * 1024

LOG2E = 1.0 / math.log(2.0)
FAST_LOG2_RANGE = 96.0
SUB = 16
LC = 256
N_SUB = LC // SUB
BOUNDED_BLKS = (64, 32)
LR_PAD = LANE
W_GLA = 4 * GROUP_W + LR_PAD
W_RG = 2 * GROUP_W
W_HY = 3 * GROUP_W
W_HG = 5 * GROUP_W
W_PROJ = W_GLA + W_RG + W_HY + W_HG
W_GLA_MM = 4 * GROUP_W
W_HG_MM = 5 * GROUP_W
W_DECAY = 2 * GROUP_W
GATED_COLS = {"gla": dict(q=0, k=(1, 1), v=2, gate=3, la=(0, 1)),
              "hg": dict(q=0, k=(1, 2), v=3, gate=4, la=(0, 1))}
TM = 512
TF = D_FF // 2
TN_MOD = 512
RG_HALF = GROUP_W // 2
RG_ROWS = 256
FREQ_TILE = 256
TWID = 64


def _dot(a, b):
    return jnp.dot(a, b, preferred_element_type=F32)


def _split(x):
    hi = x.astype(BF16)
    lo = (x - hi.astype(F32)).astype(BF16)
    return hi, lo


def _dot_x2(x, w):
    hi, lo = _split(x)
    return _dot(hi, w) + _dot(lo, w)


def _dot_w2(w, x):
    hi, lo = _split(x)
    return _dot(w, hi) + _dot(w, lo)


def _dot3(a, b):
    ah, al = _split(a)
    bh, bl = _split(b)
    return _dot(ah, bh) + _dot(al, bh) + _dot(ah, bl)


def _sigmoid(x):
    return 0.5 * jnp.tanh(0.5 * x) + 0.5


def _silu(x):
    return x * _sigmoid(x)


def _log1p_exp_neg_abs(x):
    return jnp.log(1.0 + jnp.exp(-jnp.abs(x)))


def _log_sigmoid(x):
    return jnp.minimum(x, 0.0) - _log1p_exp_neg_abs(x)


def _rms(x):
    return x * lax.rsqrt(jnp.mean(x * x, axis=-1, keepdims=True) + EPS)


def _params(sem, vmem=VMEM_LIMIT):
    return pltpu.CompilerParams(dimension_semantics=sem, vmem_limit_bytes=vmem)


def _mod_kernel(c_ref, w_ref, b_ref, o_ref):
    c = c_ref[...]
    o_ref[0] = _dot3(_silu(c), w_ref[0]) + b_ref[0]


def _modulation(cvec, w_mod, b_mod):
    depth = w_mod.shape[0]
    n = N_MOD * D_MODEL
    return pl.pallas_call(
        _mod_kernel,
        out_shape=jax.ShapeDtypeStruct((depth, SUBLANE, n), F32),
        grid=(depth, n // TN_MOD),
        in_specs=[
            pl.BlockSpec((SUBLANE, D_MODEL), lambda l, j: (0, 0)),
            pl.BlockSpec((1, D_MODEL, TN_MOD), lambda l, j: (l, 0, j)),
            pl.BlockSpec((1, 1, TN_MOD), lambda l, j: (l, 0, j)),
        ],
        out_specs=pl.BlockSpec((1, SUBLANE, TN_MOD), lambda l, j: (l, 0, j)),
        compiler_params=_params(("parallel", "parallel")),
        name="modulation",
    )(cvec, w_mod, b_mod.reshape(depth, 1, n))


def _add_kernel(x_ref, e_ref, o_ref):
    j = pl.program_id(1)
    half = D_MODEL // 2
    enc = e_ref[0:GRID_W, :]
    for r in range(TM // GRID_W):
        rows = slice(r * GRID_W, (r + 1) * GRID_W)
        enc_row = e_ref[pl.ds(j * (TM // GRID_W) + r, 1), :]
        o_ref[0, rows, 0:half] = x_ref[0, rows, 0:half] + enc_row
        o_ref[0, rows, half:] = x_ref[0, rows, half:] + enc


def _add_pos(x, enc):
    b, l, d = x.shape
    assert l // GRID_W <= enc.shape[0]
    return pl.pallas_call(
        _add_kernel,
        out_shape=jax.ShapeDtypeStruct(x.shape, F32),
        grid=(b, l // TM),
        in_specs=[pl.BlockSpec((1, TM, d), lambda i, j: (i, j, 0)),
                  pl.BlockSpec(enc.shape, lambda i, j: (0, 0))],
        out_specs=pl.BlockSpec((1, TM, d), lambda i, j: (i, j, 0)),
        compiler_params=_params(("parallel", "parallel")),
        name="add_pos",
    )(x, enc)


def _gla_features(p, wg, par):
    w = GROUP_W
    lr = p[:, 4 * w:4 * w + LR_PAD].astype(BF16)
    las = [_log_sigmoid(_dot(lr, wg[d]) + par[d:d + 1, :]) * (LOG2E / GLA_GATE_TEMP) for d in range(2)]
    return [p[:, 0:w] * (HEAD_D ** -0.5), p[:, w:2 * w], p[:, 2 * w:3 * w], _silu(p[:, 3 * w:4 * w])] + las


def _hg_features(p, par):
    w = GROUP_W
    ks, las = [], []
    for d in range(2):
        f = p[:, (1 + d) * w:(2 + d) * w]
        e = jnp.exp(-jnp.abs(f))
        ks.append(par[0:1, :] * (jnp.where(f >= 0.0, e, 1.0) / (1.0 + e)))
        y = par[2:3, :] + (jnp.minimum(f, 0.0) - jnp.log(1.0 + e))
        lb = par[1:2, :]
        las.append((jnp.maximum(lb, y) + _log1p_exp_neg_abs(lb - y)) * LOG2E)
    return [_silu(p[:, 0:w]), ks[0], ks[1], p[:, 3 * w:4 * w], _silu(p[:, 4 * w:5 * w]), las[0], las[1]]


def _proj_kernel(x_ref, mod_ref, g_ref, w_ref, wg_ref, pg_ref, ph_ref,
                 oa_ref, oal_ref, ob_ref, oc_ref, od_ref, odl_ref):
    x = x_ref[0]
    m = mod_ref[0, 0]
    sh = m[:, 0:D_MODEL]
    sc = m[:, D_MODEL:2 * D_MODEL]
    u = (_rms(x) * g_ref[...] * (1.0 + sc) + sh).astype(BF16)
    w = GROUP_W
    bounds = (0, W_GLA, W_GLA + W_RG, W_GLA + W_RG + W_HY, W_PROJ)
    p = _dot(u, w_ref[0])
    p_gla, p_rg, p_hy, p_hg = (p[:, lo:hi] for lo, hi in zip(bounds[:-1], bounds[1:]))

    def put(feats, mm_ref, decay_ref):
        n_mm = len(feats) - 2
        for i, t in enumerate(feats[:n_mm]):
            mm_ref[0, :, i * w:(i + 1) * w] = t.astype(BF16)
        for i, t in enumerate(feats[n_mm:]):
            decay_ref[0, :, i * w:(i + 1) * w] = t

    put(_gla_features(p_gla, wg_ref[0], pg_ref[0]), oa_ref, oal_ref)
    ob_ref[0] = p_rg
    oc_ref[0] = p_hy
    put(_hg_features(p_hg, ph_ref[0]), od_ref, odl_ref)


def _norm_proj(x, mod4, layer, row0, gain, w, wg, par_gla, par_hg):
    bm, lm, d = x.shape
    widths = (W_GLA_MM, W_DECAY, W_RG, W_HY, W_HG_MM, W_DECAY)
    dtypes = (BF16, F32, F32, F32, BF16, F32)
    per_layer = lambda a: pl.BlockSpec((1,) + a.shape[1:], lambda i, j: (layer,) + (0,) * (a.ndim - 1))
    return pl.pallas_call(
        _proj_kernel,
        out_shape=[jax.ShapeDtypeStruct((bm, lm, wd), dt) for wd, dt in zip(widths, dtypes)],
        grid=(bm, lm // TM),
        in_specs=[
            pl.BlockSpec((1, TM, d), lambda i, j: (i, j, 0)),
            pl.BlockSpec((1, 1, 1, N_MOD * d), lambda i, j: (layer, row0 + i, 0, 0)),
            pl.BlockSpec((1, d), lambda i, j: (0, 0)),
            per_layer(w), per_layer(wg), per_layer(par_gla), per_layer(par_hg),
        ],
        out_specs=[pl.BlockSpec((1, TM, wd), lambda i, j: (i, j, 0)) for wd in widths],
        compiler_params=_params(("parallel", "parallel")),
        name="norm_proj",
    )(x, mod4, gain, w, wg, par_gla, par_hg)


def _ffn_kernel(x_ref, ya_ref, yb_ref, yc_ref, yd_ref, mod_ref, g2_ref, gf_ref,
                wo_ref, w1_ref, w3_ref, w2_ref, o_ref, x1_s, u_s, acc_s, *, nf, final):
    f = pl.program_id(2)
    d = D_MODEL

    @pl.when(f == 0)
    def _():
        m = mod_ref[0, 0]
        g1 = m[:, 2 * d:3 * d]
        sh2 = m[:, 3 * d:4 * d]
        sc2 = m[:, 4 * d:5 * d]
        y = jnp.concatenate([ya_ref[0], yb_ref[0], yc_ref[0], yd_ref[0]], axis=-1)
        x1 = x_ref[0] + g1 * _dot(y.astype(BF16), wo_ref[...])
        x1_s[...] = x1
        u_s[...] = (_rms(x1) * g2_ref[...] * (1.0 + sc2) + sh2).astype(BF16)
        acc_s[...] = jnp.zeros_like(acc_s)

    u = u_s[...]
    h = _silu(_dot(u, w1_ref[...])) * _dot(u, w3_ref[...])
    acc_s[...] += _dot(h.astype(BF16), w2_ref[...])

    @pl.when(f == nf - 1)
    def _():
        g2 = mod_ref[0, 0][:, 5 * d:6 * d]
        xo = x1_s[...] + g2 * acc_s[...]
        if final:
            xo = _rms(xo) * gf_ref[...]
        o_ref[0] = xo


def _out_ffn(x, ys, mod4, layer, row0, g2, gf, wo, w1, w3, w2, final):
    bm, lm, d = x.shape
    nf = D_FF // TF
    tok = lambda wd: pl.BlockSpec((1, TM, wd), lambda i, j, f: (i, j, 0))
    return pl.pallas_call(
        functools.partial(_ffn_kernel, nf=nf, final=final),
        out_shape=jax.ShapeDtypeStruct(x.shape, F32),
        grid=(bm, lm // TM, nf),
        in_specs=[
            tok(d), tok(GROUP_W), tok(GROUP_W), tok(GROUP_W), tok(GROUP_W),
            pl.BlockSpec((1, 1, 1, N_MOD * d), lambda i, j, f: (layer, row0 + i, 0, 0)),
            pl.BlockSpec((1, d), lambda i, j, f: (0, 0)),
            pl.BlockSpec((1, d), lambda i, j, f: (0, 0)),
            pl.BlockSpec((None, d, d), lambda i, j, f: (layer, 0, 0)),
            pl.BlockSpec((None, d, TF), lambda i, j, f: (layer, 0, f)),
            pl.BlockSpec((None, d, TF), lambda i, j, f: (layer, 0, f)),
            pl.BlockSpec((None, TF, d), lambda i, j, f: (layer, f, 0)),
        ],
        out_specs=tok(d),
        scratch_shapes=[pltpu.VMEM((TM, d), F32), pltpu.VMEM((TM, d), BF16), pltpu.VMEM((TM, d), F32)],
        compiler_params=_params(("parallel", "parallel", "arbitrary")),
        name="out_ffn",
    )(x, *ys, mod4, g2, gf, wo, w1, w3, w2)


def _operand(p_ref, mode, name, d, rows):
    col = GATED_COLS[mode][name]
    col = col[d] if isinstance(col, tuple) else col
    ref = p_ref[1] if name == "la" else p_ref[0]
    return ref[0, rows, col * GROUP_W:(col + 1) * GROUP_W].astype(F32)


def _gated_subblock(d, i, p_ref, mode, ones_ref, mbd_ref, b_s, p_s, o_s, st_s):
    w = GROUP_W
    rev = d == 1
    tio = lax.broadcasted_iota(jnp.int32, (SUB, w), 0)
    edge = 0 if rev else SUB - 1
    rows = pl.ds(pl.multiple_of(i * SUB, SUB), SUB)
    bb = b_s[d, rows, :]
    qb = _operand(p_ref, mode, "q", d, rows)
    kb = _operand(p_ref, mode, "k", d, rows)
    vb = _operand(p_ref, mode, "v", d, rows)

    def row(a, s):
        return jnp.broadcast_to(a[s:s + 1, :], (SUB, w))

    for s in range(SUB):
        valid = (tio <= s) if rev else (tio >= s)
        e = jnp.exp2(jnp.where(valid, bb - row(bb, s), -jnp.inf))
        p_s[d, s * SUB:(s + 1) * SUB, :] = (e * qb * row(kb, s)).astype(BF16)
    r = _dot(p_s[d], ones_ref[...])
    od = jnp.zeros((SUB, w), F32)
    for s in range(SUB):
        od = od + r[s * SUB:(s + 1) * SUB, :] * row(vb, s)
    bend = row(bb, edge)
    st = st_s[d]
    qt = (qb * jnp.exp2(bb)).astype(BF16)
    oi = lax.dot_general(qt, st.astype(BF16), (((1,), (1,)), ((), ())), preferred_element_type=F32)
    kt = (kb * jnp.exp2(bend - bb)).astype(BF16)
    kv = lax.dot_general(vb.astype(BF16), kt, (((0,), (0,)), ((), ())), preferred_element_type=F32)
    st_s[d] = st * jnp.exp2(bend[0:1, :]) + kv * mbd_ref[...]
    o_s[d, rows, :] = od + oi


def _stack_heads(x):
    lane_head = lax.broadcasted_iota(jnp.int32, x.shape, 1) // HEAD_D
    return jnp.concatenate([jnp.where(lane_head == h, x, 0.0) for h in range(N_HEADS)], axis=0).astype(BF16)


def _centred_span(b_s, blk):
    span = None
    for d in range(2):
        for g in range(LC // blk):
            r0, rm, r1 = g * blk, g * blk + blk // 2, (g + 1) * blk - 1
            mid = b_s[d, rm:rm + 1, :]
            m = jnp.maximum(jnp.abs(b_s[d, r0:r0 + 1, :] - mid), jnp.abs(b_s[d, r1:r1 + 1, :] - mid))
            span = m if span is None else jnp.maximum(span, m)
    return jnp.max(span)


def _gated_block_bounded(d, g, blk, p_ref, mode, b_s, o_s, st_s):
    w = GROUP_W
    rev = d == 1
    edge = 0 if rev else blk - 1
    rows = slice(g * blk, (g + 1) * blk)
    bb = b_s[d, rows, :]
    qb = _operand(p_ref, mode, "q", d, rows)
    kb = _operand(p_ref, mode, "k", d, rows)
    cc = bb - jnp.broadcast_to(bb[blk // 2:blk // 2 + 1, :], bb.shape)
    vm = _stack_heads(_operand(p_ref, mode, "v", d, rows))
    km = _stack_heads(kb * jnp.exp2(-cc))
    sc = lax.dot_general((qb * jnp.exp2(cc)).astype(BF16), km, (((1,), (1,)), ((), ())),
                         preferred_element_type=F32)
    t_i = lax.broadcasted_iota(jnp.int32, sc.shape, 0)
    s_i = lax.broadcasted_iota(jnp.int32, sc.shape, 1) % blk
    od = _dot(jnp.where((t_i <= s_i) if rev else (t_i >= s_i), sc, 0.0).astype(BF16), vm)
    bend = jnp.broadcast_to(bb[edge:edge + 1, :], (blk, w))
    st = st_s[d]
    oi = lax.dot_general((qb * jnp.exp2(bb)).astype(BF16), st.astype(BF16), (((1,), (1,)), ((), ())),
                         preferred_element_type=F32)
    ktm = _stack_heads(kb * jnp.exp2(bend - bb))
    kv = lax.dot_general(vm, ktm, (((0,), (0,)), ((), ())), preferred_element_type=F32)
    st_s[d] = st * jnp.exp2(bend[0:1, :]) + kv
    o_s[d, rows, :] = od + oi


def _gated_kernel(pf_mm_ref, pf_la_ref, pb_mm_ref, pb_la_ref, gain_ref, s0_ref, tri_ref, ones_ref, mbd_ref,
                  y_ref, st_ref, b_s, p_s, o_s, st_s, *, mode, nc):
    j = pl.program_id(1)
    pf_ref = (pf_mm_ref, pf_la_ref)
    pb_ref = (pb_mm_ref, pb_la_ref)
    p_refs = (pf_ref, pb_ref)
    all_rows = slice(0, LC)

    @pl.when(j == 0)
    def _():
        st_s[0] = s0_ref[0, 0]
        st_s[1] = s0_ref[0, 1]

    def cumulate(level):
        for d in range(2):
            b_s[d] = _dot_w2(tri_ref[level, d], _operand(p_refs[d], mode, "la", d, all_rows))

    def run_bounded(blk):
        for g in range(LC // blk):
            _gated_block_bounded(0, g, blk, pf_ref, mode, b_s, o_s, st_s)
            _gated_block_bounded(1, LC // blk - 1 - g, blk, pb_ref, mode, b_s, o_s, st_s)

    def run_exact():
        def body(it, carry):
            _gated_subblock(0, it, pf_ref, mode, ones_ref, mbd_ref, b_s, p_s, o_s, st_s)
            _gated_subblock(1, N_SUB - 1 - it, pb_ref, mode, ones_ref, mbd_ref, b_s, p_s, o_s, st_s)
            return carry

        lax.fori_loop(0, N_SUB, body, 0)

    cumulate(0)
    wide = _centred_span(b_s, BOUNDED_BLKS[0]) < FAST_LOG2_RANGE

    @pl.when(wide)
    def _():
        run_bounded(BOUNDED_BLKS[0])

    @pl.when(jnp.logical_not(wide))
    def _():
        cumulate(1)
        narrow = _centred_span(b_s, BOUNDED_BLKS[1]) < FAST_LOG2_RANGE

        @pl.when(narrow)
        def _():
            run_bounded(BOUNDED_BLKS[1])

        @pl.when(jnp.logical_not(narrow))
        def _():
            cumulate(2)
            run_exact()

    @pl.when(j == nc - 1)
    def _():
        for d in range(2):
            for h in range(N_HEADS):
                blk = pl.ds(h * HEAD_D, HEAD_D)
                st_ref[0, d, h] = st_s[d, blk, blk]

    def finish(o, p_ref):
        ms = _dot_x2(o * o, ones_ref[...]) * (1.0 / HEAD_D)
        return o * lax.rsqrt(ms + EPS) * gain_ref[...] * _operand(p_ref, mode, "gate", 0, all_rows)

    rows_f = pl.ds(pl.multiple_of(j * LC, LC), LC)
    rows_b = pl.ds(pl.multiple_of((nc - 1 - j) * LC, LC), LC)
    if nc == 1:
        y_ref[0] = finish(o_s[0] + o_s[1], pf_ref)
    else:
        @pl.when(j < nc // 2)
        def _():
            y_ref[0, rows_f, :] = o_s[0]
            y_ref[0, rows_b, :] = o_s[1]

        @pl.when(j >= nc // 2)
        def _():
            y_ref[0, rows_f, :] = finish(y_ref[0, rows_f, :] + o_s[0], pf_ref)
            y_ref[0, rows_b, :] = finish(y_ref[0, rows_b, :] + o_s[1], pb_ref)


def _gated_consts():
    r = jnp.arange(LC)

    def tri(block):
        same = (r[:, None] // block) == (r[None, :] // block)
        lower = (same & (r[None, :] <= r[:, None])).astype(BF16)
        upper = (same & (r[None, :] >= r[:, None])).astype(BF16)
        return jnp.stack([lower, upper])

    head = (r[:, None] // HEAD_D) == (r[None, :] // HEAD_D)
    tris = jnp.stack([tri(blk) for blk in BOUNDED_BLKS + (SUB,)])
    return tris, head.astype(BF16), head.astype(F32)


def _gated_mixer(p_mm, p_la, gain, s0t, consts, mode):
    b, l, width = p_mm.shape
    nc = l // LC
    assert nc == 1 or nc % 2 == 0
    tri, ones, mbd = consts
    w = GROUP_W
    const2 = lambda shape: pl.BlockSpec(shape, lambda i, j: (0,) * len(shape))
    return pl.pallas_call(
        functools.partial(_gated_kernel, mode=mode, nc=nc),
        out_shape=[jax.ShapeDtypeStruct((b, l, w), F32),
                   jax.ShapeDtypeStruct((b, 2, N_HEADS, HEAD_D, HEAD_D), F32)],
        grid=(b, nc),
        in_specs=[
            pl.BlockSpec((1, LC, width), lambda i, j: (i, j, 0)),
            pl.BlockSpec((1, LC, W_DECAY), lambda i, j: (i, j, 0)),
            pl.BlockSpec((1, LC, width), lambda i, j: (i, nc - 1 - j, 0)),
            pl.BlockSpec((1, LC, W_DECAY), lambda i, j: (i, nc - 1 - j, 0)),
            const2(gain.shape),
            pl.BlockSpec((1, 2, w, w), lambda i, j: (i, 0, 0, 0)),
            const2(tri.shape), const2(ones.shape), const2(mbd.shape),
        ],
        out_specs=[pl.BlockSpec((1, l, w), lambda i, j: (i, 0, 0)),
                   pl.BlockSpec((1, 2, N_HEADS, HEAD_D, HEAD_D), lambda i, j: (i, 0, 0, 0, 0))],
        scratch_shapes=[pltpu.VMEM((2, LC, w), F32), pltpu.VMEM((2, SUB * SUB, w), BF16),
                        pltpu.VMEM((2, LC, w), F32), pltpu.VMEM((2, w, w), F32)],
        compiler_params=_params(("parallel", "arbitrary")),
        name="gated_" + mode,
    )(p_mm, p_la, p_mm, p_la, gain, s0t, tri, ones, mbd)


def _state_to_blockdiag_t(s):
    b = s.shape[0]
    st = jnp.swapaxes(s, -1, -2)
    eye = jnp.eye(N_HEADS, dtype=s.dtype)
    full = st[:, :, :, :, None, :] * eye[None, None, :, None, :, None]
    return full.reshape(b, 2, GROUP_W, GROUP_W)


def _blockdiag_t_to_state(st):
    return jnp.swapaxes(st, -1, -2)


def _rglru_kernel(x_ref, g_ref, cw_ref, cb_ref, w_ref, bias_ref, nsp_ref, h0_ref, y_ref, hT_ref,
                  xc_s, a_s, u_s, *, l):
    c = RG_HALF
    x = x_ref[0]
    row = lax.broadcasted_iota(jnp.int32, (l, c), 0)
    xc = x * cw_ref[2:3, :] + cb_ref[...]
    xc = xc + jnp.where(row >= 2, pltpu.roll(x, 2, 0), 0.0) * cw_ref[0:1, :]
    xc = xc + jnp.where(row >= 1, pltpu.roll(x, 1, 0), 0.0) * cw_ref[1:2, :]
    xc = xc + jnp.where(row <= l - 2, pltpu.roll(x, l - 1, 0), 0.0) * cw_ref[3:4, :]
    xc_s[...] = xc

    nslab = l // RG_ROWS
    grp = (RG_ROWS // SUBLANE, SUBLANE, c)
    sub = lax.broadcasted_iota(jnp.int32, grp, 1)

    def slab(n, carry):
        rows = pl.ds(pl.multiple_of(n * RG_ROWS, RG_ROWS), RG_ROWS)
        xs = xc_s[rows, :]
        gates = _sigmoid(_dot(xs.astype(BF16), w_ref[...]) + bias_ref[...])
        for d in range(2):
            r = gates[:, (2 * d) * c:(2 * d + 1) * c]
            i = gates[:, (2 * d + 1) * c:(2 * d + 2) * c]
            log_a = r * nsp_ref[d:d + 1, :]
            a_flat = jnp.exp(log_a)
            a = a_flat.reshape(grp)
            u = (jnp.sqrt(1.0 - a_flat * a_flat) * (i * xs)).reshape(grp)
            for sft in (1, 2, 4):
                if d == 0:
                    ok = sub >= sft
                    a_n, u_n = pltpu.roll(a, sft, 1), pltpu.roll(u, sft, 1)
                else:
                    ok = sub <= SUBLANE - 1 - sft
                    a_n, u_n = pltpu.roll(a, SUBLANE - sft, 1), pltpu.roll(u, SUBLANE - sft, 1)
                u = jnp.where(ok, a * u_n + u, u)
                a = jnp.where(ok, a * a_n, a)
            a_s[d, rows, :] = a.reshape(RG_ROWS, c)
            u_s[d, rows, :] = u.reshape(RG_ROWS, c)
        return carry

    lax.fori_loop(0, nslab, slab, 0)

    ngrp = l // SUBLANE

    def carry_step(n, hs):
        hf, hb = hs
        rows_f = pl.ds(pl.multiple_of(n * SUBLANE, SUBLANE), SUBLANE)
        rows_b = pl.ds(pl.multiple_of((ngrp - 1 - n) * SUBLANE, SUBLANE), SUBLANE)
        hh_f = a_s[0, rows_f, :] * hf + u_s[0, rows_f, :]
        hh_b = a_s[1, rows_b, :] * hb + u_s[1, rows_b, :]
        u_s[0, rows_f, :] = hh_f
        u_s[1, rows_b, :] = hh_b
        return (jnp.broadcast_to(hh_f[SUBLANE - 1:SUBLANE, :], (SUBLANE, c)),
                jnp.broadcast_to(hh_b[0:1, :], (SUBLANE, c)))

    h_init = (jnp.broadcast_to(h0_ref[0, 0:1, :], (SUBLANE, c)), jnp.broadcast_to(h0_ref[0, 1:2, :], (SUBLANE, c)))
    hf, hb = lax.fori_loop(0, ngrp, carry_step, h_init, unroll=2)
    hT_ref[0, 0:1, :] = hf[0:1, :]
    hT_ref[0, 1:2, :] = hb[0:1, :]

    g = g_ref[0]
    gelu = 0.5 * g * (1.0 + jnp.tanh(math.sqrt(2.0 / math.pi) * (g + 0.044715 * (g * g * g))))
    y_ref[0] = (u_s[0] + u_s[1]) * gelu


def _rglru(p, cw, cb, wbd, bias, nsp, h0):
    b, l, _ = p.shape
    c = RG_HALF
    half = lambda shape: pl.BlockSpec(shape, lambda i, j: (0,) * (len(shape) - 1) + (j,))
    return pl.pallas_call(
        functools.partial(_rglru_kernel, l=l),
        out_shape=[jax.ShapeDtypeStruct((b, l, GROUP_W), F32), jax.ShapeDtypeStruct((b, 2, GROUP_W), F32)],
        grid=(b, 2),
        in_specs=[
            pl.BlockSpec((1, l, c), lambda i, j: (i, 0, j)),
            pl.BlockSpec((1, l, c), lambda i, j: (i, 0, 2 + j)),
            half((RG_CONV_W, c)), half((1, c)),
            pl.BlockSpec((None, c, 4 * c), lambda i, j: (j, 0, 0)),
            pl.BlockSpec((None, 1, 4 * c), lambda i, j: (j, 0, 0)),
            half((2, c)),
            pl.BlockSpec((1, 2, c), lambda i, j: (i, 0, j)),
        ],
        out_specs=[pl.BlockSpec((1, l, c), lambda i, j: (i, 0, j)),
                   pl.BlockSpec((1, 2, c), lambda i, j: (i, 0, j))],
        scratch_shapes=[pltpu.VMEM((l, c), F32), pltpu.VMEM((2, l, c), F32), pltpu.VMEM((2, l, c), F32)],
        compiler_params=_params(("parallel", "parallel")),
        name="rglru",
    )(p, p, cw, cb, wbd, bias, nsp, h0)


def _short_conv3(x, w_ref, b_ref):
    l = x.shape[0]
    row = lax.broadcasted_iota(jnp.int32, x.shape, 0)
    o = x * w_ref[1:2, :] + b_ref[...]
    o = o + jnp.where(row >= 1, pltpu.roll(x, 1, 0), 0.0) * w_ref[0:1, :]
    return o + jnp.where(row <= l - 2, pltpu.roll(x, l - 1, 0), 0.0) * w_ref[2:3, :]


def _hy_filter_kernel(pe_ref, w1_ref, b1_ref, w2_ref, b2_ref, w3_ref, dec_ref, o_ref):
    pe = pe_ref[...]
    h = jnp.sin(_dot3(pe, w1_ref[...]) + b1_ref[...])
    h = jnp.sin(_dot3(h, w2_ref[...]) + b2_ref[...])
    h = _dot3(h, w3_ref[...])
    dist = pe[:, LANE - 1:LANE]
    h = h * jnp.exp(-dist * dec_ref[...])
    o_ref[...] = h / jnp.sum(jnp.abs(h), axis=0, keepdims=True)


def _hy_filters(pe, w1, b1, w2, b2, w3, decay):
    l = pe.shape[0]
    n = w3.shape[1]
    args = (pe, w1, b1, w2, b2, w3, decay)
    return pl.pallas_call(
        _hy_filter_kernel,
        out_shape=jax.ShapeDtypeStruct((l, n), F32),
        grid=(1,),
        in_specs=[pl.BlockSpec(a.shape, lambda i: (0, 0)) for a in args],
        out_specs=pl.BlockSpec((l, n), lambda i: (0, 0)),
        compiler_params=_params(("arbitrary",)),
        name="hy_filters",
    )(*args)


def _hy_filter_dft_kernel(h_ref, ct_ref, st_ref, hc_ref, hs_ref, ht_s):
    @pl.when(pl.program_id(0) == 0)
    def _():
        ht_s[...] = h_ref[...].T.astype(BF16)

    hc_ref[...] = _dot(ht_s[...], ct_ref[...])
    hs_ref[...] = _dot(ht_s[...], st_ref[...])


def _hy_filter_dft(h, tables):
    l, n = h.shape
    ct, st, _, _ = tables
    nk = ct.shape[0]
    spec = jax.ShapeDtypeStruct((n, nk * FREQ_TILE), F32)
    return pl.pallas_call(
        _hy_filter_dft_kernel,
        out_shape=[spec, spec],
        grid=(nk,),
        in_specs=[pl.BlockSpec((l, n), lambda k: (0, 0)),
                  pl.BlockSpec((None, l, FREQ_TILE), lambda k: (k, 0, 0)),
                  pl.BlockSpec((None, l, FREQ_TILE), lambda k: (k, 0, 0))],
        out_specs=[pl.BlockSpec((n, FREQ_TILE), lambda k: (0, k)), pl.BlockSpec((n, FREQ_TILE), lambda k: (0, k))],
        scratch_shapes=[pltpu.VMEM((n, l), BF16)],
        compiler_params=_params(("arbitrary",)),
        name="hy_filter_dft",
    )(h, ct, st)


def _hy_conv_kernel(z_ref, x_ref, hc_ref, hs_ref, skip_ref, wz_ref, bz_ref, wx_ref, bx_ref,
                    ct_ref, st_ref, wc_ref, ws_ref, o_ref, zt_s, *, bb, nk, z_raw):
    kt = pl.program_id(1)
    c = GROUP_W

    def z_of(b):
        return _short_conv3(z_ref[b], wz_ref, bz_ref) if z_raw else z_ref[b]

    @pl.when(kt == 0)
    def _():
        o_ref[...] = jnp.zeros_like(o_ref)
        for b in range(bb):
            zt_s[b * c:(b + 1) * c, :] = z_of(b).T.astype(BF16)

    zt = zt_s[...]
    xc = _dot(zt, ct_ref[...])
    xs = _dot(zt, st_ref[...])
    hc = hc_ref[...]
    hs = hs_ref[...]
    wc = wc_ref[...]
    ws = ws_ref[...]
    for b in range(bb):
        zc = xc[b * c:(b + 1) * c, :]
        zsn = xs[b * c:(b + 1) * c, :]
        yc = zc * hc - zsn * hs
        ys = zc * hs + zsn * hc
        a = (wc * yc + ws * ys).T.astype(BF16)
        bm = (wc * ys - ws * yc).T.astype(BF16)
        o_ref[b] += _dot(ct_ref[...], a) + _dot(st_ref[...], bm)

    @pl.when(kt == nk - 1)
    def _():
        for b in range(bb):
            gate = _short_conv3(x_ref[b], wx_ref, bx_ref)
            o_ref[b] = gate * (o_ref[b] + skip_ref[...] * z_of(b))


def _hy_long_conv(pc, zprev, z_col, gate_col, cw, cb, hspec, h_col, skip, skip_col, tables, bb):
    b, l, _ = pc.shape
    c = GROUP_W
    ct, st, wc, ws = tables
    nk = ct.shape[0]
    z_raw = zprev is None
    z_arr = pc if z_raw else zprev
    seq_mode = pl.Buffered(1) if b == bb else None
    return pl.pallas_call(
        functools.partial(_hy_conv_kernel, bb=bb, nk=nk, z_raw=z_raw),
        out_shape=jax.ShapeDtypeStruct((b, l, c), F32),
        grid=(b // bb, nk),
        in_specs=[
            pl.BlockSpec((bb, l, c), lambda g, k: (g, 0, z_col), pipeline_mode=seq_mode),
            pl.BlockSpec((bb, l, c), lambda g, k: (g, 0, gate_col), pipeline_mode=seq_mode),
            pl.BlockSpec((c, FREQ_TILE), lambda g, k: (h_col, k)),
            pl.BlockSpec((c, FREQ_TILE), lambda g, k: (h_col, k)),
            pl.BlockSpec((1, c), lambda g, k: (0, skip_col)),
            pl.BlockSpec((3, c), lambda g, k: (0, z_col)),
            pl.BlockSpec((1, c), lambda g, k: (0, z_col)),
            pl.BlockSpec((3, c), lambda g, k: (0, gate_col)),
            pl.BlockSpec((1, c), lambda g, k: (0, gate_col)),
            pl.BlockSpec((None, l, FREQ_TILE), lambda g, k: (k, 0, 0)),
            pl.BlockSpec((None, l, FREQ_TILE), lambda g, k: (k, 0, 0)),
            pl.BlockSpec((1, FREQ_TILE), lambda g, k: (0, k)),
            pl.BlockSpec((1, FREQ_TILE), lambda g, k: (0, k)),
        ],
        out_specs=pl.BlockSpec((bb, l, c), lambda g, k: (g, 0, 0)),
        scratch_shapes=[pltpu.VMEM((bb * c, l), BF16)],
        compiler_params=_params(("parallel", "arbitrary")),
        name="hy_long_conv",
    )(z_arr, pc, hspec[0], hspec[1], skip, cw, cb, cw, cb, ct, st, wc, ws)


def _dft_tables(l):
    n = 3 * l // 2
    nf = n // 2 + 1
    nfp = -(-nf // FREQ_TILE) * FREQ_TILE
    k = jnp.arange(nfp, dtype=jnp.int32)
    ta = jnp.arange(l // TWID, dtype=jnp.int32) * TWID
    tb = jnp.arange(TWID, dtype=jnp.int32)
    live = (k < nf)

    def cos_sin(m):
        ang = (m % n).astype(F32) * (2.0 * math.pi / n)
        return jnp.cos(ang), jnp.sin(ang)

    def tiles(x):
        return x.reshape(x.shape[0], nfp // FREQ_TILE, FREQ_TILE).transpose(1, 0, 2)

    ca, sa = cos_sin(ta[:, None] * k[None, :])
    cb, sb = cos_sin(tb[:, None] * k[None, :])
    ca, sa = (tiles(jnp.where(live[None, :], x, 0.0))[:, :, None, :] for x in (ca, sa))
    cb, sb = tiles(cb)[:, None, :, :], tiles(sb)[:, None, :, :]
    shape = (nfp // FREQ_TILE, l, FREQ_TILE)
    ct = (ca * cb - sa * sb).reshape(shape).astype(BF16)
    st = (sa * cb + ca * sb).reshape(shape).astype(BF16)
    wk = jnp.where((k == 0) | (k == n // 2), 1.0, 2.0) / n
    wk = jnp.where(live, wk, 0.0).astype(F32)
    cp, sp = cos_sin(k * (l // 2))
    return ct, st, (wk * cp)[None, :], (wk * sp)[None, :]


def _hy_pos_features(l):
    pos = jnp.arange(l, dtype=F32)
    t = pos / (l - 1)
    ang = (2.0 * jnp.pi * pos / l)[:, None] * jnp.linspace(1e-4, HY_POS_BANDS - 1, HY_POS_BANDS, dtype=F32)[None, :]
    half = l // 2
    dist = jnp.abs(pos - half) / half
    pe = jnp.concatenate([t[:, None], jnp.cos(ang), -jnp.sin(ang)], axis=-1)
    pad = jnp.zeros((l, LANE - 1 - pe.shape[1]), F32)
    return jnp.concatenate([pe, pad, dist[:, None]], axis=-1)


def _grid_position_encoding(n_pos, dim):
    quarter = dim // 4
    omega = 1.0 / (POS_BASE ** (jnp.arange(quarter, dtype=F32) / quarter))
    ang = jnp.arange(n_pos, dtype=F32)[:, None] * omega[None, :]
    return jnp.concatenate([jnp.sin(ang), jnp.cos(ang)], axis=-1)


def _blockdiag2(a, b):
    z = jnp.zeros_like(a)
    return jnp.concatenate([jnp.concatenate([a, z], axis=1), jnp.concatenate([z, b], axis=1)], axis=0)


def _layer_params(l, norm1_g, norm2_g, gla_norm_g,
                  rg_conv_w, rg_conv_b, rg_w_a, rg_b_a, rg_w_x, rg_b_x, rg_lambda,
                  hy_conv_w, hy_conv_b, hy_w1, hy_b1, hy_w2, hy_b2, hy_w3, hy_decay, hy_skip, hg_norm_g):
    wa, wx = rg_w_a[l], rg_w_x[l]
    rg_w = jnp.stack([
        jnp.concatenate([_blockdiag2(m[dd, 2 * j], m[dd, 2 * j + 1]) for dd in range(2) for m in (wa, wx)], axis=1)
        for j in range(2)])
    ba, bx = rg_b_a[l], rg_b_x[l]
    rg_bias = jnp.stack([
        jnp.concatenate([v[dd, j * RG_HALF:(j + 1) * RG_HALF] for dd in range(2) for v in (ba, bx)])[None, :]
        for j in range(2)])
    nsp = -RG_C * jax.nn.softplus(-rg_lambda[l])
    w1p = jnp.concatenate([hy_w1[l], jnp.zeros((LANE - hy_w1.shape[1], HY_FFN_W), F32)], axis=0)
    return dict(
        norm1=norm1_g[l][None, :], norm2=norm2_g[l][None, :],
        gla_gain=gla_norm_g[l][None, :], hg_gain=hg_norm_g[l][None, :],
        rg_cw=rg_conv_w[l], rg_cb=rg_conv_b[l][None, :], rg_w=rg_w.astype(BF16), rg_bias=rg_bias, rg_nsp=nsp,
        hy_cw=hy_conv_w[l], hy_cb=hy_conv_b[l][None, :], hy_w1=w1p, hy_b1=hy_b1[l][None, :],
        hy_w2=hy_w2[l], hy_b2=hy_b2[l][None, :], hy_w3=hy_w3[l], hy_decay=hy_decay[l][None, :],
        hy_skip=hy_skip[l][None, :])


def _stacked_weights(hg_lb, w_in, w_out, gla_w_gate, gla_b_gate, ffn_w1, ffn_w3, ffn_w2):
    depth, d, _ = w_in.shape
    w = GROUP_W
    n_gla = 4 * w + GLA_LOWRANK
    w_proj = jnp.concatenate([w_in[:, :, :n_gla], jnp.zeros((depth, d, LR_PAD - GLA_LOWRANK), F32),
                              w_in[:, :, n_gla:]], axis=2)
    wg = jnp.concatenate([gla_w_gate, jnp.zeros((depth, 2, LR_PAD - GLA_LOWRANK, w), F32)], axis=2)
    zrows = jnp.zeros((depth, SUBLANE - 2, w), F32)
    par_gla = jnp.concatenate([gla_b_gate, zrows], axis=1)
    par_hg = jnp.concatenate([jnp.stack([1.0 - hg_lb, jnp.log(hg_lb), jnp.log1p(-hg_lb)], axis=1),
                              jnp.zeros((depth, SUBLANE - 3, w), F32)], axis=1)
    return dict(w_proj=w_proj.astype(BF16), wg=wg.astype(BF16), par_gla=par_gla, par_hg=par_hg,
                w_out=w_out.astype(BF16), w1=ffn_w1.astype(BF16), w3=ffn_w3.astype(BF16), w2=ffn_w2.astype(BF16))


def _trunk_layer(x, p, sw, mod4, layer, row0, seq_shape, s_gla, s_rg, s_hg, stream_consts, final, final_g):
    bm, lm, d = x.shape
    b, l = seq_shape
    gated_consts, pe, tables, bb = stream_consts
    outs = _norm_proj(x, mod4, layer, row0, p["norm1"], sw["w_proj"], sw["wg"], sw["par_gla"], sw["par_hg"])
    pa, pa_la, pb, pc, pd, pd_la = (t.reshape(b, l, t.shape[-1]) for t in outs)

    ya, st_a = _gated_mixer(pa, pa_la, p["gla_gain"], _state_to_blockdiag_t(s_gla), gated_consts, "gla")
    yd, st_d = _gated_mixer(pd, pd_la, p["hg_gain"], _state_to_blockdiag_t(s_hg), gated_consts, "hg")
    yb, st_b = _rglru(pb, p["rg_cw"], p["rg_cb"], p["rg_w"], p["rg_bias"], p["rg_nsp"], s_rg)

    filt = _hy_filters(pe, p["hy_w1"], p["hy_b1"], p["hy_w2"], p["hy_b2"], p["hy_w3"], p["hy_decay"])
    hspec = _hy_filter_dft(filt, tables)
    z1 = _hy_long_conv(pc, None, 0, 1, p["hy_cw"], p["hy_cb"], hspec, 0, p["hy_skip"], 0, tables, bb)
    yc = _hy_long_conv(pc, z1, 0, 2, p["hy_cw"], p["hy_cb"], hspec, 1, p["hy_skip"], 1, tables, bb)

    ys = [t.reshape(bm, lm, GROUP_W) for t in (ya, yb, yc, yd)]
    x = _out_ffn(x, ys, mod4, layer, row0, p["norm2"], final_g, sw["w_out"], sw["w1"], sw["w3"], sw["w2"], final)
    return x, (_blockdiag_t_to_state(st_a), st_b, _blockdiag_t_to_state(st_d))


def kernel(x_prompt, x_sample, state_gla, state_rglru, state_hgrn, c, c_ctx, norm1_g, norm2_g, final_norm_g, w_mod, b_mod, w_in, w_out, gla_w_gate, gla_b_gate, gla_norm_g, rg_conv_w, rg_conv_b, rg_w_a, rg_b_a, rg_w_x, rg_b_x, rg_lambda, hy_conv_w, hy_conv_b, hy_w1, hy_b1, hy_w2, hy_b2, hy_w3, hy_decay, hy_skip, hg_lower, hg_norm_g, ffn_w1, ffn_w3, ffn_w2):
    depth = w_in.shape[0]
    nb, seq, d = x_prompt.shape
    db, dseq, _ = x_sample.shape

    hg_lb = jnp.cumsum(jax.nn.softmax(hg_lower.astype(F32), axis=0), axis=0)
    hg_lb = hg_lb - hg_lb[0:1]

    cvec = jnp.concatenate([c_ctx[None, :], c, jnp.zeros((SUBLANE - 1 - db, d), F32)], axis=0)
    mod4 = _modulation(cvec, w_mod, b_mod).reshape(depth, SUBLANE, 1, N_MOD * d)

    gated_consts = _gated_consts()
    consts_p = (gated_consts, _hy_pos_features(seq), _dft_tables(seq), 16)
    consts_s = (gated_consts, _hy_pos_features(dseq), _dft_tables(dseq), 2)

    xp = x_prompt.reshape(1, nb * seq, d)
    xs = _add_pos(x_sample, _grid_position_encoding(max(dseq // GRID_W, GRID_W), d))
    zero_gla = jnp.zeros((nb, 2, N_HEADS, HEAD_D, HEAD_D), F32)
    zero_rg = jnp.zeros((nb, 2, GROUP_W), F32)
    final_g = final_norm_g[None, :]

    sw = _stacked_weights(hg_lb, w_in, w_out, gla_w_gate, gla_b_gate, ffn_w1, ffn_w3, ffn_w2)
    gla_states, rg_states, hg_states = [], [], []
    for l in range(depth):
        p = _layer_params(l, norm1_g, norm2_g, gla_norm_g,
                          rg_conv_w, rg_conv_b, rg_w_a, rg_b_a, rg_w_x, rg_b_x, rg_lambda,
                          hy_conv_w, hy_conv_b, hy_w1, hy_b1, hy_w2, hy_b2, hy_w3, hy_decay, hy_skip, hg_norm_g)
        final = l == depth - 1
        xp, (sg, sr, sh) = _trunk_layer(xp, p, sw, mod4, l, 0, (nb, seq), zero_gla, zero_rg, zero_gla,
                                        consts_p, final, final_g)
        xs, _ = _trunk_layer(xs, p, sw, mod4, l, 1, (db, dseq), state_gla[:, l], state_rglru[:, l],
                             state_hgrn[:, l], consts_s, final, final_g)
        gla_states.append(sg)
        rg_states.append(sr)
        hg_states.append(sh)

    return (xp.reshape(nb, seq, d), xs,
            jnp.stack(gla_states, axis=1), jnp.stack(rg_states, axis=1), jnp.stack(hg_states, axis=1))
```
